```python
import jax, jax.numpy as jnp
from jax import lax
import numpy as np

D_MODEL = 1024
BATCH = 2
SEQ = 8192
DEPTH = 1

CHUNK = 64
N_META = 16
Q_BLOCK = 128
SB_HEADS = 8
SB_HEAD_DIM = 64
SB_WIDTH = SB_HEADS * SB_HEAD_DIM
SB_SCALE = SB_HEAD_DIM ** -0.5
HG_HEADS = 4
HG_KEY_DIM = 128
HG_VAL_DIM = 128
HG_KEY_WIDTH = HG_HEADS * HG_KEY_DIM
HG_VAL_WIDTH = HG_HEADS * HG_VAL_DIM
HG_SCALE = HG_KEY_DIM ** -0.5
N_BRANCH = 2
SPLIT_SIZES = (SB_WIDTH, SB_WIDTH, SB_WIDTH, SB_WIDTH,
               HG_KEY_WIDTH, HG_KEY_WIDTH, HG_VAL_WIDTH, HG_VAL_WIDTH,
               D_MODEL, D_MODEL)
IN_COLS = sum(SPLIT_SIZES)
EPS = 1e-6

kernel_name = "hybrid_stickbreak_hgrn2_block"


def _rmsnorm(x, g):
    xf = x.astype(jnp.float32)
    y = xf * lax.rsqrt(jnp.mean(xf * xf, axis=-1, keepdims=True) + EPS)
    return (y * g.astype(jnp.float32)).astype(x.dtype)


def _group_rmsnorm(o, g):
    B, L, _ = o.shape
    oh = o.reshape(B, L, HG_HEADS, HG_VAL_DIM)
    oh = oh * lax.rsqrt(jnp.mean(oh * oh, axis=-1, keepdims=True) + EPS)
    return oh.reshape(B, L, HG_VAL_WIDTH) * g.astype(jnp.float32)


def _stick_breaking(q, k, v):
    B, L, _ = q.shape

    def heads(a):
        return a.astype(jnp.float32).reshape(B, L, SB_HEADS, SB_HEAD_DIM).transpose(0, 2, 1, 3)

    q, k, v = heads(q) * SB_SCALE, heads(k), heads(v)
    nb = L // Q_BLOCK
    q_blocks = q.reshape(B, SB_HEADS, nb, Q_BLOCK, SB_HEAD_DIM).transpose(2, 0, 1, 3, 4)
    starts = jnp.arange(nb, dtype=jnp.int32) * Q_BLOCK
    s_idx = jnp.arange(L, dtype=jnp.int32)

    def block(args):
        qb, t0 = args
        z = jnp.einsum('bhtd,bhsd->bhts', qb, k)
        t_idx = t0 + jnp.arange(Q_BLOCK, dtype=jnp.int32)
        valid = s_idx[None, :] < t_idx[:, None]
        log_beta = jax.nn.log_sigmoid(z)
        log_1mb = jnp.where(valid, jax.nn.log_sigmoid(-z), 0.0)
        later = lax.cumsum(log_1mb, axis=3, reverse=True) - log_1mb
        weights = jnp.where(valid, jnp.exp(log_beta + later), 0.0)
        return jnp.einsum('bhts,bhsd->bhtd', weights, v)

    out = lax.map(block, (q_blocks, starts))
    return out.transpose(1, 0, 3, 2, 4).reshape(B, L, SB_WIDTH)


def _hgrn2(q, f_logit, i, lb):
    B, L, _ = q.shape
    n = L // CHUNK
    f = lb + (1.0 - lb) * jax.nn.sigmoid(f_logit.astype(jnp.float32))
    kk = 1.0 - f
    g = jnp.log(f)
    qf = jax.nn.silu(q.astype(jnp.float32)) * HG_SCALE

    def chunks(a, dh):
        return a.reshape(B, n, CHUNK, HG_HEADS, dh).transpose(1, 0, 3, 2, 4)

    xs = (chunks(qf, HG_KEY_DIM), chunks(kk, HG_KEY_DIM),
          chunks(i.astype(jnp.float32), HG_VAL_DIM), chunks(g, HG_KEY_DIM))
    pos = jnp.arange(CHUNK)
    causal = pos[:, None] >= pos[None, :]

    def step(S, c):
        qc, kc, vc, gc = c
        b = jnp.cumsum(gc, axis=2)
        diff = b[:, :, :, None, :] - b[:, :, None, :, :]
        decay = jnp.exp(jnp.where(causal[:, :, None], diff, -jnp.inf))
        A = jnp.einsum('bhtk,bhtsk,bhsk->bhts', qc, decay, kc)
        o = (jnp.einsum('bhts,bhsv->bhtv', A, vc)
             + jnp.einsum('bhtk,bhkv->bhtv', qc * jnp.exp(b), S))
        b_last = b[:, :, -1:, :]
        S = (jnp.exp(b_last[:, :, 0, :])[..., None] * S
             + jnp.einsum('bhsk,bhsv->bhkv', kc * jnp.exp(b_last - b), vc))
        return S, o

    S0 = jnp.zeros((B, HG_HEADS, HG_KEY_DIM, HG_VAL_DIM), jnp.float32)
    _, o = lax.scan(step, S0, xs)
    return o.transpose(1, 0, 3, 2, 4).reshape(B, L, HG_VAL_WIDTH)


def setup_inputs(seed: int = 0) -> dict:
    key = jax.random.key(seed)
    ks = jax.random.split(key, 10)
    f32 = jnp.float32
    x = jax.random.normal(ks[0], (BATCH, SEQ, D_MODEL), f32)
    meta = jax.random.normal(ks[1], (N_META, D_MODEL), f32)
    norm_g = 1.0 + 0.02 * jax.random.normal(ks[2], (DEPTH, D_MODEL), f32)
    w_in = jax.random.normal(ks[3], (DEPTH, D_MODEL, IN_COLS), f32) * D_MODEL ** -0.5
    w_sb_out = jax.random.normal(ks[4], (DEPTH, SB_WIDTH, D_MODEL), f32) * SB_WIDTH ** -0.5
    w_hg_out = jax.random.normal(ks[5], (DEPTH, HG_VAL_WIDTH, D_MODEL), f32) * HG_VAL_WIDTH ** -0.5
    w_out = jax.random.normal(ks[6], (DEPTH, D_MODEL, D_MODEL), f32) * D_MODEL ** -0.5
    hg_norm_g = 1.0 + 0.02 * jax.random.normal(ks[7], (DEPTH, HG_VAL_WIDTH), f32)
    hg_lb_logits = 0.1 * jax.random.normal(ks[8], (DEPTH + 1, HG_KEY_WIDTH), f32)
    final_norm_g = 1.0 + 0.02 * jax.random.normal(ks[9], (D_MODEL,), f32)
    return {"x": x, "meta": meta, "norm_g": norm_g, "w_in": w_in, "w_sb_out": w_sb_out,
            "w_hg_out": w_hg_out, "w_out": w_out, "hg_norm_g": hg_norm_g,
            "hg_lb_logits": hg_lb_logits, "final_norm_g": final_norm_g}


def reference(x, meta, norm_g, w_in, w_sb_out, w_hg_out, w_out, hg_norm_g, hg_lb_logits, final_norm_g):
    B, S, D = x.shape
    dtype = x.dtype
    L = N_META + S
    Lp = ((L + Q_BLOCK - 1) // Q_BLOCK) * Q_BLOCK
    h = jnp.concatenate([jnp.broadcast_to(meta.astype(dtype)[None], (B, N_META, D)), x], axis=1)
    h = jnp.pad(h, ((0, 0), (0, Lp - L), (0, 0)))

    lb_all = jnp.cumsum(jax.nn.softmax(hg_lb_logits.astype(jnp.float32), axis=0), axis=0)
    bounds = np.cumsum(SPLIT_SIZES)[:-1].tolist()

    for l in range(DEPTH):
        u = _rmsnorm(h, norm_g[l])
        proj = u @ w_in[l]
        (sb_q, sb_k, sb_v, sb_gate, hg_q, hg_f, hg_i, hg_gate,
         gate_sb, gate_hg) = jnp.split(proj, bounds, axis=-1)

        y_sb = (_stick_breaking(sb_q, sb_k, sb_v) * jax.nn.silu(sb_gate.astype(jnp.float32))).astype(dtype)
        o_hg = _group_rmsnorm(_hgrn2(hg_q, hg_f, hg_i, lb_all[l]), hg_norm_g[l])
        y_hg = (o_hg * jax.nn.silu(hg_gate.astype(jnp.float32))).astype(dtype)

        merged = (jax.nn.sigmoid(gate_sb) * (y_sb @ w_sb_out[l])
                  + jax.nn.sigmoid(gate_hg) * (y_hg @ w_hg_out[l]))
        h = h + merged @ w_out[l]

    return _rmsnorm(h, final_norm_g)[:, N_META:N_META + S]
```

```python
import functools

import jax
import jax.numpy as jnp
import numpy as np
from jax import lax
from jax.experimental import pallas as pl
from jax.experimental.pallas import tpu as pltpu

F32 = jnp.float32
BF16 = jnp.bfloat16

D_MODEL = 1024
N_META = 16
SB_HEADS = 8
SB_HEAD_DIM = 64
SB_WIDTH = SB_HEADS * SB_HEAD_DIM
SB_SCALE = SB_HEAD_DIM ** -0.5
HG_HEADS = 4
HG_DIM = 128
HG_WIDTH = HG_HEADS * HG_DIM
HG_SCALE = HG_DIM ** -0.5
IN_COLS = 4 * SB_WIDTH + 4 * HG_WIDTH + 2 * D_MODEL
EPS = 1e-6

LANES = 128
VMEM_LIMIT_BYTES = 56 * 1024 * 1024

COL_TILE = 512
T_SB_Q, T_SB_K, T_SB_V, T_SB_GATE, T_HG_Q, T_HG_F, T_HG_I, T_HG_GATE = range(8)
N_COL_TILES = IN_COLS // COL_TILE

KEY_TILE = 128
META_ROWS = 128
HG_CHUNK = 64
HG_SUB = 16
HG_NSUB = HG_CHUNK // HG_SUB


def _sigmoid(x):
    return 1.0 / (1.0 + jnp.exp(-x))


def _split_bf16(x):
    hi = x.astype(BF16)
    lo = (x - hi.astype(F32)).astype(BF16)
    return hi, lo


def _inproj_kernel(x_ref, g_ref, w_ref, o_ref, f_ref, u_ref):
    n = pl.program_id(1)

    @pl.when(n == 0)
    def _():
        x = x_ref[...]
        ms = jnp.mean(x * x, axis=-1, keepdims=True)
        u_ref[...] = (x * lax.rsqrt(ms + EPS) * g_ref[...]).astype(BF16)

    p = jnp.dot(u_ref[...], w_ref[...].astype(BF16), preferred_element_type=F32)
    p = p * jnp.where(n == T_SB_Q, SB_SCALE, 1.0).astype(F32)
    o_ref[...] = p.astype(BF16)

    @pl.when(n == T_HG_F)
    def _():
        f_ref[...] = p


def _inproj(x2d, norm_g, w_in, row_tile):
    rows = x2d.shape[0]
    assert rows % row_tile == 0
    return pl.pallas_call(
        _inproj_kernel,
        grid=(rows // row_tile, N_COL_TILES),
        in_specs=[
            pl.BlockSpec((row_tile, D_MODEL), lambda r, n: (r, 0)),
            pl.BlockSpec((1, D_MODEL), lambda r, n: (0, 0)),
            pl.BlockSpec((D_MODEL, COL_TILE), lambda r, n: (0, n)),
        ],
        out_specs=[
            pl.BlockSpec((row_tile, COL_TILE), lambda r, n: (r, n)),
            pl.BlockSpec((row_tile, COL_TILE), lambda r, n: (r, 0)),
        ],
        out_shape=[
            jax.ShapeDtypeStruct((rows, IN_COLS), BF16),
            jax.ShapeDtypeStruct((rows, HG_WIDTH), F32),
        ],
        scratch_shapes=[pltpu.VMEM((row_tile, D_MODEL), BF16)],
        compiler_params=pltpu.CompilerParams(
            dimension_semantics=("parallel", "arbitrary"),
            vmem_limit_bytes=VMEM_LIMIT_BYTES),
        name="inproj",
    )(x2d, norm_g, w_in)


def _sb_kernel(q_ref, k_ref, v_ref, g_ref, mk_ref, mv_ref, uw_ref, o_ref, acc_ref, carry_ref):
    i = pl.program_id(2)
    q = q_ref[...]
    lane = lax.broadcasted_iota(jnp.int32, (KEY_TILE, LANES), 1)
    first = lane < SB_HEAD_DIM
    zero = jnp.zeros_like(q)
    q2 = jnp.concatenate([jnp.where(first, q, zero), jnp.where(first, zero, q)], axis=0)
    uw = uw_ref[...]

    acc_ref[...] = jnp.zeros_like(acc_ref)
    carry_ref[...] = jnp.zeros_like(carry_ref)

    def tile(kb, vb, valid):
        z = lax.dot_general(q2, kb, (((1,), (1,)), ((), ())), preferred_element_type=F32)
        log_beta = jnp.minimum(z, 0.0) - jnp.log(1.0 + jnp.exp(-jnp.abs(z)))
        log_1mb = log_beta - z
        if valid is not None:
            log_1mb = jnp.where(valid, log_1mb, 0.0)
        hi, lo = _split_bf16(log_1mb)
        cs = jnp.dot(jnp.concatenate([hi, lo], axis=1), uw, preferred_element_type=F32)
        carry = carry_ref[...]
        w = jnp.exp(log_beta + cs[:, :KEY_TILE] + carry)
        if valid is not None:
            w = jnp.where(valid, w, 0.0)
        acc_ref[...] += jnp.dot(w.astype(BF16), vb, preferred_element_type=F32)
        carry_ref[...] = carry + cs[:, KEY_TILE:]

    row2 = lax.broadcasted_iota(jnp.int32, (2 * KEY_TILE, KEY_TILE), 0) % KEY_TILE
    col2 = lax.broadcasted_iota(jnp.int32, (2 * KEY_TILE, KEY_TILE), 1)

    diag = pl.multiple_of(i * KEY_TILE, KEY_TILE)
    tile(k_ref[pl.ds(diag, KEY_TILE), :], v_ref[pl.ds(diag, KEY_TILE), :], col2 < row2)

    def body(it, c):
        start = pl.multiple_of((i - 1 - it) * KEY_TILE, KEY_TILE)
        tile(k_ref[pl.ds(start, KEY_TILE), :], v_ref[pl.ds(start, KEY_TILE), :], None)
        return c

    lax.fori_loop(0, i, body, 0)

    tile(mk_ref[...], mv_ref[...], col2 >= META_ROWS - N_META)

    acc = acc_ref[...]
    out = jnp.where(first, acc[:KEY_TILE], acc[KEY_TILE:])
    gate = g_ref[...].astype(F32)
    o_ref[...] = (out * (gate * _sigmoid(gate))).astype(BF16)


def _stick_breaking(proj, proj_meta, uw, batch, seq):
    n_pairs = SB_WIDTH // LANES
    per = COL_TILE // LANES
    proj3 = proj.reshape(batch, seq, IN_COLS)
    return pl.pallas_call(
        _sb_kernel,
        grid=(batch, n_pairs, seq // KEY_TILE),
        in_specs=[
            pl.BlockSpec((None, KEY_TILE, LANES), lambda b, p, i: (b, i, T_SB_Q * per + p)),
            pl.BlockSpec((None, seq, LANES), lambda b, p, i: (b, 0, T_SB_K * per + p)),
            pl.BlockSpec((None, seq, LANES), lambda b, p, i: (b, 0, T_SB_V * per + p)),
            pl.BlockSpec((None, KEY_TILE, LANES), lambda b, p, i: (b, i, T_SB_GATE * per + p)),
            pl.BlockSpec((META_ROWS, LANES), lambda b, p, i: (0, T_SB_K * per + p)),
            pl.BlockSpec((META_ROWS, LANES), lambda b, p, i: (0, T_SB_V * per + p)),
            pl.BlockSpec((2 * KEY_TILE, 2 * KEY_TILE), lambda b, p, i: (0, 0)),
        ],
        out_specs=pl.BlockSpec((None, KEY_TILE, LANES), lambda b, p, i: (b, i, p)),
        out_shape=jax.ShapeDtypeStruct((batch, seq, SB_WIDTH), BF16),
        scratch_shapes=[pltpu.VMEM((2 * KEY_TILE, LANES), F32),
                        pltpu.VMEM((2 * KEY_TILE, LANES), F32)],
        compiler_params=pltpu.CompilerParams(
            dimension_semantics=("parallel", "parallel", "arbitrary"),
            vmem_limit_bytes=VMEM_LIMIT_BYTES),
        name="stick_breaking",
    )(proj3, proj3, proj3, proj3, proj_meta, proj_meta, uw)


def _hg_chunk(q, f_logit, v, lbv, state_t, tril2, e2, pad_rows, need_out):
    f = lbv + (1.0 - lbv) * _sigmoid(f_logit)
    if pad_rows:
        row = lax.broadcasted_iota(jnp.int32, f.shape, 0)
        f = jnp.where(row >= pad_rows, f, 1.0)
    g = jnp.log(f)
    kk = 1.0 - f
    g_hi, g_lo = _split_bf16(g)
    b = jnp.dot(tril2, jnp.concatenate([g_hi, g_lo], axis=0), preferred_element_type=F32)
    b_last = b[HG_CHUNK - 1:HG_CHUNK, :]
    k_dec = (kk * jnp.exp(b_last - b)).astype(BF16)
    new_state_t = state_t * jnp.exp(b_last) + lax.dot_general(
        v, k_dec, (((0,), (0,)), ((), ())), preferred_element_type=F32)
    if not need_out:
        return None, new_state_t

    qf = q * _sigmoid(q) * HG_SCALE
    o = lax.dot_general((qf * jnp.exp(b)).astype(BF16), state_t.astype(BF16),
                        (((1,), (1,)), ((), ())), preferred_element_type=F32)

    parts = [jnp.zeros((HG_SUB, HG_CHUNK), F32)]
    for blk in range(1, HG_NSUB):
        lo_row = blk * HG_SUB
        ref_b = b[lo_row - 1:lo_row, :]
        qh = qf[lo_row:lo_row + HG_SUB] * jnp.exp(b[lo_row:lo_row + HG_SUB] - ref_b)
        kh = kk * jnp.exp(jnp.minimum(ref_b - b, 0.0))
        parts.append(lax.dot_general(qh.astype(BF16), kh.astype(BF16),
                                     (((1,), (1,)), ((), ())), preferred_element_type=F32))
    a_off = jnp.concatenate(parts, axis=0)
    b4 = b.reshape(HG_NSUB, HG_SUB, HG_DIM)
    k4 = kk.reshape(HG_NSUB, HG_SUB, HG_DIM)
    q4 = qf.reshape(HG_NSUB, HG_SUB, HG_DIM)
    cols = []
    for j in range(HG_SUB):
        e = jnp.exp(jnp.minimum(b4 - b4[:, j:j + 1, :], 0.0))
        cols.append((q4 * k4[:, j:j + 1, :] * e).reshape(HG_CHUNK, HG_DIM).astype(BF16))
    a_rep = jnp.dot(jnp.concatenate(cols, axis=1), e2, preferred_element_type=F32)
    row = lax.broadcasted_iota(jnp.int32, (HG_CHUNK, HG_CHUNK), 0)
    col = lax.broadcasted_iota(jnp.int32, (HG_CHUNK, HG_CHUNK), 1)
    same_blk = (row // HG_SUB) == (col // HG_SUB)
    a = jnp.where(same_blk & (col <= row), a_rep,
                  jnp.where((col // HG_SUB) < (row // HG_SUB), a_off, 0.0))
    o = o + jnp.dot(a.astype(BF16), v, preferred_element_type=F32)
    return o, new_state_t


def _hg_kernel(q_ref, f_ref, v_ref, gate_ref, mf_ref, mv_ref, lbl_ref, gn_ref,
               tril_ref, e2_ref, o_ref, state_ref, *, chunks_per_tile):
    t = pl.program_id(1)
    tril2 = tril_ref[...]
    e2 = e2_ref[...]
    lbl = lbl_ref[...]
    ex = jnp.exp(lbl - jnp.max(lbl, axis=0, keepdims=True))
    lb = ex[0:1, :] / jnp.sum(ex, axis=0, keepdims=True)
    gn = gn_ref[...]

    @pl.when(t == 0)
    def _():
        for h in range(HG_HEADS):
            sl = slice(h * HG_DIM, (h + 1) * HG_DIM)
            _, st = _hg_chunk(None, mf_ref[:, sl], mv_ref[:, sl], lb[:, sl],
                              jnp.zeros((HG_DIM, HG_DIM), F32), tril2, e2,
                              HG_CHUNK - N_META, False)
            state_ref[h] = st

    def body(c, carry):
        r0 = pl.multiple_of(c * HG_CHUNK, HG_CHUNK)
        for h in range(HG_HEADS):
            sl = slice(h * HG_DIM, (h + 1) * HG_DIM)
            q = q_ref[pl.ds(r0, HG_CHUNK), sl].astype(F32)
            o, st = _hg_chunk(q, f_ref[pl.ds(r0, HG_CHUNK), sl], v_ref[pl.ds(r0, HG_CHUNK), sl],
                              lb[:, sl], state_ref[h], tril2, e2, 0, True)
            state_ref[h] = st
            o = o * lax.rsqrt(jnp.mean(o * o, axis=-1, keepdims=True) + EPS) * gn[:, sl]
            gate = gate_ref[pl.ds(r0, HG_CHUNK), sl].astype(F32)
            o_ref[pl.ds(r0, HG_CHUNK), sl] = (o * (gate * _sigmoid(gate))).astype(BF16)
        return carry

    lax.fori_loop(0, chunks_per_tile, body, 0)


def _hgrn2(proj, f_proj, proj_meta, f_meta, lb_logits, hg_norm_g, tril2, e2, batch, seq, row_tile):
    assert seq % row_tile == 0 and row_tile % HG_CHUNK == 0
    proj3 = proj.reshape(batch, seq, IN_COLS)
    f3 = f_proj.reshape(batch, seq, HG_WIDTH)
    meta_blk = META_ROWS // HG_CHUNK - 1
    const = lambda b, t: (0, 0)
    return pl.pallas_call(
        functools.partial(_hg_kernel, chunks_per_tile=row_tile // HG_CHUNK),
        grid=(batch, seq // row_tile),
        in_specs=[
            pl.BlockSpec((None, row_tile, HG_WIDTH), lambda b, t: (b, t, T_HG_Q)),
            pl.BlockSpec((None, row_tile, HG_WIDTH), lambda b, t: (b, t, 0)),
            pl.BlockSpec((None, row_tile, HG_WIDTH), lambda b, t: (b, t, T_HG_I)),
            pl.BlockSpec((None, row_tile, HG_WIDTH), lambda b, t: (b, t, T_HG_GATE)),
            pl.BlockSpec((HG_CHUNK, HG_WIDTH), lambda b, t: (meta_blk, 0)),
            pl.BlockSpec((HG_CHUNK, HG_WIDTH), lambda b, t: (meta_blk, T_HG_I)),
            pl.BlockSpec(lb_logits.shape, const),
            pl.BlockSpec((1, HG_WIDTH), const),
            pl.BlockSpec(tril2.shape, const),
            pl.BlockSpec(e2.shape, const),
        ],
        out_specs=pl.BlockSpec((None, row_tile, HG_WIDTH), lambda b, t: (b, t, 0)),
        out_shape=jax.ShapeDtypeStruct((batch, seq, HG_WIDTH), BF16),
        scratch_shapes=[pltpu.VMEM((HG_HEADS, HG_DIM, HG_DIM), F32)],
        compiler_params=pltpu.CompilerParams(
            dimension_semantics=("parallel", "arbitrary"),
            vmem_limit_bytes=VMEM_LIMIT_BYTES),
        name="hgrn2",
    )(proj3, f3, proj3, proj3, f_meta, proj_meta, lb_logits, hg_norm_g, tril2, e2)


def _final_kernel(ysb_ref, yhg_ref, gsb_ref, ghg_ref, x_ref, wsb_ref, whg_ref, wout_ref, fg_ref,
                  o_ref, wsb_s, whg_s, wout_s):
    @pl.when(pl.program_id(0) == 0)
    def _():
        wsb_s[...] = wsb_ref[...].astype(BF16)
        whg_s[...] = whg_ref[...].astype(BF16)
        wout_s[...] = wout_ref[...].astype(BF16)

    a = jnp.dot(ysb_ref[...], wsb_s[...], preferred_element_type=F32)
    b = jnp.dot(yhg_ref[...], whg_s[...], preferred_element_type=F32)
    merged = _sigmoid(gsb_ref[...].astype(F32)) * a + _sigmoid(ghg_ref[...].astype(F32)) * b
    h = x_ref[...] + jnp.dot(merged.astype(BF16), wout_s[...], preferred_element_type=F32)
    ms = jnp.mean(h * h, axis=-1, keepdims=True)
    o_ref[...] = h * lax.rsqrt(ms + EPS) * fg_ref[...]


def _final(y_sb, y_hg, proj, x2d, w_sb_out, w_hg_out, w_out, final_norm_g, row_tile):
    rows = x2d.shape[0]
    assert rows % row_tile == 0
    gate_blk = (4 * SB_WIDTH + 4 * HG_WIDTH) // D_MODEL
    const = lambda r: (0, 0)
    return pl.pallas_call(
        _final_kernel,
        grid=(rows // row_tile,),
        in_specs=[
            pl.BlockSpec((row_tile, SB_WIDTH), lambda r: (r, 0)),
            pl.BlockSpec((row_tile, HG_WIDTH), lambda r: (r, 0)),
            pl.BlockSpec((row_tile, D_MODEL), lambda r: (r, gate_blk)),
            pl.BlockSpec((row_tile, D_MODEL), lambda r: (r, gate_blk + 1)),
            pl.BlockSpec((row_tile, D_MODEL), lambda r: (r, 0)),
            pl.BlockSpec((SB_WIDTH, D_MODEL), const),
            pl.BlockSpec((HG_WIDTH, D_MODEL), const),
            pl.BlockSpec((D_MODEL, D_MODEL), const),
            pl.BlockSpec((1, D_MODEL), const),
        ],
        out_specs=pl.BlockSpec((row_tile, D_MODEL), lambda r: (r, 0)),
        out_shape=jax.ShapeDtypeStruct((rows, D_MODEL), F32),
        scratch_shapes=[pltpu.VMEM((SB_WIDTH, D_MODEL), BF16),
                        pltpu.VMEM((HG_WIDTH, D_MODEL), BF16),
                        pltpu.VMEM((D_MODEL, D_MODEL), BF16)],
        compiler_params=pltpu.CompilerParams(
            dimension_semantics=("arbitrary",),
            vmem_limit_bytes=VMEM_LIMIT_BYTES),
        name="merge_out",
    )(y_sb, y_hg, proj, proj, x2d, w_sb_out, w_hg_out, w_out, final_norm_g)


def _suffix_sum_weights():
    j = np.arange(2 * KEY_TILE)[:, None] % KEY_TILE
    s = np.arange(2 * KEY_TILE)[None, :]
    return jnp.asarray(np.where(s < KEY_TILE, j > s, True), BF16)


def _chunk_cumsum_weights():
    t = np.arange(HG_CHUNK)[:, None]
    s = np.arange(2 * HG_CHUNK)[None, :] % HG_CHUNK
    return jnp.asarray(s <= t, BF16)


def _diag_reduce_weights():
    j = np.arange(HG_SUB * HG_DIM)[:, None] // HG_DIM
    s = np.arange(HG_CHUNK)[None, :] % HG_SUB
    return jnp.asarray(j == s, BF16)


def kernel(x, meta, norm_g, w_in, w_sb_out, w_hg_out, w_out, hg_norm_g, hg_lb_logits, final_norm_g):
    batch, seq, d = x.shape
    assert d == D_MODEL and meta.shape == (N_META, D_MODEL)
    assert norm_g.shape[0] == 1 and w_in.shape == (1, D_MODEL, IN_COLS)
    assert seq % 1024 == 0

    x2d = x.reshape(batch * seq, D_MODEL)
    meta_blk = jnp.concatenate(
        [jnp.zeros((META_ROWS - N_META, D_MODEL), x.dtype), meta.astype(x.dtype)], axis=0)
    w_in2 = w_in.reshape(D_MODEL, IN_COLS)

    proj, f_proj = _inproj(x2d, norm_g, w_in2, 1024)
    proj_meta, f_meta = _inproj(meta_blk, norm_g, w_in2, META_ROWS)

    y_sb = _stick_breaking(proj, proj_meta, _suffix_sum_weights(), batch, seq)
    y_hg = _hgrn2(proj, f_proj, proj_meta, f_meta, hg_lb_logits, hg_norm_g,
                  _chunk_cumsum_weights(), _diag_reduce_weights(), batch, seq, 1024)

    out = _final(y_sb.reshape(batch * seq, SB_WIDTH), y_hg.reshape(batch * seq, HG_WIDTH), proj,
                 x2d, w_sb_out.reshape(SB_WIDTH, D_MODEL), w_hg_out.reshape(HG_WIDTH, D_MODEL),
                 w_out.reshape(D_MODEL, D_MODEL), final_norm_g.reshape(1, D_MODEL), 512)
    return out.reshape(batch, seq, D_MODEL)
```

```python
import functools

import jax
import jax.numpy as jnp
import numpy as np
from jax import lax
from jax.experimental import pallas as pl
from jax.experimental.pallas import tpu as pltpu

F32 = jnp.float32
BF16 = jnp.bfloat16

D_MODEL = 1024
N_META = 16
SB_HEADS = 8
SB_HEAD_DIM = 64
SB_WIDTH = SB_HEADS * SB_HEAD_DIM
SB_SCALE = SB_HEAD_DIM ** -0.5
HG_HEADS = 4
HG_DIM = 128
HG_WIDTH = HG_HEADS * HG_DIM
HG_SCALE = HG_DIM ** -0.5
IN_COLS = 4 * SB_WIDTH + 4 * HG_WIDTH + 2 * D_MODEL
EPS = 1e-6

LANES = 128
VMEM_LIMIT_BYTES = 56 * 1024 * 1024

COL_TILE = 512
T_SB_Q, T_SB_K, T_SB_V, T_SB_GATE, T_HG_Q, T_HG_F, T_HG_I, T_HG_GATE = range(8)
N_COL_TILES = IN_COLS // COL_TILE

KEY_TILE = 128
META_ROWS = 128
HG_CHUNK = 64
HG_SUB = 16
HG_NSUB = HG_CHUNK // HG_SUB


def _sigmoid(x):
    return 1.0 / (1.0 + jnp.exp(-x))


def _split_bf16(x):
    hi = x.astype(BF16)
    lo = (x - hi.astype(F32)).astype(BF16)
    return hi, lo


def _inproj_kernel(x_ref, g_ref, w_ref, o_ref, f_ref, u_ref):
    n = pl.program_id(1)

    @pl.when(n == 0)
    def _():
        x = x_ref[...]
        ms = jnp.mean(x * x, axis=-1, keepdims=True)
        u_ref[...] = (x * lax.rsqrt(ms + EPS) * g_ref[...]).astype(BF16)

    p = jnp.dot(u_ref[...], w_ref[...].astype(BF16), preferred_element_type=F32)
    p = p * jnp.where(n == T_SB_Q, SB_SCALE, 1.0).astype(F32)
    o_ref[...] = p.astype(BF16)

    @pl.when(n == T_HG_F)
    def _():
        f_ref[...] = p


def _inproj(x2d, norm_g, w_in, row_tile):
    rows = x2d.shape[0]
    assert rows % row_tile == 0
    return pl.pallas_call(
        _inproj_kernel,
        grid=(rows // row_tile, N_COL_TILES),
        in_specs=[
            pl.BlockSpec((row_tile, D_MODEL), lambda r, n: (r, 0)),
            pl.BlockSpec((1, D_MODEL), lambda r, n: (0, 0)),
            pl.BlockSpec((D_MODEL, COL_TILE), lambda r, n: (0, n)),
        ],
        out_specs=[
            pl.BlockSpec((row_tile, COL_TILE), lambda r, n: (r, n)),
            pl.BlockSpec((row_tile, COL_TILE), lambda r, n: (r, 0)),
        ],
        out_shape=[
            jax.ShapeDtypeStruct((rows, IN_COLS), BF16),
            jax.ShapeDtypeStruct((rows, HG_WIDTH), F32),
        ],
        scratch_shapes=[pltpu.VMEM((row_tile, D_MODEL), BF16)],
        compiler_params=pltpu.CompilerParams(
            dimension_semantics=("parallel", "arbitrary"),
            vmem_limit_bytes=VMEM_LIMIT_BYTES),
        name="inproj",
    )(x2d, norm_g, w_in)


SB_PAIRS = SB_WIDTH // LANES


def _sb_kernel(q_ref, k_ref, v_ref, g_ref, mk_ref, mv_ref, uw_ref, o_ref, acc_ref, carry_ref):
    i = pl.program_id(1)
    lane = lax.broadcasted_iota(jnp.int32, (KEY_TILE, LANES), 1)
    first = lane < SB_HEAD_DIM
    uw = uw_ref[...]
    pair_lanes = [slice(p * LANES, (p + 1) * LANES) for p in range(SB_PAIRS)]
    q2 = []
    for sl in pair_lanes:
        q = q_ref[:, sl]
        zero = jnp.zeros_like(q)
        q2.append(jnp.concatenate([jnp.where(first, q, zero), jnp.where(first, zero, q)], axis=0))

    acc_ref[...] = jnp.zeros_like(acc_ref)
    carry_ref[...] = jnp.zeros_like(carry_ref)

    def tile(kt_ref, vt_ref, rows, valid):
        zs = [lax.dot_general(q2[p], kt_ref[rows, sl], (((1,), (1,)), ((), ())),
                              preferred_element_type=F32) for p, sl in enumerate(pair_lanes)]
        log_betas, sums = [], []
        for z in zs:
            log_beta = jnp.minimum(z, 0.0) - jnp.log(1.0 + jnp.exp(-jnp.abs(z)))
            log_1mb = log_beta - z
            if valid is not None:
                log_1mb = jnp.where(valid, log_1mb, 0.0)
            hi, lo = _split_bf16(log_1mb)
            sums.append(jnp.dot(jnp.concatenate([hi, lo], axis=1), uw, preferred_element_type=F32))
            log_betas.append(log_beta)
        ws = []
        for p in range(SB_PAIRS):
            carry = carry_ref[p]
            w = jnp.exp(log_betas[p] + sums[p][:, :KEY_TILE] + carry)
            if valid is not None:
                w = jnp.where(valid, w, 0.0)
            ws.append(w.astype(BF16))
            carry_ref[p] = carry + sums[p][:, KEY_TILE:]
        for p, sl in enumerate(pair_lanes):
            acc_ref[p] += jnp.dot(ws[p], vt_ref[rows, sl], preferred_element_type=F32)

    row2 = lax.broadcasted_iota(jnp.int32, (2 * KEY_TILE, KEY_TILE), 0) % KEY_TILE
    col2 = lax.broadcasted_iota(jnp.int32, (2 * KEY_TILE, KEY_TILE), 1)

    tile(k_ref, v_ref, pl.ds(pl.multiple_of(i * KEY_TILE, KEY_TILE), KEY_TILE), col2 < row2)

    def body(it, c):
        start = pl.multiple_of((i - 1 - it) * KEY_TILE, KEY_TILE)
        tile(k_ref, v_ref, pl.ds(start, KEY_TILE), None)
        return c

    lax.fori_loop(0, i, body, 0)

    tile(mk_ref, mv_ref, slice(None), col2 >= META_ROWS - N_META)

    for p, sl in enumerate(pair_lanes):
        acc = acc_ref[p]
        out = jnp.where(first, acc[:KEY_TILE], acc[KEY_TILE:])
        gate = g_ref[:, sl].astype(F32)
        o_ref[:, sl] = (out * (gate * _sigmoid(gate))).astype(BF16)


def _stick_breaking(proj, proj_meta, uw, batch, seq):
    assert SB_WIDTH == COL_TILE
    proj3 = proj.reshape(batch, seq, IN_COLS)
    return pl.pallas_call(
        _sb_kernel,
        grid=(batch, seq // KEY_TILE),
        in_specs=[
            pl.BlockSpec((None, KEY_TILE, SB_WIDTH), lambda b, i: (b, i, T_SB_Q)),
            pl.BlockSpec((None, seq, SB_WIDTH), lambda b, i: (b, 0, T_SB_K)),
            pl.BlockSpec((None, seq, SB_WIDTH), lambda b, i: (b, 0, T_SB_V)),
            pl.BlockSpec((None, KEY_TILE, SB_WIDTH), lambda b, i: (b, i, T_SB_GATE)),
            pl.BlockSpec((META_ROWS, SB_WIDTH), lambda b, i: (0, T_SB_K)),
            pl.BlockSpec((META_ROWS, SB_WIDTH), lambda b, i: (0, T_SB_V)),
            pl.BlockSpec((2 * KEY_TILE, 2 * KEY_TILE), lambda b, i: (0, 0)),
        ],
        out_specs=pl.BlockSpec((None, KEY_TILE, SB_WIDTH), lambda b, i: (b, i, 0)),
        out_shape=jax.ShapeDtypeStruct((batch, seq, SB_WIDTH), BF16),
        scratch_shapes=[pltpu.VMEM((SB_PAIRS, 2 * KEY_TILE, LANES), F32),
                        pltpu.VMEM((SB_PAIRS, 2 * KEY_TILE, LANES), F32)],
        compiler_params=pltpu.CompilerParams(
            dimension_semantics=("parallel", "arbitrary"),
            vmem_limit_bytes=VMEM_LIMIT_BYTES),
        name="stick_breaking",
    )(proj3, proj3, proj3, proj3, proj_meta, proj_meta, uw)


def _hg_chunk(q, f_logit, v, lbv, state_t, tril2, e2, pad_rows, need_out):
    f = lbv + (1.0 - lbv) * _sigmoid(f_logit)
    if pad_rows:
        row = lax.broadcasted_iota(jnp.int32, f.shape, 0)
        f = jnp.where(row >= pad_rows, f, 1.0)
    g = jnp.log(f)
    kk = 1.0 - f
    g_hi, g_lo = _split_bf16(g)
    b = jnp.dot(tril2, jnp.concatenate([g_hi, g_lo], axis=0), preferred_element_type=F32)
    b_last = b[HG_CHUNK - 1:HG_CHUNK, :]
    k_dec = (kk * jnp.exp(b_last - b)).astype(BF16)
    new_state_t = state_t * jnp.exp(b_last) + lax.dot_general(
        v, k_dec, (((0,), (0,)), ((), ())), preferred_element_type=F32)
    if not need_out:
        return None, new_state_t

    qf = q * _sigmoid(q) * HG_SCALE
    o = lax.dot_general((qf * jnp.exp(b)).astype(BF16), state_t.astype(BF16),
                        (((1,), (1,)), ((), ())), preferred_element_type=F32)

    parts = [jnp.zeros((HG_SUB, HG_CHUNK), F32)]
    for blk in range(1, HG_NSUB):
        lo_row = blk * HG_SUB
        ref_b = b[lo_row - 1:lo_row, :]
        qh = qf[lo_row:lo_row + HG_SUB] * jnp.exp(b[lo_row:lo_row + HG_SUB] - ref_b)
        kh = kk * jnp.exp(jnp.minimum(ref_b - b, 0.0))
        parts.append(lax.dot_general(qh.astype(BF16), kh.astype(BF16),
                                     (((1,), (1,)), ((), ())), preferred_element_type=F32))
    a_off = jnp.concatenate(parts, axis=0)
    b4 = b.reshape(HG_NSUB, HG_SUB, HG_DIM)
    k4 = kk.reshape(HG_NSUB, HG_SUB, HG_DIM)
    q4 = qf.reshape(HG_NSUB, HG_SUB, HG_DIM)
    cols = []
    for j in range(HG_SUB):
        e = jnp.exp(jnp.minimum(b4 - b4[:, j:j + 1, :], 0.0))
        cols.append((q4 * k4[:, j:j + 1, :] * e).reshape(HG_CHUNK, HG_DIM).astype(BF16))
    a_rep = jnp.dot(jnp.concatenate(cols, axis=1), e2, preferred_element_type=F32)
    row = lax.broadcasted_iota(jnp.int32, (HG_CHUNK, HG_CHUNK), 0)
    col = lax.broadcasted_iota(jnp.int32, (HG_CHUNK, HG_CHUNK), 1)
    same_blk = (row // HG_SUB) == (col // HG_SUB)
    a = jnp.where(same_blk & (col <= row), a_rep,
                  jnp.where((col // HG_SUB) < (row // HG_SUB), a_off, 0.0))
    o = o + jnp.dot(a.astype(BF16), v, preferred_element_type=F32)
    return o, new_state_t


def _hg_kernel(q_ref, f_ref, v_ref, gate_ref, mf_ref, mv_ref, lbl_ref, gn_ref,
               tril_ref, e2_ref, o_ref, state_ref, *, chunks_per_tile):
    t = pl.program_id(1)
    tril2 = tril_ref[...]
    e2 = e2_ref[...]
    lbl = lbl_ref[...]
    ex = jnp.exp(lbl - jnp.max(lbl, axis=0, keepdims=True))
    lb = ex[0:1, :] / jnp.sum(ex, axis=0, keepdims=True)
    gn = gn_ref[...]

    @pl.when(t == 0)
    def _():
        for h in range(HG_HEADS):
            sl = slice(h * HG_DIM, (h + 1) * HG_DIM)
            _, st = _hg_chunk(None, mf_ref[:, sl], mv_ref[:, sl], lb[:, sl],
                              jnp.zeros((HG_DIM, HG_DIM), F32), tril2, e2,
                              HG_CHUNK - N_META, False)
            state_ref[h] = st

    def body(c, carry):
        r0 = pl.multiple_of(c * HG_CHUNK, HG_CHUNK)
        for h in range(HG_HEADS):
            sl = slice(h * HG_DIM, (h + 1) * HG_DIM)
            q = q_ref[pl.ds(r0, HG_CHUNK), sl].astype(F32)
            o, st = _hg_chunk(q, f_ref[pl.ds(r0, HG_CHUNK), sl], v_ref[pl.ds(r0, HG_CHUNK), sl],
                              lb[:, sl], state_ref[h], tril2, e2, 0, True)
            state_ref[h] = st
            o = o * lax.rsqrt(jnp.mean(o * o, axis=-1, keepdims=True) + EPS) * gn[:, sl]
            gate = gate_ref[pl.ds(r0, HG_CHUNK), sl].astype(F32)
            o_ref[pl.ds(r0, HG_CHUNK), sl] = (o * (gate * _sigmoid(gate))).astype(BF16)
        return carry

    lax.fori_loop(0, chunks_per_tile, body, 0)


def _hgrn2(proj, f_proj, proj_meta, f_meta, lb_logits, hg_norm_g, tril2, e2, batch, seq, row_tile):
    assert seq % row_tile == 0 and row_tile % HG_CHUNK == 0
    proj3 = proj.reshape(batch, seq, IN_COLS)
    f3 = f_proj.reshape(batch, seq, HG_WIDTH)
    meta_blk = META_ROWS // HG_CHUNK - 1
    const = lambda b, t: (0, 0)
    return pl.pallas_call(
        functools.partial(_hg_kernel, chunks_per_tile=row_tile // HG_CHUNK),
        grid=(batch, seq // row_tile),
        in_specs=[
            pl.BlockSpec((None, row_tile, HG_WIDTH), lambda b, t: (b, t, T_HG_Q)),
            pl.BlockSpec((None, row_tile, HG_WIDTH), lambda b, t: (b, t, 0)),
            pl.BlockSpec((None, row_tile, HG_WIDTH), lambda b, t: (b, t, T_HG_I)),
            pl.BlockSpec((None, row_tile, HG_WIDTH), lambda b, t: (b, t, T_HG_GATE)),
            pl.BlockSpec((HG_CHUNK, HG_WIDTH), lambda b, t: (meta_blk, 0)),
            pl.BlockSpec((HG_CHUNK, HG_WIDTH), lambda b, t: (meta_blk, T_HG_I)),
            pl.BlockSpec(lb_logits.shape, const),
            pl.BlockSpec((1, HG_WIDTH), const),
            pl.BlockSpec(tril2.shape, const),
            pl.BlockSpec(e2.shape, const),
        ],
        out_specs=pl.BlockSpec((None, row_tile, HG_WIDTH), lambda b, t: (b, t, 0)),
        out_shape=jax.ShapeDtypeStruct((batch, seq, HG_WIDTH), BF16),
        scratch_shapes=[pltpu.VMEM((HG_HEADS, HG_DIM, HG_DIM), F32)],
        compiler_params=pltpu.CompilerParams(
            dimension_semantics=("parallel", "arbitrary"),
            vmem_limit_bytes=VMEM_LIMIT_BYTES),
        name="hgrn2",
    )(proj3, f3, proj3, proj3, f_meta, proj_meta, lb_logits, hg_norm_g, tril2, e2)


def _final_kernel(ysb_ref, yhg_ref, gsb_ref, ghg_ref, x_ref, wsb_ref, whg_ref, wout_ref, fg_ref,
                  o_ref, wsb_s, whg_s, wout_s):
    @pl.when(pl.program_id(0) == 0)
    def _():
        wsb_s[...] = wsb_ref[...].astype(BF16)
        whg_s[...] = whg_ref[...].astype(BF16)
        wout_s[...] = wout_ref[...].astype(BF16)

    a = jnp.dot(ysb_ref[...], wsb_s[...], preferred_element_type=F32)
    b = jnp.dot(yhg_ref[...], whg_s[...], preferred_element_type=F32)
    merged = _sigmoid(gsb_ref[...].astype(F32)) * a + _sigmoid(ghg_ref[...].astype(F32)) * b
    h = x_ref[...] + jnp.dot(merged.astype(BF16), wout_s[...], preferred_element_type=F32)
    ms = jnp.mean(h * h, axis=-1, keepdims=True)
    o_ref[...] = h * lax.rsqrt(ms + EPS) * fg_ref[...]


def _final(y_sb, y_hg, proj, x2d, w_sb_out, w_hg_out, w_out, final_norm_g, row_tile):
    rows = x2d.shape[0]
    assert rows % row_tile == 0
    gate_blk = (4 * SB_WIDTH + 4 * HG_WIDTH) // D_MODEL
    const = lambda r: (0, 0)
    return pl.pallas_call(
        _final_kernel,
        grid=(rows // row_tile,),
        in_specs=[
            pl.BlockSpec((row_tile, SB_WIDTH), lambda r: (r, 0)),
            pl.BlockSpec((row_tile, HG_WIDTH), lambda r: (r, 0)),
            pl.BlockSpec((row_tile, D_MODEL), lambda r: (r, gate_blk)),
            pl.BlockSpec((row_tile, D_MODEL), lambda r: (r, gate_blk + 1)),
            pl.BlockSpec((row_tile, D_MODEL), lambda r: (r, 0)),
            pl.BlockSpec((SB_WIDTH, D_MODEL), const),
            pl.BlockSpec((HG_WIDTH, D_MODEL), const),
            pl.BlockSpec((D_MODEL, D_MODEL), const),
            pl.BlockSpec((1, D_MODEL), const),
        ],
        out_specs=pl.BlockSpec((row_tile, D_MODEL), lambda r: (r, 0)),
        out_shape=jax.ShapeDtypeStruct((rows, D_MODEL), F32),
        scratch_shapes=[pltpu.VMEM((SB_WIDTH, D_MODEL), BF16),
                        pltpu.VMEM((HG_WIDTH, D_MODEL), BF16),
                        pltpu.VMEM((D_MODEL, D_MODEL), BF16)],
        compiler_params=pltpu.CompilerParams(
            dimension_semantics=("arbitrary",),
            vmem_limit_bytes=VMEM_LIMIT_BYTES),
        name="merge_out",
    )(y_sb, y_hg, proj, proj, x2d, w_sb_out, w_hg_out, w_out, final_norm_g)


def _suffix_sum_weights():
    j = np.arange(2 * KEY_TILE)[:, None] % KEY_TILE
    s = np.arange(2 * KEY_TILE)[None, :]
    return jnp.asarray(np.where(s < KEY_TILE, j > s, True), BF16)


def _chunk_cumsum_weights():
    t = np.arange(HG_CHUNK)[:, None]
    s = np.arange(2 * HG_CHUNK)[None, :] % HG_CHUNK
    return jnp.asarray(s <= t, BF16)


def _diag_reduce_weights():
    j = np.arange(HG_SUB * HG_DIM)[:, None] // HG_DIM
    s = np.arange(HG_CHUNK)[None, :] % HG_SUB
    return jnp.asarray(j == s, BF16)


def kernel(x, meta, norm_g, w_in, w_sb_out, w_hg_out, w_out, hg_norm_g, hg_lb_logits, final_norm_g):
    batch, seq, d = x.shape
    assert d == D_MODEL and meta.shape == (N_META, D_MODEL)
    assert norm_g.shape[0] == 1 and w_in.shape == (1, D_MODEL, IN_COLS)
    assert seq % 1024 == 0

    x2d = x.reshape(batch * seq, D_MODEL)
    meta_blk = jnp.concatenate(
        [jnp.zeros((META_ROWS - N_META, D_MODEL), x.dtype), meta.astype(x.dtype)], axis=0)
    w_in2 = w_in.reshape(D_MODEL, IN_COLS)

    proj, f_proj = _inproj(x2d, norm_g, w_in2, 1024)
    proj_meta, f_meta = _inproj(meta_blk, norm_g, w_in2, META_ROWS)

    y_sb = _stick_breaking(proj, proj_meta, _suffix_sum_weights(), batch, seq)
    y_hg = _hgrn2(proj, f_proj, proj_meta, f_meta, hg_lb_logits, hg_norm_g,
                  _chunk_cumsum_weights(), _diag_reduce_weights(), batch, seq, 1024)

    out = _final(y_sb.reshape(batch * seq, SB_WIDTH), y_hg.reshape(batch * seq, HG_WIDTH), proj,
                 x2d, w_sb_out.reshape(SB_WIDTH, D_MODEL), w_hg_out.reshape(HG_WIDTH, D_MODEL),
                 w_out.reshape(D_MODEL, D_MODEL), final_norm_g.reshape(1, D_MODEL), 512)
    return out.reshape(batch, seq, D_MODEL)
```

```python
import functools

import jax
import jax.numpy as jnp
import numpy as np
from jax import lax
from jax.experimental import pallas as pl
from jax.experimental.pallas import tpu as pltpu

F32 = jnp.float32
BF16 = jnp.bfloat16

D_MODEL = 1024
N_META = 16
SB_HEADS = 8
SB_HEAD_DIM = 64
SB_WIDTH = SB_HEADS * SB_HEAD_DIM
SB_SCALE = SB_HEAD_DIM ** -0.5
HG_HEADS = 4
HG_DIM = 128
HG_WIDTH = HG_HEADS * HG_DIM
HG_SCALE = HG_DIM ** -0.5
IN_COLS = 4 * SB_WIDTH + 4 * HG_WIDTH + 2 * D_MODEL
EPS = 1e-6

LANES = 128
VMEM_LIMIT_BYTES = 56 * 1024 * 1024

COL_TILE = 512
T_SB_Q, T_SB_K, T_SB_V, T_SB_GATE, T_HG_Q, T_HG_F, T_HG_I, T_HG_GATE = range(8)
N_COL_TILES = IN_COLS // COL_TILE

KEY_TILE = 128
META_ROWS = 128
HG_CHUNK = 64
HG_SUB = 16
HG_NSUB = HG_CHUNK // HG_SUB


def _sigmoid(x):
    return 1.0 / (1.0 + jnp.exp(-x))


def _split_bf16(x):
    hi = x.astype(BF16)
    lo = (x - hi.astype(F32)).astype(BF16)
    return hi, lo


def _inproj_kernel(x_ref, g_ref, w_ref, o_ref, f_ref, u_ref):
    n = pl.program_id(1)

    @pl.when(n == 0)
    def _():
        x = x_ref[...]
        ms = jnp.mean(x * x, axis=-1, keepdims=True)
        u_ref[...] = (x * lax.rsqrt(ms + EPS) * g_ref[...]).astype(BF16)

    p = jnp.dot(u_ref[...], w_ref[...].astype(BF16), preferred_element_type=F32)
    p = p * jnp.where(n == T_SB_Q, SB_SCALE, 1.0).astype(F32)
    o_ref[...] = p.astype(BF16)

    @pl.when(n == T_HG_F)
    def _():
        f_ref[...] = p


def _inproj(x2d, norm_g, w_in, row_tile):
    rows = x2d.shape[0]
    assert rows % row_tile == 0
    return pl.pallas_call(
        _inproj_kernel,
        grid=(rows // row_tile, N_COL_TILES),
        in_specs=[
            pl.BlockSpec((row_tile, D_MODEL), lambda r, n: (r, 0)),
            pl.BlockSpec((1, D_MODEL), lambda r, n: (0, 0)),
            pl.BlockSpec((D_MODEL, COL_TILE), lambda r, n: (0, n)),
        ],
        out_specs=[
            pl.BlockSpec((row_tile, COL_TILE), lambda r, n: (r, n)),
            pl.BlockSpec((row_tile, COL_TILE), lambda r, n: (r, 0)),
        ],
        out_shape=[
            jax.ShapeDtypeStruct((rows, IN_COLS), BF16),
            jax.ShapeDtypeStruct((rows, HG_WIDTH), F32),
        ],
        scratch_shapes=[pltpu.VMEM((row_tile, D_MODEL), BF16)],
        compiler_params=pltpu.CompilerParams(
            dimension_semantics=("parallel", "arbitrary"),
            vmem_limit_bytes=VMEM_LIMIT_BYTES),
        name="inproj",
    )(x2d, norm_g, w_in)


SB_PAIRS = SB_WIDTH // LANES
SB_UNDERFLOW = -104.0


def _sb_kernel(q_ref, k_ref, v_ref, g_ref, mk_ref, mv_ref, uw_ref, o_ref, acc_ref, carry_ref):
    i = pl.program_id(1)
    lane = lax.broadcasted_iota(jnp.int32, (KEY_TILE, LANES), 1)
    first = lane < SB_HEAD_DIM
    uw = uw_ref[...]
    pair_lanes = [slice(p * LANES, (p + 1) * LANES) for p in range(SB_PAIRS)]
    q2 = []
    for sl in pair_lanes:
        q = q_ref[:, sl]
        zero = jnp.zeros_like(q)
        q2.append(jnp.concatenate([jnp.where(first, q, zero), jnp.where(first, zero, q)], axis=0))

    acc_ref[...] = jnp.zeros_like(acc_ref)
    carry_ref[...] = jnp.zeros_like(carry_ref)

    def tile(kt_ref, vt_ref, rows, valid):
        zs = [lax.dot_general(q2[p], kt_ref[rows, sl], (((1,), (1,)), ((), ())),
                              preferred_element_type=F32) for p, sl in enumerate(pair_lanes)]
        log_betas, sums = [], []
        for z in zs:
            log_beta = jnp.minimum(z, 0.0) - jnp.log(1.0 + jnp.exp(-jnp.abs(z)))
            log_1mb = log_beta - z
            if valid is not None:
                log_1mb = jnp.where(valid, log_1mb, 0.0)
            hi, lo = _split_bf16(log_1mb)
            sums.append(jnp.dot(jnp.concatenate([hi, lo], axis=1), uw, preferred_element_type=F32))
            log_betas.append(log_beta)
        ws = []
        for p in range(SB_PAIRS):
            carry = carry_ref[p]
            w = jnp.exp(log_betas[p] + sums[p][:, :KEY_TILE] + carry)
            if valid is not None:
                w = jnp.where(valid, w, 0.0)
            ws.append(w.astype(BF16))
            carry_ref[p] = carry + sums[p][:, KEY_TILE:]
        for p, sl in enumerate(pair_lanes):
            acc_ref[p] += jnp.dot(ws[p], vt_ref[rows, sl], preferred_element_type=F32)

    row2 = lax.broadcasted_iota(jnp.int32, (2 * KEY_TILE, KEY_TILE), 0) % KEY_TILE
    col2 = lax.broadcasted_iota(jnp.int32, (2 * KEY_TILE, KEY_TILE), 1)

    tile(k_ref, v_ref, pl.ds(pl.multiple_of(i * KEY_TILE, KEY_TILE), KEY_TILE), col2 < row2)

    def cond(c):
        it, live = c
        return jnp.logical_and(it < i, live > 0)

    def body(c):
        it, _ = c
        start = pl.multiple_of((i - 1 - it) * KEY_TILE, KEY_TILE)
        tile(k_ref, v_ref, pl.ds(start, KEY_TILE), None)
        top = jnp.max(carry_ref[...].reshape(-1, 8, LANES), axis=0)
        return it + 1, (jnp.max(top) > SB_UNDERFLOW).astype(jnp.int32)

    _, live = lax.while_loop(cond, body, (jnp.int32(0), jnp.int32(1)))

    @pl.when(live > 0)
    def _():
        tile(mk_ref, mv_ref, slice(None), col2 >= META_ROWS - N_META)

    for p, sl in enumerate(pair_lanes):
        acc = acc_ref[p]
        out = jnp.where(first, acc[:KEY_TILE], acc[KEY_TILE:])
        gate = g_ref[:, sl].astype(F32)
        o_ref[:, sl] = (out * (gate * _sigmoid(gate))).astype(BF16)


def _stick_breaking(proj, proj_meta, uw, batch, seq):
    assert SB_WIDTH == COL_TILE
    proj3 = proj.reshape(batch, seq, IN_COLS)
    return pl.pallas_call(
        _sb_kernel,
        grid=(batch, seq // KEY_TILE),
        in_specs=[
            pl.BlockSpec((None, KEY_TILE, SB_WIDTH), lambda b, i: (b, i, T_SB_Q)),
            pl.BlockSpec((None, seq, SB_WIDTH), lambda b, i: (b, 0, T_SB_K)),
            pl.BlockSpec((None, seq, SB_WIDTH), lambda b, i: (b, 0, T_SB_V)),
            pl.BlockSpec((None, KEY_TILE, SB_WIDTH), lambda b, i: (b, i, T_SB_GATE)),
            pl.BlockSpec((META_ROWS, SB_WIDTH), lambda b, i: (0, T_SB_K)),
            pl.BlockSpec((META_ROWS, SB_WIDTH), lambda b, i: (0, T_SB_V)),
            pl.BlockSpec((2 * KEY_TILE, 2 * KEY_TILE), lambda b, i: (0, 0)),
        ],
        out_specs=pl.BlockSpec((None, KEY_TILE, SB_WIDTH), lambda b, i: (b, i, 0)),
        out_shape=jax.ShapeDtypeStruct((batch, seq, SB_WIDTH), BF16),
        scratch_shapes=[pltpu.VMEM((SB_PAIRS, 2 * KEY_TILE, LANES), F32),
                        pltpu.VMEM((SB_PAIRS, 2 * KEY_TILE, LANES), F32)],
        compiler_params=pltpu.CompilerParams(
            dimension_semantics=("parallel", "arbitrary"),
            vmem_limit_bytes=VMEM_LIMIT_BYTES),
        name="stick_breaking",
    )(proj3, proj3, proj3, proj3, proj_meta, proj_meta, uw)


def _hg_chunk(q, f_logit, v, lbv, state_t, tril2, e2, pad_rows, need_out):
    f = lbv + (1.0 - lbv) * _sigmoid(f_logit)
    if pad_rows:
        row = lax.broadcasted_iota(jnp.int32, f.shape, 0)
        f = jnp.where(row >= pad_rows, f, 1.0)
    g = jnp.log(f)
    kk = 1.0 - f
    g_hi, g_lo = _split_bf16(g)
    b = jnp.dot(tril2, jnp.concatenate([g_hi, g_lo], axis=0), preferred_element_type=F32)
    b_last = b[HG_CHUNK - 1:HG_CHUNK, :]
    k_dec = (kk * jnp.exp(b_last - b)).astype(BF16)
    new_state_t = state_t * jnp.exp(b_last) + lax.dot_general(
        v, k_dec, (((0,), (0,)), ((), ())), preferred_element_type=F32)
    if not need_out:
        return None, new_state_t

    qf = q * _sigmoid(q) * HG_SCALE
    o = lax.dot_general((qf * jnp.exp(b)).astype(BF16), state_t.astype(BF16),
                        (((1,), (1,)), ((), ())), preferred_element_type=F32)

    parts = [jnp.zeros((HG_SUB, HG_CHUNK), F32)]
    for blk in range(1, HG_NSUB):
        lo_row = blk * HG_SUB
        ref_b = b[lo_row - 1:lo_row, :]
        qh = qf[lo_row:lo_row + HG_SUB] * jnp.exp(b[lo_row:lo_row + HG_SUB] - ref_b)
        kh = kk * jnp.exp(jnp.minimum(ref_b - b, 0.0))
        parts.append(lax.dot_general(qh.astype(BF16), kh.astype(BF16),
                                     (((1,), (1,)), ((), ())), preferred_element_type=F32))
    a_off = jnp.concatenate(parts, axis=0)
    b4 = b.reshape(HG_NSUB, HG_SUB, HG_DIM)
    k4 = kk.reshape(HG_NSUB, HG_SUB, HG_DIM)
    q4 = qf.reshape(HG_NSUB, HG_SUB, HG_DIM)
    cols = []
    for j in range(HG_SUB):
        e = jnp.exp(jnp.minimum(b4 - b4[:, j:j + 1, :], 0.0))
        cols.append((q4 * k4[:, j:j + 1, :] * e).reshape(HG_CHUNK, HG_DIM).astype(BF16))
    a_rep = jnp.dot(jnp.concatenate(cols, axis=1), e2, preferred_element_type=F32)
    row = lax.broadcasted_iota(jnp.int32, (HG_CHUNK, HG_CHUNK), 0)
    col = lax.broadcasted_iota(jnp.int32, (HG_CHUNK, HG_CHUNK), 1)
    same_blk = (row // HG_SUB) == (col // HG_SUB)
    a = jnp.where(same_blk & (col <= row), a_rep,
                  jnp.where((col // HG_SUB) < (row // HG_SUB), a_off, 0.0))
    o = o + jnp.dot(a.astype(BF16), v, preferred_element_type=F32)
    return o, new_state_t


def _hg_kernel(q_ref, f_ref, v_ref, gate_ref, mf_ref, mv_ref, lbl_ref, gn_ref,
               tril_ref, e2_ref, o_ref, state_ref, *, chunks_per_tile):
    t = pl.program_id(1)
    tril2 = tril_ref[...]
    e2 = e2_ref[...]
    lbl = lbl_ref[...]
    ex = jnp.exp(lbl - jnp.max(lbl, axis=0, keepdims=True))
    lb = ex[0:1, :] / jnp.sum(ex, axis=0, keepdims=True)
    gn = gn_ref[...]

    @pl.when(t == 0)
    def _():
        for h in range(HG_HEADS):
            sl = slice(h * HG_DIM, (h + 1) * HG_DIM)
            _, st = _hg_chunk(None, mf_ref[:, sl], mv_ref[:, sl], lb[:, sl],
                              jnp.zeros((HG_DIM, HG_DIM), F32), tril2, e2,
                              HG_CHUNK - N_META, False)
            state_ref[h] = st

    def body(c, carry):
        r0 = pl.multiple_of(c * HG_CHUNK, HG_CHUNK)
        for h in range(HG_HEADS):
            sl = slice(h * HG_DIM, (h + 1) * HG_DIM)
            q = q_ref[pl.ds(r0, HG_CHUNK), sl].astype(F32)
            o, st = _hg_chunk(q, f_ref[pl.ds(r0, HG_CHUNK), sl], v_ref[pl.ds(r0, HG_CHUNK), sl],
                              lb[:, sl], state_ref[h], tril2, e2, 0, True)
            state_ref[h] = st
            o = o * lax.rsqrt(jnp.mean(o * o, axis=-1, keepdims=True) + EPS) * gn[:, sl]
            gate = gate_ref[pl.ds(r0, HG_CHUNK), sl].astype(F32)
            o_ref[pl.ds(r0, HG_CHUNK), sl] = (o * (gate * _sigmoid(gate))).astype(BF16)
        return carry

    lax.fori_loop(0, chunks_per_tile, body, 0)


def _hgrn2(proj, f_proj, proj_meta, f_meta, lb_logits, hg_norm_g, tril2, e2, batch, seq, row_tile):
    assert seq % row_tile == 0 and row_tile % HG_CHUNK == 0
    proj3 = proj.reshape(batch, seq, IN_COLS)
    f3 = f_proj.reshape(batch, seq, HG_WIDTH)
    meta_blk = META_ROWS // HG_CHUNK - 1
    const = lambda b, t: (0, 0)
    return pl.pallas_call(
        functools.partial(_hg_kernel, chunks_per_tile=row_tile // HG_CHUNK),
        grid=(batch, seq // row_tile),
        in_specs=[
            pl.BlockSpec((None, row_tile, HG_WIDTH), lambda b, t: (b, t, T_HG_Q)),
            pl.BlockSpec((None, row_tile, HG_WIDTH), lambda b, t: (b, t, 0)),
            pl.BlockSpec((None, row_tile, HG_WIDTH), lambda b, t: (b, t, T_HG_I)),
            pl.BlockSpec((None, row_tile, HG_WIDTH), lambda b, t: (b, t, T_HG_GATE)),
            pl.BlockSpec((HG_CHUNK, HG_WIDTH), lambda b, t: (meta_blk, 0)),
            pl.BlockSpec((HG_CHUNK, HG_WIDTH), lambda b, t: (meta_blk, T_HG_I)),
            pl.BlockSpec(lb_logits.shape, const),
            pl.BlockSpec((1, HG_WIDTH), const),
            pl.BlockSpec(tril2.shape, const),
            pl.BlockSpec(e2.shape, const),
        ],
        out_specs=pl.BlockSpec((None, row_tile, HG_WIDTH), lambda b, t: (b, t, 0)),
        out_shape=jax.ShapeDtypeStruct((batch, seq, HG_WIDTH), BF16),
        scratch_shapes=[pltpu.VMEM((HG_HEADS, HG_DIM, HG_DIM), F32)],
        compiler_params=pltpu.CompilerParams(
            dimension_semantics=("parallel", "arbitrary"),
            vmem_limit_bytes=VMEM_LIMIT_BYTES),
        name="hgrn2",
    )(proj3, f3, proj3, proj3, f_meta, proj_meta, lb_logits, hg_norm_g, tril2, e2)


def _final_kernel(ysb_ref, yhg_ref, gsb_ref, ghg_ref, x_ref, wsb_ref, whg_ref, wout_ref, fg_ref,
                  o_ref, wsb_s, whg_s, wout_s):
    @pl.when(pl.program_id(0) == 0)
    def _():
        wsb_s[...] = wsb_ref[...].astype(BF16)
        whg_s[...] = whg_ref[...].astype(BF16)
        wout_s[...] = wout_ref[...].astype(BF16)

    a = jnp.dot(ysb_ref[...], wsb_s[...], preferred_element_type=F32)
    b = jnp.dot(yhg_ref[...], whg_s[...], preferred_element_type=F32)
    merged = _sigmoid(gsb_ref[...].astype(F32)) * a + _sigmoid(ghg_ref[...].astype(F32)) * b
    h = x_ref[...] + jnp.dot(merged.astype(BF16), wout_s[...], preferred_element_type=F32)
    ms = jnp.mean(h * h, axis=-1, keepdims=True)
    o_ref[...] = h * lax.rsqrt(ms + EPS) * fg_ref[...]


def _final(y_sb, y_hg, proj, x2d, w_sb_out, w_hg_out, w_out, final_norm_g, row_tile):
    rows = x2d.shape[0]
    assert rows % row_tile == 0
    gate_blk = (4 * SB_WIDTH + 4 * HG_WIDTH) // D_MODEL
    const = lambda r: (0, 0)
    return pl.pallas_call(
        _final_kernel,
        grid=(rows // row_tile,),
        in_specs=[
            pl.BlockSpec((row_tile, SB_WIDTH), lambda r: (r, 0)),
            pl.BlockSpec((row_tile, HG_WIDTH), lambda r: (r, 0)),
            pl.BlockSpec((row_tile, D_MODEL), lambda r: (r, gate_blk)),
            pl.BlockSpec((row_tile, D_MODEL), lambda r: (r, gate_blk + 1)),
            pl.BlockSpec((row_tile, D_MODEL), lambda r: (r, 0)),
            pl.BlockSpec((SB_WIDTH, D_MODEL), const),
            pl.BlockSpec((HG_WIDTH, D_MODEL), const),
            pl.BlockSpec((D_MODEL, D_MODEL), const),
            pl.BlockSpec((1, D_MODEL), const),
        ],
        out_specs=pl.BlockSpec((row_tile, D_MODEL), lambda r: (r, 0)),
        out_shape=jax.ShapeDtypeStruct((rows, D_MODEL), F32),
        scratch_shapes=[pltpu.VMEM((SB_WIDTH, D_MODEL), BF16),
                        pltpu.VMEM((HG_WIDTH, D_MODEL), BF16),
                        pltpu.VMEM((D_MODEL, D_MODEL), BF16)],
        compiler_params=pltpu.CompilerParams(
            dimension_semantics=("arbitrary",),
            vmem_limit_bytes=VMEM_LIMIT_BYTES),
        name="merge_out",
    )(y_sb, y_hg, proj, proj, x2d, w_sb_out, w_hg_out, w_out, final_norm_g)


def _suffix_sum_weights():
    j = np.arange(2 * KEY_TILE)[:, None] % KEY_TILE
    s = np.arange(2 * KEY_TILE)[None, :]
    return jnp.asarray(np.where(s < KEY_TILE, j > s, True), BF16)


def _chunk_cumsum_weights():
    t = np.arange(HG_CHUNK)[:, None]
    s = np.arange(2 * HG_CHUNK)[None, :] % HG_CHUNK
    return jnp.asarray(s <= t, BF16)


def _diag_reduce_weights():
    j = np.arange(HG_SUB * HG_DIM)[:, None] // HG_DIM
    s = np.arange(HG_CHUNK)[None, :] % HG_SUB
    return jnp.asarray(j == s, BF16)


def kernel(x, meta, norm_g, w_in, w_sb_out, w_hg_out, w_out, hg_norm_g, hg_lb_logits, final_norm_g):
    batch, seq, d = x.shape
    assert d == D_MODEL and meta.shape == (N_META, D_MODEL)
    assert norm_g.shape[0] == 1 and w_in.shape == (1, D_MODEL, IN_COLS)
    assert seq % 1024 == 0

    x2d = x.reshape(batch * seq, D_MODEL)
    meta_blk = jnp.concatenate(
        [jnp.zeros((META_ROWS - N_META, D_MODEL), x.dtype), meta.astype(x.dtype)], axis=0)
    w_in2 = w_in.reshape(D_MODEL, IN_COLS)

    proj, f_proj = _inproj(x2d, norm_g, w_in2, 1024)
    proj_meta, f_meta = _inproj(meta_blk, norm_g, w_in2, META_ROWS)

    y_sb = _stick_breaking(proj, proj_meta, _suffix_sum_weights(), batch, seq)
    y_hg = _hgrn2(proj, f_proj, proj_meta, f_meta, hg_lb_logits, hg_norm_g,
                  _chunk_cumsum_weights(), _diag_reduce_weights(), batch, seq, 1024)

    out = _final(y_sb.reshape(batch * seq, SB_WIDTH), y_hg.reshape(batch * seq, HG_WIDTH), proj,
                 x2d, w_sb_out.reshape(SB_WIDTH, D_MODEL), w_hg_out.reshape(HG_WIDTH, D_MODEL),
                 w_out.reshape(D_MODEL, D_MODEL), final_norm_g.reshape(1, D_MODEL), 512)
    return out.reshape(batch, seq, D_MODEL)
```

```python
import functools

import jax
import jax.numpy as jnp
import numpy as np
from jax import lax
from jax.experimental import pallas as pl
from jax.experimental.pallas import tpu as pltpu

F32 = jnp.float32
BF16 = jnp.bfloat16

D_MODEL = 1024
N_META = 16
SB_HEADS = 8
SB_HEAD_DIM = 64
SB_WIDTH = SB_HEADS * SB_HEAD_DIM
SB_SCALE = SB_HEAD_DIM ** -0.5
HG_HEADS = 4
HG_DIM = 128
HG_WIDTH = HG_HEADS * HG_DIM
HG_SCALE = HG_DIM ** -0.5
IN_COLS = 4 * SB_WIDTH + 4 * HG_WIDTH + 2 * D_MODEL
EPS = 1e-6

LANES = 128
VMEM_LIMIT_BYTES = 56 * 1024 * 1024
INPROJ_VMEM_LIMIT_BYTES = 58 * 1024 * 1024

COL_TILE = 512
T_SB_Q, T_SB_K, T_SB_V, T_SB_GATE, T_HG_Q, T_HG_F, T_HG_I, T_HG_GATE = range(8)
N_COL_TILES = IN_COLS // COL_TILE

KEY_TILE = 128
META_ROWS = 128
HG_CHUNK = 64
HG_SUB = 8
HG_LEVELS = (8, 16, 32)


def _sigmoid(x):
    return 1.0 / (1.0 + jnp.exp(-x))


def _split_bf16(x):
    hi = x.astype(BF16)
    lo = (x - hi.astype(F32)).astype(BF16)
    return hi, lo


def _inproj_kernel(x_ref, g_ref, w_ref, o_ref, f_ref, u_ref, wb_ref, *, row_tile):
    n = pl.program_id(0)
    r = pl.program_id(1)
    rows = pl.ds(pl.multiple_of(r * row_tile, row_tile), row_tile)

    @pl.when(n == 0)
    def _():
        x = x_ref[...]
        ms = jnp.mean(x * x, axis=-1, keepdims=True)
        u_ref[rows, :] = (x * lax.rsqrt(ms + EPS) * g_ref[...]).astype(BF16)

    @pl.when(r == 0)
    def _():
        scale = jnp.where(n == T_SB_Q, SB_SCALE, 1.0).astype(F32)
        wb_ref[...] = (w_ref[...] * scale).astype(BF16)

    p = jnp.dot(u_ref[rows, :], wb_ref[...], preferred_element_type=F32)
    o_ref[...] = p.astype(BF16)

    @pl.when(n == T_HG_F)
    def _():
        f_ref[...] = p


def _inproj(x2d, norm_g, w_in, row_tile):
    rows = x2d.shape[0]
    assert rows % row_tile == 0
    last = rows // row_tile - 1

    def x_map(n, r):
        return (jnp.where(n == 0, r, last), 0)

    def f_map(n, r):
        return (jnp.where(n < T_HG_F, 0, jnp.where(n == T_HG_F, r, last)), 0)

    return pl.pallas_call(
        functools.partial(_inproj_kernel, row_tile=row_tile),
        grid=(N_COL_TILES, rows // row_tile),
        in_specs=[
            pl.BlockSpec((row_tile, D_MODEL), x_map),
            pl.BlockSpec((1, D_MODEL), lambda n, r: (0, 0)),
            pl.BlockSpec((D_MODEL, COL_TILE), lambda n, r: (0, n)),
        ],
        out_specs=[
            pl.BlockSpec((row_tile, COL_TILE), lambda n, r: (r, n)),
            pl.BlockSpec((row_tile, COL_TILE), f_map),
        ],
        out_shape=[
            jax.ShapeDtypeStruct((rows, IN_COLS), BF16),
            jax.ShapeDtypeStruct((rows, HG_WIDTH), F32),
        ],
        scratch_shapes=[pltpu.VMEM((rows, D_MODEL), BF16),
                        pltpu.VMEM((D_MODEL, COL_TILE), BF16)],
        compiler_params=pltpu.CompilerParams(
            dimension_semantics=("arbitrary", "arbitrary"),
            vmem_limit_bytes=INPROJ_VMEM_LIMIT_BYTES),
        name="inproj",
    )(x2d, norm_g, w_in)


SB_PAIRS = SB_WIDTH // LANES
SB_UNDERFLOW = -104.0


def _sb_kernel(q_ref, k_ref, v_ref, g_ref, mk_ref, mv_ref, uw_ref, o_ref, acc_ref, carry_ref):
    i = pl.program_id(1)
    lane = lax.broadcasted_iota(jnp.int32, (KEY_TILE, LANES), 1)
    first = lane < SB_HEAD_DIM
    uw = uw_ref[...]
    pair_lanes = [slice(p * LANES, (p + 1) * LANES) for p in range(SB_PAIRS)]
    q2 = []
    for sl in pair_lanes:
        q = q_ref[:, sl]
        zero = jnp.zeros_like(q)
        q2.append(jnp.concatenate([jnp.where(first, q, zero), jnp.where(first, zero, q)], axis=0))

    acc_ref[...] = jnp.zeros_like(acc_ref)
    carry_ref[...] = jnp.zeros_like(carry_ref)

    def tile(kt_ref, vt_ref, rows, valid):
        zs = [lax.dot_general(q2[p], kt_ref[rows, sl], (((1,), (1,)), ((), ())),
                              preferred_element_type=F32) for p, sl in enumerate(pair_lanes)]
        log_betas, sums = [], []
        for z in zs:
            log_beta = jnp.minimum(z, 0.0) - jnp.log(1.0 + jnp.exp(-jnp.abs(z)))
            log_1mb = log_beta - z
            if valid is not None:
                log_1mb = jnp.where(valid, log_1mb, 0.0)
            hi, lo = _split_bf16(log_1mb)
            sums.append(jnp.dot(jnp.concatenate([hi, lo], axis=1), uw, preferred_element_type=F32))
            log_betas.append(log_beta)
        ws = []
        for p in range(SB_PAIRS):
            carry = carry_ref[p]
            w = jnp.exp(log_betas[p] + sums[p][:, :KEY_TILE] + carry)
            if valid is not None:
                w = jnp.where(valid, w, 0.0)
            ws.append(w.astype(BF16))
            carry_ref[p] = carry + sums[p][:, KEY_TILE:]
        for p, sl in enumerate(pair_lanes):
            acc_ref[p] += jnp.dot(ws[p], vt_ref[rows, sl], preferred_element_type=F32)

    row2 = lax.broadcasted_iota(jnp.int32, (2 * KEY_TILE, KEY_TILE), 0) % KEY_TILE
    col2 = lax.broadcasted_iota(jnp.int32, (2 * KEY_TILE, KEY_TILE), 1)

    tile(k_ref, v_ref, pl.ds(pl.multiple_of(i * KEY_TILE, KEY_TILE), KEY_TILE), col2 < row2)

    def cond(c):
        it, live = c
        return jnp.logical_and(it < i, live > 0)

    def body(c):
        it, _ = c
        start = pl.multiple_of((i - 1 - it) * KEY_TILE, KEY_TILE)
        tile(k_ref, v_ref, pl.ds(start, KEY_TILE), None)
        top = jnp.max(carry_ref[...].reshape(-1, 8, LANES), axis=0)
        return it + 1, (jnp.max(top) > SB_UNDERFLOW).astype(jnp.int32)

    _, live = lax.while_loop(cond, body, (jnp.int32(0), jnp.int32(1)))

    @pl.when(live > 0)
    def _():
        tile(mk_ref, mv_ref, slice(None), col2 >= META_ROWS - N_META)

    for p, sl in enumerate(pair_lanes):
        acc = acc_ref[p]
        out = jnp.where(first, acc[:KEY_TILE], acc[KEY_TILE:])
        gate = g_ref[:, sl].astype(F32)
        o_ref[:, sl] = (out * (gate * _sigmoid(gate))).astype(BF16)


def _stick_breaking(proj, proj_meta, uw, batch, seq):
    assert SB_WIDTH == COL_TILE
    proj3 = proj.reshape(batch, seq, IN_COLS)
    return pl.pallas_call(
        _sb_kernel,
        grid=(batch, seq // KEY_TILE),
        in_specs=[
            pl.BlockSpec((None, KEY_TILE, SB_WIDTH), lambda b, i: (b, i, T_SB_Q)),
            pl.BlockSpec((None, seq, SB_WIDTH), lambda b, i: (b, 0, T_SB_K)),
            pl.BlockSpec((None, seq, SB_WIDTH), lambda b, i: (b, 0, T_SB_V)),
            pl.BlockSpec((None, KEY_TILE, SB_WIDTH), lambda b, i: (b, i, T_SB_GATE)),
            pl.BlockSpec((META_ROWS, SB_WIDTH), lambda b, i: (0, T_SB_K)),
            pl.BlockSpec((META_ROWS, SB_WIDTH), lambda b, i: (0, T_SB_V)),
            pl.BlockSpec((2 * KEY_TILE, 2 * KEY_TILE), lambda b, i: (0, 0)),
        ],
        out_specs=pl.BlockSpec((None, KEY_TILE, SB_WIDTH), lambda b, i: (b, i, 0)),
        out_shape=jax.ShapeDtypeStruct((batch, seq, SB_WIDTH), BF16),
        scratch_shapes=[pltpu.VMEM((SB_PAIRS, 2 * KEY_TILE, LANES), F32),
                        pltpu.VMEM((SB_PAIRS, 2 * KEY_TILE, LANES), F32)],
        compiler_params=pltpu.CompilerParams(
            dimension_semantics=("parallel", "arbitrary"),
            vmem_limit_bytes=VMEM_LIMIT_BYTES),
        name="stick_breaking",
    )(proj3, proj3, proj3, proj3, proj_meta, proj_meta, uw)


_NT = (((1,), (1,)), ((), ()))
_TN = (((0,), (0,)), ((), ()))


def _hg_decay(f_logit, lbv, tril2, pad_rows):
    f = lbv + (1.0 - lbv) * _sigmoid(f_logit)
    if pad_rows:
        row = lax.broadcasted_iota(jnp.int32, f.shape, 0)
        f = jnp.where(row >= pad_rows, f, 1.0)
    g_hi, g_lo = _split_bf16(jnp.log(f))
    b = jnp.dot(tril2, jnp.concatenate([g_hi, g_lo], axis=0), preferred_element_type=F32)
    return b, 1.0 - f


def _hg_next_state(state_t, b, kk, v):
    b_last = b[HG_CHUNK - 1:HG_CHUNK, :]
    k_dec = (kk * jnp.exp(b_last - b)).astype(BF16)
    return state_t * jnp.exp(b_last) + lax.dot_general(v, k_dec, _TN, preferred_element_type=F32)


def _hg_score_parts(qf, kk, b, e2):
    shape8 = (HG_CHUNK // HG_SUB, HG_SUB, HG_DIM)
    b8, q8, k8 = b.reshape(shape8), qf.reshape(shape8), kk.reshape(shape8)
    last = jnp.broadcast_to(b8[:, HG_SUB - 1:HG_SUB, :], shape8)
    parts = []
    for m in HG_LEVELS:
        per = m // HG_SUB
        idx = [(blk // (2 * per)) * 2 * per + per - 1 for blk in range(shape8[0])]
        ref = jnp.concatenate([last[i:i + 1] for i in idx], axis=0)
        e = jnp.exp(-jnp.abs(b8 - ref))
        qh = (q8 * e).reshape(HG_CHUNK, HG_DIM).astype(BF16)
        kh = (k8 * e).reshape(HG_CHUNK, HG_DIM).astype(BF16)
        parts.append(lax.dot_general(qh, kh, _NT, preferred_element_type=F32))
    cols = []
    for j in range(HG_SUB):
        e = jnp.exp(jnp.minimum(b8 - b8[:, j:j + 1, :], 0.0))
        cols.append((q8 * e * k8[:, j:j + 1, :]).reshape(HG_CHUNK, HG_DIM).astype(BF16))
    parts.append(jnp.dot(jnp.concatenate(cols, axis=1), e2, preferred_element_type=F32))
    return parts


def _hg_score_masks():
    row = lax.broadcasted_iota(jnp.int32, (HG_CHUNK, HG_CHUNK), 0)
    col = lax.broadcasted_iota(jnp.int32, (HG_CHUNK, HG_CHUNK), 1)
    masks = [((row // (2 * m)) == (col // (2 * m))) & ((row // m) % 2 == 1) & ((col // m) % 2 == 0)
             for m in HG_LEVELS]
    masks.append(((row // HG_SUB) == (col // HG_SUB)) & (col <= row))
    return masks


def _hg_kernel(q_ref, f_ref, v_ref, gate_ref, mf_ref, mv_ref, lbl_ref, gn_ref,
               tril_ref, e2_ref, o_ref, state_ref, *, chunks_per_tile):
    t = pl.program_id(1)
    tril2 = tril_ref[...]
    e2 = e2_ref[...]
    lbl = lbl_ref[...]
    ex = jnp.exp(lbl - jnp.max(lbl, axis=0, keepdims=True))
    lb = ex[0:1, :] / jnp.sum(ex, axis=0, keepdims=True)
    gn = gn_ref[...]
    head_lanes = [slice(h * HG_DIM, (h + 1) * HG_DIM) for h in range(HG_HEADS)]

    @pl.when(t == 0)
    def _():
        for h, sl in enumerate(head_lanes):
            b, kk = _hg_decay(mf_ref[:, sl], lb[:, sl], tril2, HG_CHUNK - N_META)
            state_ref[h] = _hg_next_state(jnp.zeros((HG_DIM, HG_DIM), F32), b, kk, mv_ref[:, sl])

    masks = _hg_score_masks()

    def body(c, carry):
        rows = pl.ds(pl.multiple_of(c * HG_CHUNK, HG_CHUNK), HG_CHUNK)
        decay = [_hg_decay(f_ref[rows, sl], lb[:, sl], tril2, 0) for sl in head_lanes]
        parts, inter = [], []
        for h, sl in enumerate(head_lanes):
            b, kk = decay[h]
            q = q_ref[rows, sl].astype(F32)
            qf = q * _sigmoid(q) * HG_SCALE
            parts.append(_hg_score_parts(qf, kk, b, e2))
            state_t = state_ref[h]
            inter.append(lax.dot_general((qf * jnp.exp(b)).astype(BF16), state_t.astype(BF16),
                                         _NT, preferred_element_type=F32))
            state_ref[h] = _hg_next_state(state_t, b, kk, v_ref[rows, sl])
        for h, sl in enumerate(head_lanes):
            a = jnp.zeros((HG_CHUNK, HG_CHUNK), F32)
            for mask, part in zip(masks, parts[h]):
                a = jnp.where(mask, part, a)
            o = inter[h] + jnp.dot(a.astype(BF16), v_ref[rows, sl], preferred_element_type=F32)
            o = o * lax.rsqrt(jnp.mean(o * o, axis=-1, keepdims=True) + EPS) * gn[:, sl]
            gate = gate_ref[rows, sl].astype(F32)
            o_ref[rows, sl] = (o * (gate * _sigmoid(gate))).astype(BF16)
        return carry

    lax.fori_loop(0, chunks_per_tile, body, 0)


def _hgrn2(proj, f_proj, proj_meta, f_meta, lb_logits, hg_norm_g, tril2, e2, batch, seq, row_tile):
    assert seq % row_tile == 0 and row_tile % HG_CHUNK == 0
    proj3 = proj.reshape(batch, seq, IN_COLS)
    f3 = f_proj.reshape(batch, seq, HG_WIDTH)
    meta_blk = META_ROWS // HG_CHUNK - 1
    const = lambda b, t: (0, 0)
    return pl.pallas_call(
        functools.partial(_hg_kernel, chunks_per_tile=row_tile // HG_CHUNK),
        grid=(batch, seq // row_tile),
        in_specs=[
            pl.BlockSpec((None, row_tile, HG_WIDTH), lambda b, t: (b, t, T_HG_Q)),
            pl.BlockSpec((None, row_tile, HG_WIDTH), lambda b, t: (b, t, 0)),
            pl.BlockSpec((None, row_tile, HG_WIDTH), lambda b, t: (b, t, T_HG_I)),
            pl.BlockSpec((None, row_tile, HG_WIDTH), lambda b, t: (b, t, T_HG_GATE)),
            pl.BlockSpec((HG_CHUNK, HG_WIDTH), lambda b, t: (meta_blk, 0)),
            pl.BlockSpec((HG_CHUNK, HG_WIDTH), lambda b, t: (meta_blk, T_HG_I)),
            pl.BlockSpec(lb_logits.shape, const),
            pl.BlockSpec((1, HG_WIDTH), const),
            pl.BlockSpec(tril2.shape, const),
            pl.BlockSpec(e2.shape, const),
        ],
        out_specs=pl.BlockSpec((None, row_tile, HG_WIDTH), lambda b, t: (b, t, 0)),
        out_shape=jax.ShapeDtypeStruct((batch, seq, HG_WIDTH), BF16),
        scratch_shapes=[pltpu.VMEM((HG_HEADS, HG_DIM, HG_DIM), F32)],
        compiler_params=pltpu.CompilerParams(
            dimension_semantics=("parallel", "arbitrary"),
            vmem_limit_bytes=VMEM_LIMIT_BYTES),
        name="hgrn2",
    )(proj3, f3, proj3, proj3, f_meta, proj_meta, lb_logits, hg_norm_g, tril2, e2)


def _final_kernel(ysb_ref, yhg_ref, gsb_ref, ghg_ref, x_ref, wsb_ref, whg_ref, wout_ref, fg_ref,
                  o_ref, wsb_s, whg_s, wout_s):
    @pl.when(pl.program_id(0) == 0)
    def _():
        wsb_s[...] = wsb_ref[...].astype(BF16)
        whg_s[...] = whg_ref[...].astype(BF16)
        wout_s[...] = wout_ref[...].astype(BF16)

    a = jnp.dot(ysb_ref[...], wsb_s[...], preferred_element_type=F32)
    b = jnp.dot(yhg_ref[...], whg_s[...], preferred_element_type=F32)
    merged = _sigmoid(gsb_ref[...].astype(F32)) * a + _sigmoid(ghg_ref[...].astype(F32)) * b
    h = x_ref[...] + jnp.dot(merged.astype(BF16), wout_s[...], preferred_element_type=F32)
    ms = jnp.mean(h * h, axis=-1, keepdims=True)
    o_ref[...] = h * lax.rsqrt(ms + EPS) * fg_ref[...]


def _final(y_sb, y_hg, proj, x2d, w_sb_out, w_hg_out, w_out, final_norm_g, row_tile):
    rows = x2d.shape[0]
    assert rows % row_tile == 0
    gate_blk = (4 * SB_WIDTH + 4 * HG_WIDTH) // D_MODEL
    const = lambda r: (0, 0)
    return pl.pallas_call(
        _final_kernel,
        grid=(rows // row_tile,),
        in_specs=[
            pl.BlockSpec((row_tile, SB_WIDTH), lambda r: (r, 0)),
            pl.BlockSpec((row_tile, HG_WIDTH), lambda r: (r, 0)),
            pl.BlockSpec((row_tile, D_MODEL), lambda r: (r, gate_blk)),
            pl.BlockSpec((row_tile, D_MODEL), lambda r: (r, gate_blk + 1)),
            pl.BlockSpec((row_tile, D_MODEL), lambda r: (r, 0)),
            pl.BlockSpec((SB_WIDTH, D_MODEL), const),
            pl.BlockSpec((HG_WIDTH, D_MODEL), const),
            pl.BlockSpec((D_MODEL, D_MODEL), const),
            pl.BlockSpec((1, D_MODEL), const),
        ],
        out_specs=pl.BlockSpec((row_tile, D_MODEL), lambda r: (r, 0)),
        out_shape=jax.ShapeDtypeStruct((rows, D_MODEL), F32),
        scratch_shapes=[pltpu.VMEM((SB_WIDTH, D_MODEL), BF16),
                        pltpu.VMEM((HG_WIDTH, D_MODEL), BF16),
                        pltpu.VMEM((D_MODEL, D_MODEL), BF16)],
        compiler_params=pltpu.CompilerParams(
            dimension_semantics=("arbitrary",),
            vmem_limit_bytes=VMEM_LIMIT_BYTES),
        name="merge_out",
    )(y_sb, y_hg, proj, proj, x2d, w_sb_out, w_hg_out, w_out, final_norm_g)


def _suffix_sum_weights():
    j = np.arange(2 * KEY_TILE)[:, None] % KEY_TILE
    s = np.arange(2 * KEY_TILE)[None, :]
    return jnp.asarray(np.where(s < KEY_TILE, j > s, True), BF16)


def _chunk_cumsum_weights():
    t = np.arange(HG_CHUNK)[:, None]
    s = np.arange(2 * HG_CHUNK)[None, :] % HG_CHUNK
    return jnp.asarray(s <= t, BF16)


def _diag_reduce_weights():
    j = np.arange(HG_SUB * HG_DIM)[:, None] // HG_DIM
    s = np.arange(HG_CHUNK)[None, :] % HG_SUB
    return jnp.asarray(j == s, BF16)


def kernel(x, meta, norm_g, w_in, w_sb_out, w_hg_out, w_out, hg_norm_g, hg_lb_logits, final_norm_g):
    batch, seq, d = x.shape
    assert d == D_MODEL and meta.shape == (N_META, D_MODEL)
    assert norm_g.shape[0] == 1 and w_in.shape == (1, D_MODEL, IN_COLS)
    assert seq % 1024 == 0

    x2d = x.reshape(batch * seq, D_MODEL)
    meta_blk = jnp.concatenate(
        [jnp.zeros((META_ROWS - N_META, D_MODEL), x.dtype), meta.astype(x.dtype)], axis=0)
    w_in2 = w_in.reshape(D_MODEL, IN_COLS)

    proj, f_proj = _inproj(x2d, norm_g, w_in2, 1024)
    proj_meta, f_meta = _inproj(meta_blk, norm_g, w_in2, META_ROWS)

    y_sb = _stick_breaking(proj, proj_meta, _suffix_sum_weights(), batch, seq)
    y_hg = _hgrn2(proj, f_proj, proj_meta, f_meta, hg_lb_logits, hg_norm_g,
                  _chunk_cumsum_weights(), _diag_reduce_weights(), batch, seq, 1024)

    out = _final(y_sb.reshape(batch * seq, SB_WIDTH), y_hg.reshape(batch * seq, HG_WIDTH), proj,
                 x2d, w_sb_out.reshape(SB_WIDTH, D_MODEL), w_hg_out.reshape(HG_WIDTH, D_MODEL),
                 w_out.reshape(D_MODEL, D_MODEL), final_norm_g.reshape(1, D_MODEL), 512)
    return out.reshape(batch, seq, D_MODEL)
```

```python
import functools

import jax
import jax.numpy as jnp
import numpy as np
from jax import lax
from jax.experimental import pallas as pl
from jax.experimental.pallas import tpu as pltpu

F32 = jnp.float32
BF16 = jnp.bfloat16

D_MODEL = 1024
N_META = 16
SB_HEADS = 8
SB_HEAD_DIM = 64
SB_WIDTH = SB_HEADS * SB_HEAD_DIM
SB_SCALE = SB_HEAD_DIM ** -0.5
HG_HEADS = 4
HG_DIM = 128
HG_WIDTH = HG_HEADS * HG_DIM
HG_SCALE = HG_DIM ** -0.5
IN_COLS = 4 * SB_WIDTH + 4 * HG_WIDTH + 2 * D_MODEL
EPS = 1e-6

LANES = 128
VMEM_LIMIT_BYTES = 56 * 1024 * 1024
INPROJ_VMEM_LIMIT_BYTES = 58 * 1024 * 1024

COL_TILE = 512
T_SB_Q, T_SB_K, T_SB_V, T_SB_GATE, T_HG_Q, T_HG_F, T_HG_I, T_HG_GATE = range(8)
N_COL_TILES = IN_COLS // COL_TILE

KEY_TILE = 128
META_ROWS = 128
HG_CHUNK = 64
HG_SUB = 8
HG_LEVELS = (8, 16, 32)


_NT = (((1,), (1,)), ((), ()))
_TN = (((0,), (0,)), ((), ()))


def _sigmoid(x):
    return 1.0 / (1.0 + jnp.exp(-x))


def _split_bf16(x):
    hi = x.astype(BF16)
    lo = (x - hi.astype(F32)).astype(BF16)
    return hi, lo


def _inproj_kernel(x_ref, g_ref, w_ref, o_ref, f_ref, u_ref, wb_ref, *, row_tile):
    n = pl.program_id(0)
    r = pl.program_id(1)
    rows = pl.ds(pl.multiple_of(r * row_tile, row_tile), row_tile)

    @pl.when(n == 0)
    def _():
        x = x_ref[...]
        ms = jnp.mean(x * x, axis=-1, keepdims=True)
        u_ref[rows, :] = (x * lax.rsqrt(ms + EPS) * g_ref[...]).astype(BF16)

    @pl.when(r == 0)
    def _():
        scale = jnp.where(n == T_SB_Q, SB_SCALE, 1.0).astype(F32)
        wb_ref[...] = (w_ref[...] * scale).astype(BF16)

    p = jnp.dot(u_ref[rows, :], wb_ref[...], preferred_element_type=F32)
    o_ref[...] = p.astype(BF16)

    @pl.when(n == T_HG_F)
    def _():
        f_ref[...] = p


def _inproj(x2d, norm_g, w_in, row_tile):
    rows = x2d.shape[0]
    assert rows % row_tile == 0
    last = rows // row_tile - 1

    def x_map(n, r):
        return (jnp.where(n == 0, r, last), 0)

    def f_map(n, r):
        return (jnp.where(n < T_HG_F, 0, jnp.where(n == T_HG_F, r, last)), 0)

    return pl.pallas_call(
        functools.partial(_inproj_kernel, row_tile=row_tile),
        grid=(N_COL_TILES, rows // row_tile),
        in_specs=[
            pl.BlockSpec((row_tile, D_MODEL), x_map),
            pl.BlockSpec((1, D_MODEL), lambda n, r: (0, 0)),
            pl.BlockSpec((D_MODEL, COL_TILE), lambda n, r: (0, n)),
        ],
        out_specs=[
            pl.BlockSpec((row_tile, COL_TILE), lambda n, r: (r, n)),
            pl.BlockSpec((row_tile, COL_TILE), f_map),
        ],
        out_shape=[
            jax.ShapeDtypeStruct((rows, IN_COLS), BF16),
            jax.ShapeDtypeStruct((rows, HG_WIDTH), F32),
        ],
        scratch_shapes=[pltpu.VMEM((rows, D_MODEL), BF16),
                        pltpu.VMEM((D_MODEL, COL_TILE), BF16)],
        compiler_params=pltpu.CompilerParams(
            dimension_semantics=("arbitrary", "arbitrary"),
            vmem_limit_bytes=INPROJ_VMEM_LIMIT_BYTES),
        name="inproj",
    )(x2d, norm_g, w_in)


SB_PAIRS = SB_WIDTH // LANES
SB_UNDERFLOW = -104.0
SB_FIRST_SWEEP = 3


def _sb_kernel(q_ref, k_ref, v_ref, g_ref, mk_ref, mv_ref, uw_ref, o_ref, acc_ref, carry_ref):
    i = pl.program_id(1)
    lane = lax.broadcasted_iota(jnp.int32, (KEY_TILE, LANES), 1)
    first = lane < SB_HEAD_DIM
    uw = uw_ref[...]
    pair_lanes = [slice(p * LANES, (p + 1) * LANES) for p in range(SB_PAIRS)]
    q2 = []
    for sl in pair_lanes:
        q = q_ref[:, sl]
        zero = jnp.zeros_like(q)
        q2.append(jnp.concatenate([jnp.where(first, q, zero), jnp.where(first, zero, q)], axis=0))

    acc_ref[...] = jnp.zeros_like(acc_ref)
    carry_ref[...] = jnp.zeros_like(carry_ref)

    def sweep(kt_ref, vt_ref, rows, valids):
        n = len(valids)
        zs = [lax.dot_general(q2[p], kt_ref[rows, sl], _NT, preferred_element_type=F32)
              for p, sl in enumerate(pair_lanes)]
        log_betas, sums = [], []
        for z in zs:
            log_beta_t, hi_lo = [], []
            for t in range(n):
                zt = z[:, t * KEY_TILE:(t + 1) * KEY_TILE]
                log_beta = jnp.minimum(zt, 0.0) - jnp.log(1.0 + jnp.exp(-jnp.abs(zt)))
                log_1mb = log_beta - zt
                if valids[t] is not None:
                    log_1mb = jnp.where(valids[t], log_1mb, 0.0)
                hi, lo = _split_bf16(log_1mb)
                hi_lo.append(jnp.concatenate([hi, lo], axis=1))
                log_beta_t.append(log_beta)
            sums.append(jnp.dot(jnp.concatenate(hi_lo, axis=0), uw, preferred_element_type=F32))
            log_betas.append(log_beta_t)
        ws = []
        for p in range(SB_PAIRS):
            carry = carry_ref[p]
            w_t = [None] * n
            for t in reversed(range(n)):
                cs = sums[p][t * 2 * KEY_TILE:(t + 1) * 2 * KEY_TILE]
                w = jnp.exp(log_betas[p][t] + cs[:, :KEY_TILE] + carry)
                if valids[t] is not None:
                    w = jnp.where(valids[t], w, 0.0)
                w_t[t] = w.astype(BF16)
                carry = carry + cs[:, KEY_TILE:]
            carry_ref[p] = carry
            ws.append(jnp.concatenate(w_t, axis=1))
        for p, sl in enumerate(pair_lanes):
            acc_ref[p] += jnp.dot(ws[p], vt_ref[rows, sl], preferred_element_type=F32)

    row2 = lax.broadcasted_iota(jnp.int32, (2 * KEY_TILE, KEY_TILE), 0) % KEY_TILE
    col2 = lax.broadcasted_iota(jnp.int32, (2 * KEY_TILE, KEY_TILE), 1)
    causal = col2 < row2

    fused = i >= SB_FIRST_SWEEP - 1

    @pl.when(fused)
    def _():
        start = pl.multiple_of((i - (SB_FIRST_SWEEP - 1)) * KEY_TILE, KEY_TILE)
        sweep(k_ref, v_ref, pl.ds(start, SB_FIRST_SWEEP * KEY_TILE),
              [None] * (SB_FIRST_SWEEP - 1) + [causal])

    @pl.when(jnp.logical_not(fused))
    def _():
        sweep(k_ref, v_ref, pl.ds(pl.multiple_of(i * KEY_TILE, KEY_TILE), KEY_TILE), [causal])

    def live_rows():
        top = jnp.max(carry_ref[...].reshape(-1, 8, LANES), axis=0)
        return (jnp.max(top) > SB_UNDERFLOW).astype(jnp.int32)

    def cond(c):
        it, live = c
        return jnp.logical_and(it < i, live > 0)

    def body(c):
        it, _ = c
        start = pl.multiple_of((i - 1 - it) * KEY_TILE, KEY_TILE)
        sweep(k_ref, v_ref, pl.ds(start, KEY_TILE), [None])
        return it + 1, live_rows()

    visited = jnp.where(fused, SB_FIRST_SWEEP - 1, 0).astype(jnp.int32)
    _, live = lax.while_loop(cond, body, (visited, live_rows()))

    @pl.when(live > 0)
    def _():
        sweep(mk_ref, mv_ref, slice(None), [col2 >= META_ROWS - N_META])

    for p, sl in enumerate(pair_lanes):
        acc = acc_ref[p]
        out = jnp.where(first, acc[:KEY_TILE], acc[KEY_TILE:])
        gate = g_ref[:, sl].astype(F32)
        o_ref[:, sl] = (out * (gate * _sigmoid(gate))).astype(BF16)


def _stick_breaking(proj, proj_meta, uw, batch, seq):
    assert SB_WIDTH == COL_TILE
    proj3 = proj.reshape(batch, seq, IN_COLS)
    return pl.pallas_call(
        _sb_kernel,
        grid=(batch, seq // KEY_TILE),
        in_specs=[
            pl.BlockSpec((None, KEY_TILE, SB_WIDTH), lambda b, i: (b, i, T_SB_Q)),
            pl.BlockSpec((None, seq, SB_WIDTH), lambda b, i: (b, 0, T_SB_K)),
            pl.BlockSpec((None, seq, SB_WIDTH), lambda b, i: (b, 0, T_SB_V)),
            pl.BlockSpec((None, KEY_TILE, SB_WIDTH), lambda b, i: (b, i, T_SB_GATE)),
            pl.BlockSpec((META_ROWS, SB_WIDTH), lambda b, i: (0, T_SB_K)),
            pl.BlockSpec((META_ROWS, SB_WIDTH), lambda b, i: (0, T_SB_V)),
            pl.BlockSpec((2 * KEY_TILE, 2 * KEY_TILE), lambda b, i: (0, 0)),
        ],
        out_specs=pl.BlockSpec((None, KEY_TILE, SB_WIDTH), lambda b, i: (b, i, 0)),
        out_shape=jax.ShapeDtypeStruct((batch, seq, SB_WIDTH), BF16),
        scratch_shapes=[pltpu.VMEM((SB_PAIRS, 2 * KEY_TILE, LANES), F32),
                        pltpu.VMEM((SB_PAIRS, 2 * KEY_TILE, LANES), F32)],
        compiler_params=pltpu.CompilerParams(
            dimension_semantics=("parallel", "arbitrary"),
            vmem_limit_bytes=VMEM_LIMIT_BYTES),
        name="stick_breaking",
    )(proj3, proj3, proj3, proj3, proj_meta, proj_meta, uw)


def _hg_decay(f_logit, lbv, tril2, pad_rows):
    f = lbv + (1.0 - lbv) * _sigmoid(f_logit)
    if pad_rows:
        row = lax.broadcasted_iota(jnp.int32, f.shape, 0)
        f = jnp.where(row >= pad_rows, f, 1.0)
    g_hi, g_lo = _split_bf16(jnp.log(f))
    b = jnp.dot(tril2, jnp.concatenate([g_hi, g_lo], axis=0), preferred_element_type=F32)
    return b, 1.0 - f


def _hg_next_state(state_t, b, kk, v):
    b_last = b[HG_CHUNK - 1:HG_CHUNK, :]
    k_dec = (kk * jnp.exp(b_last - b)).astype(BF16)
    return state_t * jnp.exp(b_last) + lax.dot_general(v, k_dec, _TN, preferred_element_type=F32)


def _hg_score_parts(qf, kk, b, e2):
    shape8 = (HG_CHUNK // HG_SUB, HG_SUB, HG_DIM)
    b8, q8, k8 = b.reshape(shape8), qf.reshape(shape8), kk.reshape(shape8)
    last = jnp.broadcast_to(b8[:, HG_SUB - 1:HG_SUB, :], shape8)
    parts = []
    for m in HG_LEVELS:
        per = m // HG_SUB
        idx = [(blk // (2 * per)) * 2 * per + per - 1 for blk in range(shape8[0])]
        ref = jnp.concatenate([last[i:i + 1] for i in idx], axis=0)
        e = jnp.exp(-jnp.abs(b8 - ref))
        qh = (q8 * e).reshape(HG_CHUNK, HG_DIM).astype(BF16)
        kh = (k8 * e).reshape(HG_CHUNK, HG_DIM).astype(BF16)
        parts.append(lax.dot_general(qh, kh, _NT, preferred_element_type=F32))
    cols = []
    for j in range(HG_SUB):
        e = jnp.exp(jnp.minimum(b8 - b8[:, j:j + 1, :], 0.0))
        cols.append((q8 * e * k8[:, j:j + 1, :]).reshape(HG_CHUNK, HG_DIM).astype(BF16))
    parts.append(jnp.dot(jnp.concatenate(cols, axis=1), e2, preferred_element_type=F32))
    return parts


def _hg_score_masks():
    row = lax.broadcasted_iota(jnp.int32, (HG_CHUNK, HG_CHUNK), 0)
    col = lax.broadcasted_iota(jnp.int32, (HG_CHUNK, HG_CHUNK), 1)
    masks = [((row // (2 * m)) == (col // (2 * m))) & ((row // m) % 2 == 1) & ((col // m) % 2 == 0)
             for m in HG_LEVELS]
    masks.append(((row // HG_SUB) == (col // HG_SUB)) & (col <= row))
    return masks


def _hg_kernel(q_ref, f_ref, v_ref, gate_ref, mf_ref, mv_ref, lbl_ref, gn_ref,
               tril_ref, e2_ref, o_ref, state_ref, *, chunks_per_tile):
    t = pl.program_id(1)
    tril2 = tril_ref[...]
    e2 = e2_ref[...]
    lbl = lbl_ref[...]
    ex = jnp.exp(lbl - jnp.max(lbl, axis=0, keepdims=True))
    lb = ex[0:1, :] / jnp.sum(ex, axis=0, keepdims=True)
    gn = gn_ref[...]
    head_lanes = [slice(h * HG_DIM, (h + 1) * HG_DIM) for h in range(HG_HEADS)]

    @pl.when(t == 0)
    def _():
        for h, sl in enumerate(head_lanes):
            b, kk = _hg_decay(mf_ref[:, sl], lb[:, sl], tril2, HG_CHUNK - N_META)
            state_ref[h] = _hg_next_state(jnp.zeros((HG_DIM, HG_DIM), F32), b, kk, mv_ref[:, sl])

    masks = _hg_score_masks()

    def body(c, carry):
        rows = pl.ds(pl.multiple_of(c * HG_CHUNK, HG_CHUNK), HG_CHUNK)
        decay = [_hg_decay(f_ref[rows, sl], lb[:, sl], tril2, 0) for sl in head_lanes]
        parts, inter = [], []
        for h, sl in enumerate(head_lanes):
            b, kk = decay[h]
            q = q_ref[rows, sl].astype(F32)
            qf = q * _sigmoid(q) * HG_SCALE
            parts.append(_hg_score_parts(qf, kk, b, e2))
            state_t = state_ref[h]
            inter.append(lax.dot_general((qf * jnp.exp(b)).astype(BF16), state_t.astype(BF16),
                                         _NT, preferred_element_type=F32))
            state_ref[h] = _hg_next_state(state_t, b, kk, v_ref[rows, sl])
        for h, sl in enumerate(head_lanes):
            a = jnp.zeros((HG_CHUNK, HG_CHUNK), F32)
            for mask, part in zip(masks, parts[h]):
                a = jnp.where(mask, part, a)
            o = inter[h] + jnp.dot(a.astype(BF16), v_ref[rows, sl], preferred_element_type=F32)
            o = o * lax.rsqrt(jnp.mean(o * o, axis=-1, keepdims=True) + EPS) * gn[:, sl]
            gate = gate_ref[rows, sl].astype(F32)
            o_ref[rows, sl] = (o * (gate * _sigmoid(gate))).astype(BF16)
        return carry

    lax.fori_loop(0, chunks_per_tile, body, 0)


def _hgrn2(proj, f_proj, proj_meta, f_meta, lb_logits, hg_norm_g, tril2, e2, batch, seq, row_tile):
    assert seq % row_tile == 0 and row_tile % HG_CHUNK == 0
    proj3 = proj.reshape(batch, seq, IN_COLS)
    f3 = f_proj.reshape(batch, seq, HG_WIDTH)
    meta_blk = META_ROWS // HG_CHUNK - 1
    const = lambda b, t: (0, 0)
    return pl.pallas_call(
        functools.partial(_hg_kernel, chunks_per_tile=row_tile // HG_CHUNK),
        grid=(batch, seq // row_tile),
        in_specs=[
            pl.BlockSpec((None, row_tile, HG_WIDTH), lambda b, t: (b, t, T_HG_Q)),
            pl.BlockSpec((None, row_tile, HG_WIDTH), lambda b, t: (b, t, 0)),
            pl.BlockSpec((None, row_tile, HG_WIDTH), lambda b, t: (b, t, T_HG_I)),
            pl.BlockSpec((None, row_tile, HG_WIDTH), lambda b, t: (b, t, T_HG_GATE)),
            pl.BlockSpec((HG_CHUNK, HG_WIDTH), lambda b, t: (meta_blk, 0)),
            pl.BlockSpec((HG_CHUNK, HG_WIDTH), lambda b, t: (meta_blk, T_HG_I)),
            pl.BlockSpec(lb_logits.shape, const),
            pl.BlockSpec((1, HG_WIDTH), const),
            pl.BlockSpec(tril2.shape, const),
            pl.BlockSpec(e2.shape, const),
        ],
        out_specs=pl.BlockSpec((None, row_tile, HG_WIDTH), lambda b, t: (b, t, 0)),
        out_shape=jax.ShapeDtypeStruct((batch, seq, HG_WIDTH), BF16),
        scratch_shapes=[pltpu.VMEM((HG_HEADS, HG_DIM, HG_DIM), F32)],
        compiler_params=pltpu.CompilerParams(
            dimension_semantics=("parallel", "arbitrary"),
            vmem_limit_bytes=VMEM_LIMIT_BYTES),
        name="hgrn2",
    )(proj3, f3, proj3, proj3, f_meta, proj_meta, lb_logits, hg_norm_g, tril2, e2)


def _final_kernel(ysb_ref, yhg_ref, gsb_ref, ghg_ref, x_ref, wsb_ref, whg_ref, wout_ref, fg_ref,
                  o_ref, wsb_s, whg_s, wout_s):
    @pl.when(pl.program_id(0) == 0)
    def _():
        wsb_s[...] = wsb_ref[...].astype(BF16)
        whg_s[...] = whg_ref[...].astype(BF16)
        wout_s[...] = wout_ref[...].astype(BF16)

    a = jnp.dot(ysb_ref[...], wsb_s[...], preferred_element_type=F32)
    b = jnp.dot(yhg_ref[...], whg_s[...], preferred_element_type=F32)
    merged = _sigmoid(gsb_ref[...].astype(F32)) * a + _sigmoid(ghg_ref[...].astype(F32)) * b
    h = x_ref[...] + jnp.dot(merged.astype(BF16), wout_s[...], preferred_element_type=F32)
    ms = jnp.mean(h * h, axis=-1, keepdims=True)
    o_ref[...] = h * lax.rsqrt(ms + EPS) * fg_ref[...]


def _final(y_sb, y_hg, proj, x2d, w_sb_out, w_hg_out, w_out, final_norm_g, row_tile):
    rows = x2d.shape[0]
    assert rows % row_tile == 0
    gate_blk = (4 * SB_WIDTH + 4 * HG_WIDTH) // D_MODEL
    const = lambda r: (0, 0)
    return pl.pallas_call(
        _final_kernel,
        grid=(rows // row_tile,),
        in_specs=[
            pl.BlockSpec((row_tile, SB_WIDTH), lambda r: (r, 0)),
            pl.BlockSpec((row_tile, HG_WIDTH), lambda r: (r, 0)),
            pl.BlockSpec((row_tile, D_MODEL), lambda r: (r, gate_blk)),
            pl.BlockSpec((row_tile, D_MODEL), lambda r: (r, gate_blk + 1)),
            pl.BlockSpec((row_tile, D_MODEL), lambda r: (r, 0)),
            pl.BlockSpec((SB_WIDTH, D_MODEL), const),
            pl.BlockSpec((HG_WIDTH, D_MODEL), const),
            pl.BlockSpec((D_MODEL, D_MODEL), const),
            pl.BlockSpec((1, D_MODEL), const),
        ],
        out_specs=pl.BlockSpec((row_tile, D_MODEL), lambda r: (r, 0)),
        out_shape=jax.ShapeDtypeStruct((rows, D_MODEL), F32),
        scratch_shapes=[pltpu.VMEM((SB_WIDTH, D_MODEL), BF16),
                        pltpu.VMEM((HG_WIDTH, D_MODEL), BF16),
                        pltpu.VMEM((D_MODEL, D_MODEL), BF16)],
        compiler_params=pltpu.CompilerParams(
            dimension_semantics=("arbitrary",),
            vmem_limit_bytes=VMEM_LIMIT_BYTES),
        name="merge_out",
    )(y_sb, y_hg, proj, proj, x2d, w_sb_out, w_hg_out, w_out, final_norm_g)


def _suffix_sum_weights():
    j = np.arange(2 * KEY_TILE)[:, None] % KEY_TILE
    s = np.arange(2 * KEY_TILE)[None, :]
    return jnp.asarray(np.where(s < KEY_TILE, j > s, True), BF16)


def _chunk_cumsum_weights():
    t = np.arange(HG_CHUNK)[:, None]
    s = np.arange(2 * HG_CHUNK)[None, :] % HG_CHUNK
    return jnp.asarray(s <= t, BF16)


def _diag_reduce_weights():
    j = np.arange(HG_SUB * HG_DIM)[:, None] // HG_DIM
    s = np.arange(HG_CHUNK)[None, :] % HG_SUB
    return jnp.asarray(j == s, BF16)


def kernel(x, meta, norm_g, w_in, w_sb_out, w_hg_out, w_out, hg_norm_g, hg_lb_logits, final_norm_g):
    batch, seq, d = x.shape
    assert d == D_MODEL and meta.shape == (N_META, D_MODEL)
    assert norm_g.shape[0] == 1 and w_in.shape == (1, D_MODEL, IN_COLS)
    assert seq % 1024 == 0

    x2d = x.reshape(batch * seq, D_MODEL)
    meta_blk = jnp.concatenate(
        [jnp.zeros((META_ROWS - N_META, D_MODEL), x.dtype), meta.astype(x.dtype)], axis=0)
    w_in2 = w_in.reshape(D_MODEL, IN_COLS)

    proj, f_proj = _inproj(x2d, norm_g, w_in2, 1024)
    proj_meta, f_meta = _inproj(meta_blk, norm_g, w_in2, META_ROWS)

    y_sb = _stick_breaking(proj, proj_meta, _suffix_sum_weights(), batch, seq)
    y_hg = _hgrn2(proj, f_proj, proj_meta, f_meta, hg_lb_logits, hg_norm_g,
                  _chunk_cumsum_weights(), _diag_reduce_weights(), batch, seq, 1024)

    out = _final(y_sb.reshape(batch * seq, SB_WIDTH), y_hg.reshape(batch * seq, HG_WIDTH), proj,
                 x2d, w_sb_out.reshape(SB_WIDTH, D_MODEL), w_hg_out.reshape(HG_WIDTH, D_MODEL),
                 w_out.reshape(D_MODEL, D_MODEL), final_norm_g.reshape(1, D_MODEL), 512)
    return out.reshape(batch, seq, D_MODEL)
```

```python
import functools

import jax
import jax.numpy as jnp
import numpy as np
from jax import lax
from jax.experimental import pallas as pl
from jax.experimental.pallas import tpu as pltpu

F32 = jnp.float32
BF16 = jnp.bfloat16

D_MODEL = 1024
N_META = 16
SB_HEADS = 8
SB_HEAD_DIM = 64
SB_WIDTH = SB_HEADS * SB_HEAD_DIM
SB_SCALE = SB_HEAD_DIM ** -0.5
HG_HEADS = 4
HG_DIM = 128
HG_WIDTH = HG_HEADS * HG_DIM
HG_SCALE = HG_DIM ** -0.5
IN_COLS = 4 * SB_WIDTH + 4 * HG_WIDTH + 2 * D_MODEL
EPS = 1e-6

LANES = 128
VMEM_LIMIT_BYTES = 56 * 1024 * 1024

COL_TILE = 512
T_SB_Q, T_SB_K, T_SB_V, T_SB_GATE, T_HG_Q, T_HG_F, T_HG_I, T_HG_GATE = range(8)
N_COL_TILES = IN_COLS // COL_TILE

KEY_TILE = 128
META_ROWS = 128
HG_CHUNK = 64
HG_SUB = 8
HG_LEVELS = (8, 16, 32)


_NT = (((1,), (1,)), ((), ()))
_TN = (((0,), (0,)), ((), ()))


def _sigmoid(x):
    return 1.0 / (1.0 + jnp.exp(-x))


def _split_bf16(x):
    hi = x.astype(BF16)
    lo = (x - hi.astype(F32)).astype(BF16)
    return hi, lo


def _inproj_kernel(x_ref, g_ref, w_ref, o_ref, f_ref, u_ref):
    n = pl.program_id(1)

    @pl.when(n == 0)
    def _():
        x = x_ref[...]
        ms = jnp.mean(x * x, axis=-1, keepdims=True)
        u_ref[...] = (x * lax.rsqrt(ms + EPS) * g_ref[...]).astype(BF16)

    scale = jnp.where(n == T_SB_Q, SB_SCALE, 1.0).astype(F32)
    p = jnp.dot(u_ref[...], (w_ref[...] * scale).astype(BF16), preferred_element_type=F32)
    o_ref[...] = p.astype(BF16)

    @pl.when(n == T_HG_F)
    def _():
        f_ref[...] = p


def _inproj(x2d, norm_g, w_in, row_tile):
    rows = x2d.shape[0]
    assert rows % row_tile == 0
    return pl.pallas_call(
        _inproj_kernel,
        grid=(rows // row_tile, N_COL_TILES),
        in_specs=[
            pl.BlockSpec((row_tile, D_MODEL), lambda r, n: (r, 0)),
            pl.BlockSpec((1, D_MODEL), lambda r, n: (0, 0)),
            pl.BlockSpec((D_MODEL, COL_TILE), lambda r, n: (0, n)),
        ],
        out_specs=[
            pl.BlockSpec((row_tile, COL_TILE), lambda r, n: (r, n)),
            pl.BlockSpec((row_tile, COL_TILE), lambda r, n: (r, 0)),
        ],
        out_shape=[
            jax.ShapeDtypeStruct((rows, IN_COLS), BF16),
            jax.ShapeDtypeStruct((rows, HG_WIDTH), F32),
        ],
        scratch_shapes=[pltpu.VMEM((row_tile, D_MODEL), BF16)],
        compiler_params=pltpu.CompilerParams(
            dimension_semantics=("parallel", "arbitrary"),
            vmem_limit_bytes=VMEM_LIMIT_BYTES),
        name="inproj",
    )(x2d, norm_g, w_in)


SB_PAIRS = SB_WIDTH // LANES
SB_UNDERFLOW = -104.0
SB_FIRST_SWEEP = 3


def _sb_kernel(q_ref, k_ref, v_ref, g_ref, mk_ref, mv_ref, uw_ref, o_ref, acc_ref, carry_ref):
    i = pl.program_id(1)
    lane = lax.broadcasted_iota(jnp.int32, (KEY_TILE, LANES), 1)
    first = lane < SB_HEAD_DIM
    uw = uw_ref[...]
    pair_lanes = [slice(p * LANES, (p + 1) * LANES) for p in range(SB_PAIRS)]
    q2 = []
    for sl in pair_lanes:
        q = q_ref[:, sl]
        zero = jnp.zeros_like(q)
        q2.append(jnp.concatenate([jnp.where(first, q, zero), jnp.where(first, zero, q)], axis=0))

    acc_ref[...] = jnp.zeros_like(acc_ref)
    carry_ref[...] = jnp.zeros_like(carry_ref)

    def sweep(kt_ref, vt_ref, rows, valids):
        n = len(valids)
        zs = [lax.dot_general(q2[p], kt_ref[rows, sl], _NT, preferred_element_type=F32)
              for p, sl in enumerate(pair_lanes)]
        log_betas, sums = [], []
        for z in zs:
            log_beta_t, hi_lo = [], []
            for t in range(n):
                zt = z[:, t * KEY_TILE:(t + 1) * KEY_TILE]
                log_beta = jnp.minimum(zt, 0.0) - jnp.log(1.0 + jnp.exp(-jnp.abs(zt)))
                log_1mb = log_beta - zt
                if valids[t] is not None:
                    log_1mb = jnp.where(valids[t], log_1mb, 0.0)
                hi, lo = _split_bf16(log_1mb)
                hi_lo.append(jnp.concatenate([hi, lo], axis=1))
                log_beta_t.append(log_beta)
            sums.append(jnp.dot(jnp.concatenate(hi_lo, axis=0), uw, preferred_element_type=F32))
            log_betas.append(log_beta_t)
        ws = []
        for p in range(SB_PAIRS):
            carry = carry_ref[p]
            w_t = [None] * n
            for t in reversed(range(n)):
                cs = sums[p][t * 2 * KEY_TILE:(t + 1) * 2 * KEY_TILE]
                w = jnp.exp(log_betas[p][t] + cs[:, :KEY_TILE] + carry)
                if valids[t] is not None:
                    w = jnp.where(valids[t], w, 0.0)
                w_t[t] = w.astype(BF16)
                carry = carry + cs[:, KEY_TILE:]
            carry_ref[p] = carry
            ws.append(jnp.concatenate(w_t, axis=1))
        for p, sl in enumerate(pair_lanes):
            acc_ref[p] += jnp.dot(ws[p], vt_ref[rows, sl], preferred_element_type=F32)

    row2 = lax.broadcasted_iota(jnp.int32, (2 * KEY_TILE, KEY_TILE), 0) % KEY_TILE
    col2 = lax.broadcasted_iota(jnp.int32, (2 * KEY_TILE, KEY_TILE), 1)
    causal = col2 < row2

    fused = i >= SB_FIRST_SWEEP - 1

    @pl.when(fused)
    def _():
        start = pl.multiple_of((i - (SB_FIRST_SWEEP - 1)) * KEY_TILE, KEY_TILE)
        sweep(k_ref, v_ref, pl.ds(start, SB_FIRST_SWEEP * KEY_TILE),
              [None] * (SB_FIRST_SWEEP - 1) + [causal])

    @pl.when(jnp.logical_not(fused))
    def _():
        sweep(k_ref, v_ref, pl.ds(pl.multiple_of(i * KEY_TILE, KEY_TILE), KEY_TILE), [causal])

    def live_rows():
        top = jnp.max(carry_ref[...].reshape(-1, 8, LANES), axis=0)
        return (jnp.max(top) > SB_UNDERFLOW).astype(jnp.int32)

    def cond(c):
        it, live = c
        return jnp.logical_and(it < i, live > 0)

    def body(c):
        it, _ = c
        start = pl.multiple_of((i - 1 - it) * KEY_TILE, KEY_TILE)
        sweep(k_ref, v_ref, pl.ds(start, KEY_TILE), [None])
        return it + 1, live_rows()

    visited = jnp.where(fused, SB_FIRST_SWEEP - 1, 0).astype(jnp.int32)
    _, live = lax.while_loop(cond, body, (visited, live_rows()))

    @pl.when(live > 0)
    def _():
        sweep(mk_ref, mv_ref, slice(None), [col2 >= META_ROWS - N_META])

    for p, sl in enumerate(pair_lanes):
        acc = acc_ref[p]
        out = jnp.where(first, acc[:KEY_TILE], acc[KEY_TILE:])
        gate = g_ref[:, sl].astype(F32)
        o_ref[:, sl] = (out * (gate * _sigmoid(gate))).astype(BF16)


def _stick_breaking(proj, proj_meta, uw, batch, seq):
    assert SB_WIDTH == COL_TILE
    proj3 = proj.reshape(batch, seq, IN_COLS)
    return pl.pallas_call(
        _sb_kernel,
        grid=(batch, seq // KEY_TILE),
        in_specs=[
            pl.BlockSpec((None, KEY_TILE, SB_WIDTH), lambda b, i: (b, i, T_SB_Q)),
            pl.BlockSpec((None, seq, SB_WIDTH), lambda b, i: (b, 0, T_SB_K)),
            pl.BlockSpec((None, seq, SB_WIDTH), lambda b, i: (b, 0, T_SB_V)),
            pl.BlockSpec((None, KEY_TILE, SB_WIDTH), lambda b, i: (b, i, T_SB_GATE)),
            pl.BlockSpec((META_ROWS, SB_WIDTH), lambda b, i: (0, T_SB_K)),
            pl.BlockSpec((META_ROWS, SB_WIDTH), lambda b, i: (0, T_SB_V)),
            pl.BlockSpec((2 * KEY_TILE, 2 * KEY_TILE), lambda b, i: (0, 0)),
        ],
        out_specs=pl.BlockSpec((None, KEY_TILE, SB_WIDTH), lambda b, i: (b, i, 0)),
        out_shape=jax.ShapeDtypeStruct((batch, seq, SB_WIDTH), BF16),
        scratch_shapes=[pltpu.VMEM((SB_PAIRS, 2 * KEY_TILE, LANES), F32),
                        pltpu.VMEM((SB_PAIRS, 2 * KEY_TILE, LANES), F32)],
        compiler_params=pltpu.CompilerParams(
            dimension_semantics=("parallel", "arbitrary"),
            vmem_limit_bytes=VMEM_LIMIT_BYTES),
        name="stick_breaking",
    )(proj3, proj3, proj3, proj3, proj_meta, proj_meta, uw)


def _hg_decay(f_logit, lbv, tril2, pad_rows):
    f = lbv + (1.0 - lbv) * _sigmoid(f_logit)
    if pad_rows:
        row = lax.broadcasted_iota(jnp.int32, f.shape, 0)
        f = jnp.where(row >= pad_rows, f, 1.0)
    g_hi, g_lo = _split_bf16(jnp.log(f))
    b = jnp.dot(tril2, jnp.concatenate([g_hi, g_lo], axis=0), preferred_element_type=F32)
    return b, 1.0 - f


def _hg_next_state(state_t, b, kk, v):
    b_last = b[HG_CHUNK - 1:HG_CHUNK, :]
    k_dec = (kk * jnp.exp(b_last - b)).astype(BF16)
    return state_t * jnp.exp(b_last) + lax.dot_general(v, k_dec, _TN, preferred_element_type=F32)


def _hg_score_parts(qf, kk, b, e2):
    shape8 = (HG_CHUNK // HG_SUB, HG_SUB, HG_DIM)
    b8, q8, k8 = b.reshape(shape8), qf.reshape(shape8), kk.reshape(shape8)
    last = jnp.broadcast_to(b8[:, HG_SUB - 1:HG_SUB, :], shape8)
    parts = []
    for m in HG_LEVELS:
        per = m // HG_SUB
        idx = [(blk // (2 * per)) * 2 * per + per - 1 for blk in range(shape8[0])]
        ref = jnp.concatenate([last[i:i + 1] for i in idx], axis=0)
        e = jnp.exp(-jnp.abs(b8 - ref))
        qh = (q8 * e).reshape(HG_CHUNK, HG_DIM).astype(BF16)
        kh = (k8 * e).reshape(HG_CHUNK, HG_DIM).astype(BF16)
        parts.append(lax.dot_general(qh, kh, _NT, preferred_element_type=F32))
    cols = []
    for j in range(HG_SUB):
        e = jnp.exp(jnp.minimum(b8 - b8[:, j:j + 1, :], 0.0))
        cols.append((q8 * e * k8[:, j:j + 1, :]).reshape(HG_CHUNK, HG_DIM).astype(BF16))
    parts.append(jnp.dot(jnp.concatenate(cols, axis=1), e2, preferred_element_type=F32))
    return parts


def _hg_score_masks():
    row = lax.broadcasted_iota(jnp.int32, (HG_CHUNK, HG_CHUNK), 0)
    col = lax.broadcasted_iota(jnp.int32, (HG_CHUNK, HG_CHUNK), 1)
    masks = [((row // (2 * m)) == (col // (2 * m))) & ((row // m) % 2 == 1) & ((col // m) % 2 == 0)
             for m in HG_LEVELS]
    masks.append(((row // HG_SUB) == (col // HG_SUB)) & (col <= row))
    return masks


def _hg_kernel(q_ref, f_ref, v_ref, gate_ref, mf_ref, mv_ref, lbl_ref, gn_ref,
               tril_ref, e2_ref, o_ref, state_ref, *, chunks_per_tile):
    t = pl.program_id(1)
    tril2 = tril_ref[...]
    e2 = e2_ref[...]
    lbl = lbl_ref[...]
    ex = jnp.exp(lbl - jnp.max(lbl, axis=0, keepdims=True))
    lb = ex[0:1, :] / jnp.sum(ex, axis=0, keepdims=True)
    gn = gn_ref[...]
    head_lanes = [slice(h * HG_DIM, (h + 1) * HG_DIM) for h in range(HG_HEADS)]

    @pl.when(t == 0)
    def _():
        for h, sl in enumerate(head_lanes):
            b, kk = _hg_decay(mf_ref[:, sl], lb[:, sl], tril2, HG_CHUNK - N_META)
            state_ref[h] = _hg_next_state(jnp.zeros((HG_DIM, HG_DIM), F32), b, kk, mv_ref[:, sl])

    masks = _hg_score_masks()

    def body(c, carry):
        rows = pl.ds(pl.multiple_of(c * HG_CHUNK, HG_CHUNK), HG_CHUNK)
        decay = [_hg_decay(f_ref[rows, sl], lb[:, sl], tril2, 0) for sl in head_lanes]
        parts, inter = [], []
        for h, sl in enumerate(head_lanes):
            b, kk = decay[h]
            q = q_ref[rows, sl].astype(F32)
            qf = q * _sigmoid(q) * HG_SCALE
            parts.append(_hg_score_parts(qf, kk, b, e2))
            state_t = state_ref[h]
            inter.append(lax.dot_general((qf * jnp.exp(b)).astype(BF16), state_t.astype(BF16),
                                         _NT, preferred_element_type=F32))
            state_ref[h] = _hg_next_state(state_t, b, kk, v_ref[rows, sl])
        for h, sl in enumerate(head_lanes):
            a = jnp.zeros((HG_CHUNK, HG_CHUNK), F32)
            for mask, part in zip(masks, parts[h]):
                a = jnp.where(mask, part, a)
            o = inter[h] + jnp.dot(a.astype(BF16), v_ref[rows, sl], preferred_element_type=F32)
            o = o * lax.rsqrt(jnp.mean(o * o, axis=-1, keepdims=True) + EPS) * gn[:, sl]
            gate = gate_ref[rows, sl].astype(F32)
            o_ref[rows, sl] = (o * (gate * _sigmoid(gate))).astype(BF16)
        return carry

    lax.fori_loop(0, chunks_per_tile, body, 0)


def _hgrn2(proj, f_proj, proj_meta, f_meta, lb_logits, hg_norm_g, tril2, e2, batch, seq, row_tile):
    assert seq % row_tile == 0 and row_tile % HG_CHUNK == 0
    proj3 = proj.reshape(batch, seq, IN_COLS)
    f3 = f_proj.reshape(batch, seq, HG_WIDTH)
    meta_blk = META_ROWS // HG_CHUNK - 1
    const = lambda b, t: (0, 0)
    return pl.pallas_call(
        functools.partial(_hg_kernel, chunks_per_tile=row_tile // HG_CHUNK),
        grid=(batch, seq // row_tile),
        in_specs=[
            pl.BlockSpec((None, row_tile, HG_WIDTH), lambda b, t: (b, t, T_HG_Q)),
            pl.BlockSpec((None, row_tile, HG_WIDTH), lambda b, t: (b, t, 0)),
            pl.BlockSpec((None, row_tile, HG_WIDTH), lambda b, t: (b, t, T_HG_I)),
            pl.BlockSpec((None, row_tile, HG_WIDTH), lambda b, t: (b, t, T_HG_GATE)),
            pl.BlockSpec((HG_CHUNK, HG_WIDTH), lambda b, t: (meta_blk, 0)),
            pl.BlockSpec((HG_CHUNK, HG_WIDTH), lambda b, t: (meta_blk, T_HG_I)),
            pl.BlockSpec(lb_logits.shape, const),
            pl.BlockSpec((1, HG_WIDTH), const),
            pl.BlockSpec(tril2.shape, const),
            pl.BlockSpec(e2.shape, const),
        ],
        out_specs=pl.BlockSpec((None, row_tile, HG_WIDTH), lambda b, t: (b, t, 0)),
        out_shape=jax.ShapeDtypeStruct((batch, seq, HG_WIDTH), BF16),
        scratch_shapes=[pltpu.VMEM((HG_HEADS, HG_DIM, HG_DIM), F32)],
        compiler_params=pltpu.CompilerParams(
            dimension_semantics=("parallel", "arbitrary"),
            vmem_limit_bytes=VMEM_LIMIT_BYTES),
        name="hgrn2",
    )(proj3, f3, proj3, proj3, f_meta, proj_meta, lb_logits, hg_norm_g, tril2, e2)


MERGE_SLAB = 256


def _final_kernel(ysb_ref, yhg_ref, gsb_ref, ghg_ref, x_ref, wsb_ref, whg_ref, wout_ref, fg_ref,
                  o_ref, wsb_s, whg_s, wout_s):
    @pl.when(pl.program_id(0) == 0)
    def _():
        wsb_s[...] = wsb_ref[...].astype(BF16)
        whg_s[...] = whg_ref[...].astype(BF16)
        wout_s[...] = wout_ref[...].astype(BF16)

    n_slab = o_ref.shape[0] // MERGE_SLAB
    slabs = [pl.ds(s * MERGE_SLAB, MERGE_SLAB) for s in range(n_slab)]
    a = [jnp.dot(ysb_ref[sl, :], wsb_s[...], preferred_element_type=F32) for sl in slabs]
    b = [jnp.dot(yhg_ref[sl, :], whg_s[...], preferred_element_type=F32) for sl in slabs]
    merged = [(_sigmoid(gsb_ref[sl, :].astype(F32)) * a[s]
               + _sigmoid(ghg_ref[sl, :].astype(F32)) * b[s]).astype(BF16)
              for s, sl in enumerate(slabs)]
    h = [x_ref[sl, :] + jnp.dot(merged[s], wout_s[...], preferred_element_type=F32)
         for s, sl in enumerate(slabs)]
    for s, sl in enumerate(slabs):
        ms = jnp.mean(h[s] * h[s], axis=-1, keepdims=True)
        o_ref[sl, :] = h[s] * lax.rsqrt(ms + EPS) * fg_ref[...]


def _final(y_sb, y_hg, proj, x2d, w_sb_out, w_hg_out, w_out, final_norm_g, row_tile):
    rows = x2d.shape[0]
    assert rows % row_tile == 0
    gate_blk = (4 * SB_WIDTH + 4 * HG_WIDTH) // D_MODEL
    const = lambda r: (0, 0)
    return pl.pallas_call(
        _final_kernel,
        grid=(rows // row_tile,),
        in_specs=[
            pl.BlockSpec((row_tile, SB_WIDTH), lambda r: (r, 0)),
            pl.BlockSpec((row_tile, HG_WIDTH), lambda r: (r, 0)),
            pl.BlockSpec((row_tile, D_MODEL), lambda r: (r, gate_blk)),
            pl.BlockSpec((row_tile, D_MODEL), lambda r: (r, gate_blk + 1)),
            pl.BlockSpec((row_tile, D_MODEL), lambda r: (r, 0)),
            pl.BlockSpec((SB_WIDTH, D_MODEL), const),
            pl.BlockSpec((HG_WIDTH, D_MODEL), const),
            pl.BlockSpec((D_MODEL, D_MODEL), const),
            pl.BlockSpec((1, D_MODEL), const),
        ],
        out_specs=pl.BlockSpec((row_tile, D_MODEL), lambda r: (r, 0)),
        out_shape=jax.ShapeDtypeStruct((rows, D_MODEL), F32),
        scratch_shapes=[pltpu.VMEM((SB_WIDTH, D_MODEL), BF16),
                        pltpu.VMEM((HG_WIDTH, D_MODEL), BF16),
                        pltpu.VMEM((D_MODEL, D_MODEL), BF16)],
        compiler_params=pltpu.CompilerParams(
            dimension_semantics=("arbitrary",),
            vmem_limit_bytes=VMEM_LIMIT_BYTES),
        name="merge_out",
    )(y_sb, y_hg, proj, proj, x2d, w_sb_out, w_hg_out, w_out, final_norm_g)


def _suffix_sum_weights():
    j = np.arange(2 * KEY_TILE)[:, None] % KEY_TILE
    s = np.arange(2 * KEY_TILE)[None, :]
    return jnp.asarray(np.where(s < KEY_TILE, j > s, True), BF16)


def _chunk_cumsum_weights():
    t = np.arange(HG_CHUNK)[:, None]
    s = np.arange(2 * HG_CHUNK)[None, :] % HG_CHUNK
    return jnp.asarray(s <= t, BF16)


def _diag_reduce_weights():
    j = np.arange(HG_SUB * HG_DIM)[:, None] // HG_DIM
    s = np.arange(HG_CHUNK)[None, :] % HG_SUB
    return jnp.asarray(j == s, BF16)


def kernel(x, meta, norm_g, w_in, w_sb_out, w_hg_out, w_out, hg_norm_g, hg_lb_logits, final_norm_g):
    batch, seq, d = x.shape
    assert d == D_MODEL and meta.shape == (N_META, D_MODEL)
    assert norm_g.shape[0] == 1 and w_in.shape == (1, D_MODEL, IN_COLS)
    assert seq % 1024 == 0

    x2d = x.reshape(batch * seq, D_MODEL)
    meta_blk = jnp.concatenate(
        [jnp.zeros((META_ROWS - N_META, D_MODEL), x.dtype), meta.astype(x.dtype)], axis=0)
    w_in2 = w_in.reshape(D_MODEL, IN_COLS)

    proj, f_proj = _inproj(x2d, norm_g, w_in2, 2048)
    proj_meta, f_meta = _inproj(meta_blk, norm_g, w_in2, META_ROWS)

    y_sb = _stick_breaking(proj, proj_meta, _suffix_sum_weights(), batch, seq)
    y_hg = _hgrn2(proj, f_proj, proj_meta, f_meta, hg_lb_logits, hg_norm_g,
                  _chunk_cumsum_weights(), _diag_reduce_weights(), batch, seq, 1024)

    out = _final(y_sb.reshape(batch * seq, SB_WIDTH), y_hg.reshape(batch * seq, HG_WIDTH), proj,
                 x2d, w_sb_out.reshape(SB_WIDTH, D_MODEL), w_hg_out.reshape(HG_WIDTH, D_MODEL),
                 w_out.reshape(D_MODEL, D_MODEL), final_norm_g.reshape(1, D_MODEL), 512)
    return out.reshape(batch, seq, D_MODEL)
```

```python
import functools

import jax
import jax.numpy as jnp
import numpy as np
from jax import lax
from jax.experimental import pallas as pl
from jax.experimental.pallas import tpu as pltpu

F32 = jnp.float32
BF16 = jnp.bfloat16

D_MODEL = 1024
N_META = 16
SB_HEADS = 8
SB_HEAD_DIM = 64
SB_WIDTH = SB_HEADS * SB_HEAD_DIM
SB_SCALE = SB_HEAD_DIM ** -0.5
HG_HEADS = 4
HG_DIM = 128
HG_WIDTH = HG_HEADS * HG_DIM
HG_SCALE = HG_DIM ** -0.5
IN_COLS = 4 * SB_WIDTH + 4 * HG_WIDTH + 2 * D_MODEL
EPS = 1e-6

LANES = 128
VMEM_LIMIT_BYTES = 56 * 1024 * 1024

COL_TILE = 512
T_SB_Q, T_SB_K, T_SB_V, T_SB_GATE, T_HG_Q, T_HG_F, T_HG_I, T_HG_GATE = range(8)
N_COL_TILES = IN_COLS // COL_TILE

KEY_TILE = 128
META_ROWS = 128
HG_CHUNK = 64
HG_SUB = 8
HG_LEVELS = (8, 16, 32)


_NT = (((1,), (1,)), ((), ()))
_TN = (((0,), (0,)), ((), ()))


def _sigmoid(x):
    return 1.0 / (1.0 + jnp.exp(-x))


def _split_bf16(x):
    hi = x.astype(BF16)
    lo = (x - hi.astype(F32)).astype(BF16)
    return hi, lo


def _inproj_kernel(x_ref, g_ref, w_ref, o_ref, f_ref, u_ref):
    n = pl.program_id(1)

    @pl.when(n == 0)
    def _():
        x = x_ref[...]
        ms = jnp.mean(x * x, axis=-1, keepdims=True)
        u_ref[...] = (x * lax.rsqrt(ms + EPS) * g_ref[...]).astype(BF16)

    scale = jnp.where(n == T_SB_Q, SB_SCALE, 1.0).astype(F32)
    p = jnp.dot(u_ref[...], (w_ref[...] * scale).astype(BF16), preferred_element_type=F32)
    o_ref[...] = p.astype(BF16)

    @pl.when(n == T_HG_F)
    def _():
        f_ref[...] = p


def _inproj(x2d, norm_g, w_in, row_tile):
    rows = x2d.shape[0]
    assert rows % row_tile == 0
    return pl.pallas_call(
        _inproj_kernel,
        grid=(rows // row_tile, N_COL_TILES),
        in_specs=[
            pl.BlockSpec((row_tile, D_MODEL), lambda r, n: (r, 0)),
            pl.BlockSpec((1, D_MODEL), lambda r, n: (0, 0)),
            pl.BlockSpec((D_MODEL, COL_TILE), lambda r, n: (0, n)),
        ],
        out_specs=[
            pl.BlockSpec((row_tile, COL_TILE), lambda r, n: (r, n)),
            pl.BlockSpec((row_tile, COL_TILE), lambda r, n: (r, 0)),
        ],
        out_shape=[
            jax.ShapeDtypeStruct((rows, IN_COLS), BF16),
            jax.ShapeDtypeStruct((rows, HG_WIDTH), F32),
        ],
        scratch_shapes=[pltpu.VMEM((row_tile, D_MODEL), BF16)],
        compiler_params=pltpu.CompilerParams(
            dimension_semantics=("parallel", "arbitrary"),
            vmem_limit_bytes=VMEM_LIMIT_BYTES),
        name="inproj",
    )(x2d, norm_g, w_in)


SB_PAIRS = SB_WIDTH // LANES
SB_UNDERFLOW = -104.0
SB_FIRST_SWEEP = 3
SB_QTILES = 2


def _sb_kernel(q_ref, k_ref, v_ref, g_ref, mk_ref, mv_ref, uw_ref, o_ref, acc_ref, carry_ref):
    j = pl.program_id(1)
    lane = lax.broadcasted_iota(jnp.int32, (KEY_TILE, LANES), 1)
    first = lane < SB_HEAD_DIM
    uw = uw_ref[...]
    pair_lanes = [slice(p * LANES, (p + 1) * LANES) for p in range(SB_PAIRS)]
    q_rows = [slice(t * KEY_TILE, (t + 1) * KEY_TILE) for t in range(SB_QTILES)]
    q2 = []
    for qr in q_rows:
        for sl in pair_lanes:
            q = q_ref[qr, sl]
            zero = jnp.zeros_like(q)
            q2.append(jnp.concatenate([jnp.where(first, q, zero), jnp.where(first, zero, q)], axis=0))

    acc_ref[...] = jnp.zeros_like(acc_ref)
    carry_ref[...] = jnp.zeros_like(carry_ref)

    def sweep(kt_ref, vt_ref, jobs):
        chains = [(t * SB_PAIRS + p, sl, rows, valids)
                  for t, (rows, valids) in enumerate(jobs) for p, sl in enumerate(pair_lanes)]
        zs = [lax.dot_general(q2[c], kt_ref[rows, sl], _NT, preferred_element_type=F32)
              for c, sl, rows, _ in chains]
        log_betas, sums = [], []
        for z, (_, _, _, valids) in zip(zs, chains):
            log_beta_n, hi_lo = [], []
            for n, valid in enumerate(valids):
                zn = z[:, n * KEY_TILE:(n + 1) * KEY_TILE]
                log_beta = jnp.minimum(zn, 0.0) - jnp.log(1.0 + jnp.exp(-jnp.abs(zn)))
                log_1mb = log_beta - zn
                if valid is not None:
                    log_1mb = jnp.where(valid, log_1mb, 0.0)
                hi, lo = _split_bf16(log_1mb)
                hi_lo.append(jnp.concatenate([hi, lo], axis=1))
                log_beta_n.append(log_beta)
            sums.append(jnp.dot(jnp.concatenate(hi_lo, axis=0), uw, preferred_element_type=F32))
            log_betas.append(log_beta_n)
        ws = []
        for idx, (c, _, _, valids) in enumerate(chains):
            carry = carry_ref[c]
            w_n = [None] * len(valids)
            for n in reversed(range(len(valids))):
                cs = sums[idx][n * 2 * KEY_TILE:(n + 1) * 2 * KEY_TILE]
                w = jnp.exp(log_betas[idx][n] + cs[:, :KEY_TILE] + carry)
                if valids[n] is not None:
                    w = jnp.where(valids[n], w, 0.0)
                w_n[n] = w.astype(BF16)
                carry = carry + cs[:, KEY_TILE:]
            carry_ref[c] = carry
            ws.append(jnp.concatenate(w_n, axis=1))
        for idx, (c, sl, rows, _) in enumerate(chains):
            acc_ref[c] += jnp.dot(ws[idx], vt_ref[rows, sl], preferred_element_type=F32)

    row2 = lax.broadcasted_iota(jnp.int32, (2 * KEY_TILE, KEY_TILE), 0) % KEY_TILE
    col2 = lax.broadcasted_iota(jnp.int32, (2 * KEY_TILE, KEY_TILE), 1)
    causal = col2 < row2

    def key_rows(tile, n_tiles):
        return pl.ds(pl.multiple_of(tile * KEY_TILE, KEY_TILE), n_tiles * KEY_TILE)

    i_first = j * SB_QTILES
    fused = j >= 1
    assert SB_FIRST_SWEEP - 1 <= SB_QTILES

    @pl.when(fused)
    def _():
        sweep(k_ref, v_ref, [(key_rows(i_first + t - (SB_FIRST_SWEEP - 1), SB_FIRST_SWEEP),
                              [None] * (SB_FIRST_SWEEP - 1) + [causal]) for t in range(SB_QTILES)])

    @pl.when(jnp.logical_not(fused))
    def _():
        sweep(k_ref, v_ref, [(key_rows(0, t + 1), [None] * t + [causal]) for t in range(SB_QTILES)])

    def live_rows():
        top = jnp.max(carry_ref[...].reshape(-1, 8, LANES), axis=0)
        return (jnp.max(top) > SB_UNDERFLOW).astype(jnp.int32)

    left = [jnp.where(fused, i_first + t - (SB_FIRST_SWEEP - 1), 0) for t in range(SB_QTILES)]

    def cond(c):
        it, live = c
        return jnp.logical_and(it < left[-1], live > 0)

    def body(c):
        it, _ = c
        jobs = []
        for t in range(SB_QTILES):
            has = it < left[t]
            tile = jnp.maximum(left[t] - 1 - it, 0)
            valid = None if t == SB_QTILES - 1 else jnp.logical_and(col2 >= 0, has)
            jobs.append((key_rows(tile, 1), [valid]))
        sweep(k_ref, v_ref, jobs)
        return it + 1, live_rows()

    _, live = lax.while_loop(cond, body, (jnp.int32(0), live_rows()))

    @pl.when(live > 0)
    def _():
        sweep(mk_ref, mv_ref, [(slice(None), [col2 >= META_ROWS - N_META])] * SB_QTILES)

    for t, qr in enumerate(q_rows):
        for p, sl in enumerate(pair_lanes):
            acc = acc_ref[t * SB_PAIRS + p]
            out = jnp.where(first, acc[:KEY_TILE], acc[KEY_TILE:])
            gate = g_ref[qr, sl].astype(F32)
            o_ref[qr, sl] = (out * (gate * _sigmoid(gate))).astype(BF16)


def _stick_breaking(proj, proj_meta, uw, batch, seq):
    assert SB_WIDTH == COL_TILE
    proj3 = proj.reshape(batch, seq, IN_COLS)
    q_blk = SB_QTILES * KEY_TILE
    assert seq % q_blk == 0
    return pl.pallas_call(
        _sb_kernel,
        grid=(batch, seq // q_blk),
        in_specs=[
            pl.BlockSpec((None, q_blk, SB_WIDTH), lambda b, i: (b, i, T_SB_Q)),
            pl.BlockSpec((None, seq, SB_WIDTH), lambda b, i: (b, 0, T_SB_K)),
            pl.BlockSpec((None, seq, SB_WIDTH), lambda b, i: (b, 0, T_SB_V)),
            pl.BlockSpec((None, q_blk, SB_WIDTH), lambda b, i: (b, i, T_SB_GATE)),
            pl.BlockSpec((META_ROWS, SB_WIDTH), lambda b, i: (0, T_SB_K)),
            pl.BlockSpec((META_ROWS, SB_WIDTH), lambda b, i: (0, T_SB_V)),
            pl.BlockSpec((2 * KEY_TILE, 2 * KEY_TILE), lambda b, i: (0, 0)),
        ],
        out_specs=pl.BlockSpec((None, q_blk, SB_WIDTH), lambda b, i: (b, i, 0)),
        out_shape=jax.ShapeDtypeStruct((batch, seq, SB_WIDTH), BF16),
        scratch_shapes=[pltpu.VMEM((SB_QTILES * SB_PAIRS, 2 * KEY_TILE, LANES), F32),
                        pltpu.VMEM((SB_QTILES * SB_PAIRS, 2 * KEY_TILE, LANES), F32)],
        compiler_params=pltpu.CompilerParams(
            dimension_semantics=("parallel", "arbitrary"),
            vmem_limit_bytes=VMEM_LIMIT_BYTES),
        name="stick_breaking",
    )(proj3, proj3, proj3, proj3, proj_meta, proj_meta, uw)


def _hg_decay(f_logit, lbv, tril2, pad_rows):
    f = lbv + (1.0 - lbv) * _sigmoid(f_logit)
    if pad_rows:
        row = lax.broadcasted_iota(jnp.int32, f.shape, 0)
        f = jnp.where(row >= pad_rows, f, 1.0)
    g_hi, g_lo = _split_bf16(jnp.log(f))
    b = jnp.dot(tril2, jnp.concatenate([g_hi, g_lo], axis=0), preferred_element_type=F32)
    return b, 1.0 - f


def _hg_next_state(state_t, b, kk, v):
    b_last = b[HG_CHUNK - 1:HG_CHUNK, :]
    k_dec = (kk * jnp.exp(b_last - b)).astype(BF16)
    return state_t * jnp.exp(b_last) + lax.dot_general(v, k_dec, _TN, preferred_element_type=F32)


def _hg_score_parts(qf, kk, b, e2):
    shape8 = (HG_CHUNK // HG_SUB, HG_SUB, HG_DIM)
    b8, q8, k8 = b.reshape(shape8), qf.reshape(shape8), kk.reshape(shape8)
    last = jnp.broadcast_to(b8[:, HG_SUB - 1:HG_SUB, :], shape8)
    parts = []
    for m in HG_LEVELS:
        per = m // HG_SUB
        idx = [(blk // (2 * per)) * 2 * per + per - 1 for blk in range(shape8[0])]
        ref = jnp.concatenate([last[i:i + 1] for i in idx], axis=0)
        e = jnp.exp(-jnp.abs(b8 - ref))
        qh = (q8 * e).reshape(HG_CHUNK, HG_DIM).astype(BF16)
        kh = (k8 * e).reshape(HG_CHUNK, HG_DIM).astype(BF16)
        parts.append(lax.dot_general(qh, kh, _NT, preferred_element_type=F32))
    cols = []
    for j in range(HG_SUB):
        e = jnp.exp(jnp.minimum(b8 - b8[:, j:j + 1, :], 0.0))
        cols.append((q8 * e * k8[:, j:j + 1, :]).reshape(HG_CHUNK, HG_DIM).astype(BF16))
    parts.append(jnp.dot(jnp.concatenate(cols, axis=1), e2, preferred_element_type=F32))
    return parts


def _hg_score_masks():
    row = lax.broadcasted_iota(jnp.int32, (HG_CHUNK, HG_CHUNK), 0)
    col = lax.broadcasted_iota(jnp.int32, (HG_CHUNK, HG_CHUNK), 1)
    masks = [((row // (2 * m)) == (col // (2 * m))) & ((row // m) % 2 == 1) & ((col // m) % 2 == 0)
             for m in HG_LEVELS]
    masks.append(((row // HG_SUB) == (col // HG_SUB)) & (col <= row))
    return masks


def _hg_kernel(q_ref, f_ref, v_ref, gate_ref, mf_ref, mv_ref, lbl_ref, gn_ref,
               tril_ref, e2_ref, o_ref, state_ref, *, chunks_per_tile):
    t = pl.program_id(1)
    tril2 = tril_ref[...]
    e2 = e2_ref[...]
    lbl = lbl_ref[...]
    ex = jnp.exp(lbl - jnp.max(lbl, axis=0, keepdims=True))
    lb = ex[0:1, :] / jnp.sum(ex, axis=0, keepdims=True)
    gn = gn_ref[...]
    head_lanes = [slice(h * HG_DIM, (h + 1) * HG_DIM) for h in range(HG_HEADS)]

    @pl.when(t == 0)
    def _():
        for h, sl in enumerate(head_lanes):
            b, kk = _hg_decay(mf_ref[:, sl], lb[:, sl], tril2, HG_CHUNK - N_META)
            state_ref[h] = _hg_next_state(jnp.zeros((HG_DIM, HG_DIM), F32), b, kk, mv_ref[:, sl])

    masks = _hg_score_masks()

    def body(c, carry):
        rows = pl.ds(pl.multiple_of(c * HG_CHUNK, HG_CHUNK), HG_CHUNK)
        decay = [_hg_decay(f_ref[rows, sl], lb[:, sl], tril2, 0) for sl in head_lanes]
        parts, inter = [], []
        for h, sl in enumerate(head_lanes):
            b, kk = decay[h]
            q = q_ref[rows, sl].astype(F32)
            qf = q * _sigmoid(q) * HG_SCALE
            parts.append(_hg_score_parts(qf, kk, b, e2))
            state_t = state_ref[h]
            inter.append(lax.dot_general((qf * jnp.exp(b)).astype(BF16), state_t.astype(BF16),
                                         _NT, preferred_element_type=F32))
            state_ref[h] = _hg_next_state(state_t, b, kk, v_ref[rows, sl])
        for h, sl in enumerate(head_lanes):
            a = jnp.zeros((HG_CHUNK, HG_CHUNK), F32)
            for mask, part in zip(masks, parts[h]):
                a = jnp.where(mask, part, a)
            o = inter[h] + jnp.dot(a.astype(BF16), v_ref[rows, sl], preferred_element_type=F32)
            o = o * lax.rsqrt(jnp.mean(o * o, axis=-1, keepdims=True) + EPS) * gn[:, sl]
            gate = gate_ref[rows, sl].astype(F32)
            o_ref[rows, sl] = (o * (gate * _sigmoid(gate))).astype(BF16)
        return carry

    lax.fori_loop(0, chunks_per_tile, body, 0)


def _hgrn2(proj, f_proj, proj_meta, f_meta, lb_logits, hg_norm_g, tril2, e2, batch, seq, row_tile):
    assert seq % row_tile == 0 and row_tile % HG_CHUNK == 0
    proj3 = proj.reshape(batch, seq, IN_COLS)
    f3 = f_proj.reshape(batch, seq, HG_WIDTH)
    meta_blk = META_ROWS // HG_CHUNK - 1
    const = lambda b, t: (0, 0)
    return pl.pallas_call(
        functools.partial(_hg_kernel, chunks_per_tile=row_tile // HG_CHUNK),
        grid=(batch, seq // row_tile),
        in_specs=[
            pl.BlockSpec((None, row_tile, HG_WIDTH), lambda b, t: (b, t, T_HG_Q)),
            pl.BlockSpec((None, row_tile, HG_WIDTH), lambda b, t: (b, t, 0)),
            pl.BlockSpec((None, row_tile, HG_WIDTH), lambda b, t: (b, t, T_HG_I)),
            pl.BlockSpec((None, row_tile, HG_WIDTH), lambda b, t: (b, t, T_HG_GATE)),
            pl.BlockSpec((HG_CHUNK, HG_WIDTH), lambda b, t: (meta_blk, 0)),
            pl.BlockSpec((HG_CHUNK, HG_WIDTH), lambda b, t: (meta_blk, T_HG_I)),
            pl.BlockSpec(lb_logits.shape, const),
            pl.BlockSpec((1, HG_WIDTH), const),
            pl.BlockSpec(tril2.shape, const),
            pl.BlockSpec(e2.shape, const),
        ],
        out_specs=pl.BlockSpec((None, row_tile, HG_WIDTH), lambda b, t: (b, t, 0)),
        out_shape=jax.ShapeDtypeStruct((batch, seq, HG_WIDTH), BF16),
        scratch_shapes=[pltpu.VMEM((HG_HEADS, HG_DIM, HG_DIM), F32)],
        compiler_params=pltpu.CompilerParams(
            dimension_semantics=("parallel", "arbitrary"),
            vmem_limit_bytes=VMEM_LIMIT_BYTES),
        name="hgrn2",
    )(proj3, f3, proj3, proj3, f_meta, proj_meta, lb_logits, hg_norm_g, tril2, e2)


MERGE_SLAB = 256


def _final_kernel(ysb_ref, yhg_ref, gsb_ref, ghg_ref, x_ref, wsb_ref, whg_ref, wout_ref, fg_ref,
                  o_ref, wsb_s, whg_s, wout_s):
    @pl.when(pl.program_id(0) == 0)
    def _():
        wsb_s[...] = wsb_ref[...].astype(BF16)
        whg_s[...] = whg_ref[...].astype(BF16)
        wout_s[...] = wout_ref[...].astype(BF16)

    n_slab = o_ref.shape[0] // MERGE_SLAB
    slabs = [pl.ds(s * MERGE_SLAB, MERGE_SLAB) for s in range(n_slab)]
    a = [jnp.dot(ysb_ref[sl, :], wsb_s[...], preferred_element_type=F32) for sl in slabs]
    b = [jnp.dot(yhg_ref[sl, :], whg_s[...], preferred_element_type=F32) for sl in slabs]
    merged = [(_sigmoid(gsb_ref[sl, :].astype(F32)) * a[s]
               + _sigmoid(ghg_ref[sl, :].astype(F32)) * b[s]).astype(BF16)
              for s, sl in enumerate(slabs)]
    h = [x_ref[sl, :] + jnp.dot(merged[s], wout_s[...], preferred_element_type=F32)
         for s, sl in enumerate(slabs)]
    for s, sl in enumerate(slabs):
        ms = jnp.mean(h[s] * h[s], axis=-1, keepdims=True)
        o_ref[sl, :] = h[s] * lax.rsqrt(ms + EPS) * fg_ref[...]


def _final(y_sb, y_hg, proj, x2d, w_sb_out, w_hg_out, w_out, final_norm_g, row_tile):
    rows = x2d.shape[0]
    assert rows % row_tile == 0
    gate_blk = (4 * SB_WIDTH + 4 * HG_WIDTH) // D_MODEL
    const = lambda r: (0, 0)
    return pl.pallas_call(
        _final_kernel,
        grid=(rows // row_tile,),
        in_specs=[
            pl.BlockSpec((row_tile, SB_WIDTH), lambda r: (r, 0)),
            pl.BlockSpec((row_tile, HG_WIDTH), lambda r: (r, 0)),
            pl.BlockSpec((row_tile, D_MODEL), lambda r: (r, gate_blk)),
            pl.BlockSpec((row_tile, D_MODEL), lambda r: (r, gate_blk + 1)),
            pl.BlockSpec((row_tile, D_MODEL), lambda r: (r, 0)),
            pl.BlockSpec((SB_WIDTH, D_MODEL), const),
            pl.BlockSpec((HG_WIDTH, D_MODEL), const),
            pl.BlockSpec((D_MODEL, D_MODEL), const),
            pl.BlockSpec((1, D_MODEL), const),
        ],
        out_specs=pl.BlockSpec((row_tile, D_MODEL), lambda r: (r, 0)),
        out_shape=jax.ShapeDtypeStruct((rows, D_MODEL), F32),
        scratch_shapes=[pltpu.VMEM((SB_WIDTH, D_MODEL), BF16),
                        pltpu.VMEM((HG_WIDTH, D_MODEL), BF16),
                        pltpu.VMEM((D_MODEL, D_MODEL), BF16)],
        compiler_params=pltpu.CompilerParams(
            dimension_semantics=("arbitrary",),
            vmem_limit_bytes=VMEM_LIMIT_BYTES),
        name="merge_out",
    )(y_sb, y_hg, proj, proj, x2d, w_sb_out, w_hg_out, w_out, final_norm_g)


def _suffix_sum_weights():
    j = np.arange(2 * KEY_TILE)[:, None] % KEY_TILE
    s = np.arange(2 * KEY_TILE)[None, :]
    return jnp.asarray(np.where(s < KEY_TILE, j > s, True), BF16)


def _chunk_cumsum_weights():
    t = np.arange(HG_CHUNK)[:, None]
    s = np.arange(2 * HG_CHUNK)[None, :] % HG_CHUNK
    return jnp.asarray(s <= t, BF16)


def _diag_reduce_weights():
    j = np.arange(HG_SUB * HG_DIM)[:, None] // HG_DIM
    s = np.arange(HG_CHUNK)[None, :] % HG_SUB
    return jnp.asarray(j == s, BF16)


def kernel(x, meta, norm_g, w_in, w_sb_out, w_hg_out, w_out, hg_norm_g, hg_lb_logits, final_norm_g):
    batch, seq, d = x.shape
    assert d == D_MODEL and meta.shape == (N_META, D_MODEL)
    assert norm_g.shape[0] == 1 and w_in.shape == (1, D_MODEL, IN_COLS)
    assert seq % 1024 == 0

    x2d = x.reshape(batch * seq, D_MODEL)
    meta_blk = jnp.concatenate(
        [jnp.zeros((META_ROWS - N_META, D_MODEL), x.dtype), meta.astype(x.dtype)], axis=0)
    w_in2 = w_in.reshape(D_MODEL, IN_COLS)

    proj, f_proj = _inproj(x2d, norm_g, w_in2, 2048)
    proj_meta, f_meta = _inproj(meta_blk, norm_g, w_in2, META_ROWS)

    y_sb = _stick_breaking(proj, proj_meta, _suffix_sum_weights(), batch, seq)
    y_hg = _hgrn2(proj, f_proj, proj_meta, f_meta, hg_lb_logits, hg_norm_g,
                  _chunk_cumsum_weights(), _diag_reduce_weights(), batch, seq, 1024)

    out = _final(y_sb.reshape(batch * seq, SB_WIDTH), y_hg.reshape(batch * seq, HG_WIDTH), proj,
                 x2d, w_sb_out.reshape(SB_WIDTH, D_MODEL), w_hg_out.reshape(HG_WIDTH, D_MODEL),
                 w_out.reshape(D_MODEL, D_MODEL), final_norm_g.reshape(1, D_MODEL), 512)
    return out.reshape(batch, seq, D_MODEL)
```

```python
import functools

import jax
import jax.numpy as jnp
import numpy as np
from jax import lax
from jax.experimental import pallas as pl
from jax.experimental.pallas import tpu as pltpu

F32 = jnp.float32
BF16 = jnp.bfloat16

D_MODEL = 1024
N_META = 16
SB_HEADS = 8
SB_HEAD_DIM = 64
SB_WIDTH = SB_HEADS * SB_HEAD_DIM
SB_SCALE = SB_HEAD_DIM ** -0.5
HG_HEADS = 4
HG_DIM = 128
HG_WIDTH = HG_HEADS * HG_DIM
HG_SCALE = HG_DIM ** -0.5
IN_COLS = 4 * SB_WIDTH + 4 * HG_WIDTH + 2 * D_MODEL
EPS = 1e-6

LANES = 128
VMEM_LIMIT_BYTES = 56 * 1024 * 1024

COL_TILE = 512
T_SB_Q, T_SB_K, T_SB_V, T_SB_GATE, T_HG_Q, T_HG_F, T_HG_I, T_HG_GATE = range(8)
N_COL_TILES = IN_COLS // COL_TILE

KEY_TILE = 128
META_ROWS = 128
HG_CHUNK = 64
HG_LEVELS = (1, 2, 4, 8, 16, 32)
HG_UNROLL = 2


_NT = (((1,), (1,)), ((), ()))
_TN = (((0,), (0,)), ((), ()))


def _sigmoid(x):
    return 1.0 / (1.0 + jnp.exp(-x))


def _split_bf16(x):
    hi = x.astype(BF16)
    lo = (x - hi.astype(F32)).astype(BF16)
    return hi, lo


def _inproj_kernel(x_ref, g_ref, w_ref, o_ref, f_ref, u_ref):
    n = pl.program_id(1)

    @pl.when(n == 0)
    def _():
        x = x_ref[...]
        ms = jnp.mean(x * x, axis=-1, keepdims=True)
        u_ref[...] = (x * lax.rsqrt(ms + EPS) * g_ref[...]).astype(BF16)

    scale = jnp.where(n == T_SB_Q, SB_SCALE, 1.0).astype(F32)
    p = jnp.dot(u_ref[...], (w_ref[...] * scale).astype(BF16), preferred_element_type=F32)
    o_ref[...] = p.astype(BF16)

    @pl.when(n == T_HG_F)
    def _():
        f_ref[...] = p


def _inproj(x2d, norm_g, w_in, row_tile):
    rows = x2d.shape[0]
    assert rows % row_tile == 0
    return pl.pallas_call(
        _inproj_kernel,
        grid=(rows // row_tile, N_COL_TILES),
        in_specs=[
            pl.BlockSpec((row_tile, D_MODEL), lambda r, n: (r, 0)),
            pl.BlockSpec((1, D_MODEL), lambda r, n: (0, 0)),
            pl.BlockSpec((D_MODEL, COL_TILE), lambda r, n: (0, n)),
        ],
        out_specs=[
            pl.BlockSpec((row_tile, COL_TILE), lambda r, n: (r, n)),
            pl.BlockSpec((row_tile, COL_TILE), lambda r, n: (r, 0)),
        ],
        out_shape=[
            jax.ShapeDtypeStruct((rows, IN_COLS), BF16),
            jax.ShapeDtypeStruct((rows, HG_WIDTH), F32),
        ],
        scratch_shapes=[pltpu.VMEM((row_tile, D_MODEL), BF16)],
        compiler_params=pltpu.CompilerParams(
            dimension_semantics=("parallel", "arbitrary"),
            vmem_limit_bytes=VMEM_LIMIT_BYTES),
        name="inproj",
    )(x2d, norm_g, w_in)


SB_PAIRS = SB_WIDTH // LANES
SB_UNDERFLOW = -104.0
SB_FIRST_SWEEP = 3
SB_QTILES = 2


def _sb_kernel(q_ref, k_ref, v_ref, g_ref, mk_ref, mv_ref, uw_ref, o_ref, acc_ref, carry_ref):
    j = pl.program_id(1)
    lane = lax.broadcasted_iota(jnp.int32, (KEY_TILE, LANES), 1)
    first = lane < SB_HEAD_DIM
    uw = uw_ref[...]
    pair_lanes = [slice(p * LANES, (p + 1) * LANES) for p in range(SB_PAIRS)]
    q_rows = [slice(t * KEY_TILE, (t + 1) * KEY_TILE) for t in range(SB_QTILES)]
    q2 = []
    for qr in q_rows:
        for sl in pair_lanes:
            q = q_ref[qr, sl]
            zero = jnp.zeros_like(q)
            q2.append(jnp.concatenate([jnp.where(first, q, zero), jnp.where(first, zero, q)], axis=0))

    acc_ref[...] = jnp.zeros_like(acc_ref)
    carry_ref[...] = jnp.zeros_like(carry_ref)

    def sweep(kt_ref, vt_ref, jobs):
        chains = [(t * SB_PAIRS + p, sl, rows, valids)
                  for t, (rows, valids) in enumerate(jobs) for p, sl in enumerate(pair_lanes)]
        zs = [lax.dot_general(q2[c], kt_ref[rows, sl], _NT, preferred_element_type=F32)
              for c, sl, rows, _ in chains]
        log_betas, sums = [], []
        for z, (_, _, _, valids) in zip(zs, chains):
            log_beta_n, hi_lo = [], []
            for n, valid in enumerate(valids):
                zn = z[:, n * KEY_TILE:(n + 1) * KEY_TILE]
                log_beta = jnp.minimum(zn, 0.0) - jnp.log(1.0 + jnp.exp(-jnp.abs(zn)))
                log_1mb = log_beta - zn
                if valid is not None:
                    log_1mb = jnp.where(valid, log_1mb, 0.0)
                hi, lo = _split_bf16(log_1mb)
                hi_lo.append(jnp.concatenate([hi, lo], axis=1))
                log_beta_n.append(log_beta)
            sums.append(jnp.dot(jnp.concatenate(hi_lo, axis=0), uw, preferred_element_type=F32))
            log_betas.append(log_beta_n)
        ws = []
        for idx, (c, _, _, valids) in enumerate(chains):
            carry = carry_ref[c]
            w_n = [None] * len(valids)
            for n in reversed(range(len(valids))):
                cs = sums[idx][n * 2 * KEY_TILE:(n + 1) * 2 * KEY_TILE]
                w = jnp.exp(log_betas[idx][n] + cs[:, :KEY_TILE] + carry)
                if valids[n] is not None:
                    w = jnp.where(valids[n], w, 0.0)
                w_n[n] = w.astype(BF16)
                carry = carry + cs[:, KEY_TILE:]
            carry_ref[c] = carry
            ws.append(jnp.concatenate(w_n, axis=1))
        for idx, (c, sl, rows, _) in enumerate(chains):
            acc_ref[c] += jnp.dot(ws[idx], vt_ref[rows, sl], preferred_element_type=F32)

    row2 = lax.broadcasted_iota(jnp.int32, (2 * KEY_TILE, KEY_TILE), 0) % KEY_TILE
    col2 = lax.broadcasted_iota(jnp.int32, (2 * KEY_TILE, KEY_TILE), 1)
    causal = col2 < row2

    def key_rows(tile, n_tiles):
        return pl.ds(pl.multiple_of(tile * KEY_TILE, KEY_TILE), n_tiles * KEY_TILE)

    i_first = j * SB_QTILES
    fused = j >= 1
    assert SB_FIRST_SWEEP - 1 <= SB_QTILES

    @pl.when(fused)
    def _():
        sweep(k_ref, v_ref, [(key_rows(i_first + t - (SB_FIRST_SWEEP - 1), SB_FIRST_SWEEP),
                              [None] * (SB_FIRST_SWEEP - 1) + [causal]) for t in range(SB_QTILES)])

    @pl.when(jnp.logical_not(fused))
    def _():
        sweep(k_ref, v_ref, [(key_rows(0, t + 1), [None] * t + [causal]) for t in range(SB_QTILES)])

    def live_rows():
        top = jnp.max(carry_ref[...].reshape(-1, 8, LANES), axis=0)
        return (jnp.max(top) > SB_UNDERFLOW).astype(jnp.int32)

    left = [jnp.where(fused, i_first + t - (SB_FIRST_SWEEP - 1), 0) for t in range(SB_QTILES)]

    def cond(c):
        it, live = c
        return jnp.logical_and(it < left[-1], live > 0)

    def body(c):
        it, _ = c
        jobs = []
        for t in range(SB_QTILES):
            has = it < left[t]
            tile = jnp.maximum(left[t] - 1 - it, 0)
            valid = None if t == SB_QTILES - 1 else jnp.logical_and(col2 >= 0, has)
            jobs.append((key_rows(tile, 1), [valid]))
        sweep(k_ref, v_ref, jobs)
        return it + 1, live_rows()

    _, live = lax.while_loop(cond, body, (jnp.int32(0), live_rows()))

    @pl.when(live > 0)
    def _():
        sweep(mk_ref, mv_ref, [(slice(None), [col2 >= META_ROWS - N_META])] * SB_QTILES)

    for t, qr in enumerate(q_rows):
        for p, sl in enumerate(pair_lanes):
            acc = acc_ref[t * SB_PAIRS + p]
            out = jnp.where(first, acc[:KEY_TILE], acc[KEY_TILE:])
            gate = g_ref[qr, sl].astype(F32)
            o_ref[qr, sl] = (out * (gate * _sigmoid(gate))).astype(BF16)


def _stick_breaking(proj, proj_meta, uw, batch, seq):
    assert SB_WIDTH == COL_TILE
    proj3 = proj.reshape(batch, seq, IN_COLS)
    q_blk = SB_QTILES * KEY_TILE
    assert seq % q_blk == 0
    return pl.pallas_call(
        _sb_kernel,
        grid=(batch, seq // q_blk),
        in_specs=[
            pl.BlockSpec((None, q_blk, SB_WIDTH), lambda b, i: (b, i, T_SB_Q)),
            pl.BlockSpec((None, seq, SB_WIDTH), lambda b, i: (b, 0, T_SB_K)),
            pl.BlockSpec((None, seq, SB_WIDTH), lambda b, i: (b, 0, T_SB_V)),
            pl.BlockSpec((None, q_blk, SB_WIDTH), lambda b, i: (b, i, T_SB_GATE)),
            pl.BlockSpec((META_ROWS, SB_WIDTH), lambda b, i: (0, T_SB_K)),
            pl.BlockSpec((META_ROWS, SB_WIDTH), lambda b, i: (0, T_SB_V)),
            pl.BlockSpec((2 * KEY_TILE, 2 * KEY_TILE), lambda b, i: (0, 0)),
        ],
        out_specs=pl.BlockSpec((None, q_blk, SB_WIDTH), lambda b, i: (b, i, 0)),
        out_shape=jax.ShapeDtypeStruct((batch, seq, SB_WIDTH), BF16),
        scratch_shapes=[pltpu.VMEM((SB_QTILES * SB_PAIRS, 2 * KEY_TILE, LANES), F32),
                        pltpu.VMEM((SB_QTILES * SB_PAIRS, 2 * KEY_TILE, LANES), F32)],
        compiler_params=pltpu.CompilerParams(
            dimension_semantics=("parallel", "arbitrary"),
            vmem_limit_bytes=VMEM_LIMIT_BYTES),
        name="stick_breaking",
    )(proj3, proj3, proj3, proj3, proj_meta, proj_meta, uw)


def _hg_decay(f_logit, lbv, sum_w, pad_rows):
    f = lbv + (1.0 - lbv) * _sigmoid(f_logit)
    if pad_rows:
        row = lax.broadcasted_iota(jnp.int32, f.shape, 0)
        f = jnp.where(row >= pad_rows, f, 1.0)
    g_hi, g_lo = _split_bf16(jnp.log(f))
    sums = jnp.dot(sum_w, jnp.concatenate([g_hi, g_lo], axis=0), preferred_element_type=F32)
    return sums, 1.0 - f


def _hg_next_state(state_t, b, kk, v):
    b_last = b[HG_CHUNK - 1:HG_CHUNK, :]
    k_dec = (kk * jnp.exp(b_last - b)).astype(BF16)
    return state_t * jnp.exp(b_last) + lax.dot_general(v, k_dec, _TN, preferred_element_type=F32)


def _hg_scores(qf, kk, sums, masks):
    a = jnp.where(masks[0], lax.dot_general(qf.astype(BF16), kk.astype(BF16), _NT,
                                            preferred_element_type=F32), 0.0)
    for lvl in range(len(HG_LEVELS)):
        e = jnp.exp(sums[(1 + lvl) * HG_CHUNK:(2 + lvl) * HG_CHUNK])
        part = lax.dot_general((qf * e).astype(BF16), (kk * e).astype(BF16), _NT,
                               preferred_element_type=F32)
        a = jnp.where(masks[1 + lvl], part, a)
    return a


def _hg_score_masks():
    row = lax.broadcasted_iota(jnp.int32, (HG_CHUNK, HG_CHUNK), 0)
    col = lax.broadcasted_iota(jnp.int32, (HG_CHUNK, HG_CHUNK), 1)
    return [row == col] + [
        ((row // (2 * m)) == (col // (2 * m))) & ((row // m) % 2 == 1) & ((col // m) % 2 == 0)
        for m in HG_LEVELS]


def _hg_kernel(q_ref, f_ref, v_ref, gate_ref, mf_ref, mv_ref, lbl_ref, gn_ref,
               sumw_ref, o_ref, state_ref, *, chunks_per_tile):
    t = pl.program_id(1)
    sum_w = sumw_ref[...]
    lbl = lbl_ref[...]
    ex = jnp.exp(lbl - jnp.max(lbl, axis=0, keepdims=True))
    lb = ex[0:1, :] / jnp.sum(ex, axis=0, keepdims=True)
    gn = gn_ref[...]
    head_lanes = [slice(h * HG_DIM, (h + 1) * HG_DIM) for h in range(HG_HEADS)]

    @pl.when(t == 0)
    def _():
        for h, sl in enumerate(head_lanes):
            b, kk = _hg_decay(mf_ref[:, sl], lb[:, sl], sum_w[:HG_CHUNK], HG_CHUNK - N_META)
            state_ref[h] = _hg_next_state(jnp.zeros((HG_DIM, HG_DIM), F32), b, kk, mv_ref[:, sl])

    masks = _hg_score_masks()

    def body(c, carry):
        base = c * (HG_UNROLL * HG_CHUNK)
        rows = [pl.ds(pl.multiple_of(base + u * HG_CHUNK, HG_CHUNK), HG_CHUNK)
                for u in range(HG_UNROLL)]
        decay = [[_hg_decay(f_ref[r, sl], lb[:, sl], sum_w, 0) for sl in head_lanes] for r in rows]
        states = [state_ref[h] for h in range(HG_HEADS)]
        scores, inter = [], []
        for u, r in enumerate(rows):
            for h, sl in enumerate(head_lanes):
                sums, kk = decay[u][h]
                b = sums[:HG_CHUNK]
                q = q_ref[r, sl].astype(F32)
                qf = q * _sigmoid(q) * HG_SCALE
                scores.append(_hg_scores(qf, kk, sums, masks))
                inter.append(lax.dot_general((qf * jnp.exp(b)).astype(BF16),
                                             states[h].astype(BF16), _NT,
                                             preferred_element_type=F32))
                states[h] = _hg_next_state(states[h], b, kk, v_ref[r, sl])
        for h in range(HG_HEADS):
            state_ref[h] = states[h]
        for u, r in enumerate(rows):
            for h, sl in enumerate(head_lanes):
                i = u * HG_HEADS + h
                o = inter[i] + jnp.dot(scores[i].astype(BF16), v_ref[r, sl],
                                       preferred_element_type=F32)
                o = o * lax.rsqrt(jnp.mean(o * o, axis=-1, keepdims=True) + EPS) * gn[:, sl]
                gate = gate_ref[r, sl].astype(F32)
                o_ref[r, sl] = (o * (gate * _sigmoid(gate))).astype(BF16)
        return carry

    lax.fori_loop(0, chunks_per_tile // HG_UNROLL, body, 0)


def _hgrn2(proj, f_proj, proj_meta, f_meta, lb_logits, hg_norm_g, sum_w, batch, seq, row_tile):
    assert seq % row_tile == 0 and row_tile % HG_CHUNK == 0
    proj3 = proj.reshape(batch, seq, IN_COLS)
    f3 = f_proj.reshape(batch, seq, HG_WIDTH)
    meta_blk = META_ROWS // HG_CHUNK - 1
    const = lambda b, t: (0, 0)
    return pl.pallas_call(
        functools.partial(_hg_kernel, chunks_per_tile=row_tile // HG_CHUNK),
        grid=(batch, seq // row_tile),
        in_specs=[
            pl.BlockSpec((None, row_tile, HG_WIDTH), lambda b, t: (b, t, T_HG_Q)),
            pl.BlockSpec((None, row_tile, HG_WIDTH), lambda b, t: (b, t, 0)),
            pl.BlockSpec((None, row_tile, HG_WIDTH), lambda b, t: (b, t, T_HG_I)),
            pl.BlockSpec((None, row_tile, HG_WIDTH), lambda b, t: (b, t, T_HG_GATE)),
            pl.BlockSpec((HG_CHUNK, HG_WIDTH), lambda b, t: (meta_blk, 0)),
            pl.BlockSpec((HG_CHUNK, HG_WIDTH), lambda b, t: (meta_blk, T_HG_I)),
            pl.BlockSpec(lb_logits.shape, const),
            pl.BlockSpec((1, HG_WIDTH), const),
            pl.BlockSpec(sum_w.shape, const),
        ],
        out_specs=pl.BlockSpec((None, row_tile, HG_WIDTH), lambda b, t: (b, t, 0)),
        out_shape=jax.ShapeDtypeStruct((batch, seq, HG_WIDTH), BF16),
        scratch_shapes=[pltpu.VMEM((HG_HEADS, HG_DIM, HG_DIM), F32)],
        compiler_params=pltpu.CompilerParams(
            dimension_semantics=("parallel", "arbitrary"),
            vmem_limit_bytes=VMEM_LIMIT_BYTES),
        name="hgrn2",
    )(proj3, f3, proj3, proj3, f_meta, proj_meta, lb_logits, hg_norm_g, sum_w)


MERGE_SLAB = 256


def _final_kernel(ysb_ref, yhg_ref, gsb_ref, ghg_ref, x_ref, wsb_ref, whg_ref, wout_ref, fg_ref,
                  o_ref, wsb_s, whg_s, wout_s):
    @pl.when(pl.program_id(0) == 0)
    def _():
        wsb_s[...] = wsb_ref[...].astype(BF16)
        whg_s[...] = whg_ref[...].astype(BF16)
        wout_s[...] = wout_ref[...].astype(BF16)

    n_slab = o_ref.shape[0] // MERGE_SLAB
    slabs = [pl.ds(s * MERGE_SLAB, MERGE_SLAB) for s in range(n_slab)]
    a = [jnp.dot(ysb_ref[sl, :], wsb_s[...], preferred_element_type=F32) for sl in slabs]
    b = [jnp.dot(yhg_ref[sl, :], whg_s[...], preferred_element_type=F32) for sl in slabs]
    merged = [(_sigmoid(gsb_ref[sl, :].astype(F32)) * a[s]
               + _sigmoid(ghg_ref[sl, :].astype(F32)) * b[s]).astype(BF16)
              for s, sl in enumerate(slabs)]
    h = [x_ref[sl, :] + jnp.dot(merged[s], wout_s[...], preferred_element_type=F32)
         for s, sl in enumerate(slabs)]
    for s, sl in enumerate(slabs):
        ms = jnp.mean(h[s] * h[s], axis=-1, keepdims=True)
        o_ref[sl, :] = h[s] * lax.rsqrt(ms + EPS) * fg_ref[...]


def _final(y_sb, y_hg, proj, x2d, w_sb_out, w_hg_out, w_out, final_norm_g, row_tile):
    rows = x2d.shape[0]
    assert rows % row_tile == 0
    gate_blk = (4 * SB_WIDTH + 4 * HG_WIDTH) // D_MODEL
    const = lambda r: (0, 0)
    return pl.pallas_call(
        _final_kernel,
        grid=(rows // row_tile,),
        in_specs=[
            pl.BlockSpec((row_tile, SB_WIDTH), lambda r: (r, 0)),
            pl.BlockSpec((row_tile, HG_WIDTH), lambda r: (r, 0)),
            pl.BlockSpec((row_tile, D_MODEL), lambda r: (r, gate_blk)),
            pl.BlockSpec((row_tile, D_MODEL), lambda r: (r, gate_blk + 1)),
            pl.BlockSpec((row_tile, D_MODEL), lambda r: (r, 0)),
            pl.BlockSpec((SB_WIDTH, D_MODEL), const),
            pl.BlockSpec((HG_WIDTH, D_MODEL), const),
            pl.BlockSpec((D_MODEL, D_MODEL), const),
            pl.BlockSpec((1, D_MODEL), const),
        ],
        out_specs=pl.BlockSpec((row_tile, D_MODEL), lambda r: (r, 0)),
        out_shape=jax.ShapeDtypeStruct((rows, D_MODEL), F32),
        scratch_shapes=[pltpu.VMEM((SB_WIDTH, D_MODEL), BF16),
                        pltpu.VMEM((HG_WIDTH, D_MODEL), BF16),
                        pltpu.VMEM((D_MODEL, D_MODEL), BF16)],
        compiler_params=pltpu.CompilerParams(
            dimension_semantics=("arbitrary",),
            vmem_limit_bytes=VMEM_LIMIT_BYTES),
        name="merge_out",
    )(y_sb, y_hg, proj, proj, x2d, w_sb_out, w_hg_out, w_out, final_norm_g)


def _suffix_sum_weights():
    j = np.arange(2 * KEY_TILE)[:, None] % KEY_TILE
    s = np.arange(2 * KEY_TILE)[None, :]
    return jnp.asarray(np.where(s < KEY_TILE, j > s, True), BF16)


def _hg_sum_weights():
    r = np.arange(HG_CHUNK)[:, None]
    c = np.arange(HG_CHUNK)[None, :]
    groups = [c <= r]
    for m in HG_LEVELS:
        ref = (r // (2 * m)) * 2 * m + m - 1
        groups.append(np.where(r > ref, (c > ref) & (c <= r), (c > r) & (c <= ref)))
    w = np.concatenate(groups, axis=0)
    return jnp.asarray(np.concatenate([w, w], axis=1), BF16)


def kernel(x, meta, norm_g, w_in, w_sb_out, w_hg_out, w_out, hg_norm_g, hg_lb_logits, final_norm_g):
    batch, seq, d = x.shape
    assert d == D_MODEL and meta.shape == (N_META, D_MODEL)
    assert norm_g.shape[0] == 1 and w_in.shape == (1, D_MODEL, IN_COLS)
    assert seq % 1024 == 0

    x2d = x.reshape(batch * seq, D_MODEL)
    meta_blk = jnp.concatenate(
        [jnp.zeros((META_ROWS - N_META, D_MODEL), x.dtype), meta.astype(x.dtype)], axis=0)
    w_in2 = w_in.reshape(D_MODEL, IN_COLS)

    proj, f_proj = _inproj(x2d, norm_g, w_in2, 2048)
    proj_meta, f_meta = _inproj(meta_blk, norm_g, w_in2, META_ROWS)

    y_sb = _stick_breaking(proj, proj_meta, _suffix_sum_weights(), batch, seq)
    y_hg = _hgrn2(proj, f_proj, proj_meta, f_meta, hg_lb_logits, hg_norm_g,
                  _hg_sum_weights(), batch, seq, 1024)

    out = _final(y_sb.reshape(batch * seq, SB_WIDTH), y_hg.reshape(batch * seq, HG_WIDTH), proj,
                 x2d, w_sb_out.reshape(SB_WIDTH, D_MODEL), w_hg_out.reshape(HG_WIDTH, D_MODEL),
                 w_out.reshape(D_MODEL, D_MODEL), final_norm_g.reshape(1, D_MODEL), 512)
    return out.reshape(batch, seq, D_MODEL)
```

```python
import functools

import jax
import jax.numpy as jnp
import numpy as np
from jax import lax
from jax.experimental import pallas as pl
from jax.experimental.pallas import tpu as pltpu

F32 = jnp.float32
BF16 = jnp.bfloat16

D_MODEL = 1024
N_META = 16
SB_HEADS = 8
SB_HEAD_DIM = 64
SB_WIDTH = SB_HEADS * SB_HEAD_DIM
SB_SCALE = SB_HEAD_DIM ** -0.5
HG_HEADS = 4
HG_DIM = 128
HG_WIDTH = HG_HEADS * HG_DIM
HG_SCALE = HG_DIM ** -0.5
IN_COLS = 4 * SB_WIDTH + 4 * HG_WIDTH + 2 * D_MODEL
EPS = 1e-6

LANES = 128
VMEM_LIMIT_BYTES = 56 * 1024 * 1024

COL_TILE = 512
T_SB_Q, T_SB_K, T_SB_V, T_SB_GATE, T_HG_Q, T_HG_F, T_HG_I, T_HG_GATE = range(8)
N_COL_TILES = IN_COLS // COL_TILE

KEY_TILE = 128
META_ROWS = 128
HG_CHUNK = 64
HG_LEVELS = (1, 2, 4, 8, 16, 32)
HG_UNROLL = 2


_NT = (((1,), (1,)), ((), ()))
_TN = (((0,), (0,)), ((), ()))


def _sigmoid(x):
    return 1.0 / (1.0 + jnp.exp(-x))


def _split_bf16(x):
    hi = x.astype(BF16)
    lo = (x - hi.astype(F32)).astype(BF16)
    return hi, lo


INPROJ_COLS = 2 * COL_TILE
assert (T_SB_Q * COL_TILE) % INPROJ_COLS == 0


def _inproj_kernel(x_ref, g_ref, w_ref, o_ref, f_ref, u_ref):
    n = pl.program_id(1)

    @pl.when(n == 0)
    def _():
        x = x_ref[...]
        ms = jnp.mean(x * x, axis=-1, keepdims=True)
        u_ref[...] = (x * lax.rsqrt(ms + EPS) * g_ref[...]).astype(BF16)

    lane = lax.broadcasted_iota(jnp.int32, (1, INPROJ_COLS), 1)
    is_q = jnp.logical_and(n == (T_SB_Q * COL_TILE) // INPROJ_COLS, lane < SB_WIDTH)
    scale = jnp.where(is_q, SB_SCALE, 1.0).astype(F32)
    p = jnp.dot(u_ref[...], (w_ref[...] * scale).astype(BF16), preferred_element_type=F32)
    o_ref[...] = p.astype(BF16)

    f_col = (T_HG_F * COL_TILE) % INPROJ_COLS

    @pl.when(n == (T_HG_F * COL_TILE) // INPROJ_COLS)
    def _():
        f_ref[...] = p[:, f_col:f_col + HG_WIDTH]


def _inproj(x2d, norm_g, w_in, row_tile):
    rows = x2d.shape[0]
    assert rows % row_tile == 0
    return pl.pallas_call(
        _inproj_kernel,
        grid=(rows // row_tile, IN_COLS // INPROJ_COLS),
        in_specs=[
            pl.BlockSpec((row_tile, D_MODEL), lambda r, n: (r, 0)),
            pl.BlockSpec((1, D_MODEL), lambda r, n: (0, 0)),
            pl.BlockSpec((D_MODEL, INPROJ_COLS), lambda r, n: (0, n)),
        ],
        out_specs=[
            pl.BlockSpec((row_tile, INPROJ_COLS), lambda r, n: (r, n)),
            pl.BlockSpec((row_tile, HG_WIDTH), lambda r, n: (r, 0)),
        ],
        out_shape=[
            jax.ShapeDtypeStruct((rows, IN_COLS), BF16),
            jax.ShapeDtypeStruct((rows, HG_WIDTH), F32),
        ],
        scratch_shapes=[pltpu.VMEM((row_tile, D_MODEL), BF16)],
        compiler_params=pltpu.CompilerParams(
            dimension_semantics=("parallel", "arbitrary"),
            vmem_limit_bytes=VMEM_LIMIT_BYTES),
        name="inproj",
    )(x2d, norm_g, w_in)


SB_PAIRS = SB_WIDTH // LANES
SB_UNDERFLOW = -104.0
SB_FIRST_SWEEP = 3
SB_QTILES = 2


def _sb_kernel(q_ref, k_ref, v_ref, g_ref, mk_ref, mv_ref, uw_ref, o_ref, acc_ref, carry_ref):
    j = pl.program_id(1)
    lane = lax.broadcasted_iota(jnp.int32, (KEY_TILE, LANES), 1)
    first = lane < SB_HEAD_DIM
    uw = uw_ref[...]
    pair_lanes = [slice(p * LANES, (p + 1) * LANES) for p in range(SB_PAIRS)]
    q_rows = [slice(t * KEY_TILE, (t + 1) * KEY_TILE) for t in range(SB_QTILES)]
    q2 = []
    for qr in q_rows:
        for sl in pair_lanes:
            q = q_ref[qr, sl]
            zero = jnp.zeros_like(q)
            q2.append(jnp.concatenate([jnp.where(first, q, zero), jnp.where(first, zero, q)], axis=0))

    acc_ref[...] = jnp.zeros_like(acc_ref)
    carry_ref[...] = jnp.zeros_like(carry_ref)

    def sweep(kt_ref, vt_ref, jobs):
        chains = [(t * SB_PAIRS + p, sl, rows, valids)
                  for t, (rows, valids) in enumerate(jobs) for p, sl in enumerate(pair_lanes)]
        zs = [lax.dot_general(q2[c], kt_ref[rows, sl], _NT, preferred_element_type=F32)
              for c, sl, rows, _ in chains]
        log_betas, sums = [], []
        for z, (_, _, _, valids) in zip(zs, chains):
            log_beta_n, hi_lo = [], []
            for n, valid in enumerate(valids):
                zn = z[:, n * KEY_TILE:(n + 1) * KEY_TILE]
                log_beta = jnp.minimum(zn, 0.0) - jnp.log(1.0 + jnp.exp(-jnp.abs(zn)))
                log_1mb = log_beta - zn
                if valid is not None:
                    log_1mb = jnp.where(valid, log_1mb, 0.0)
                hi, lo = _split_bf16(log_1mb)
                hi_lo.append(jnp.concatenate([hi, lo], axis=1))
                log_beta_n.append(log_beta)
            sums.append(jnp.dot(jnp.concatenate(hi_lo, axis=0), uw, preferred_element_type=F32))
            log_betas.append(log_beta_n)
        ws = []
        for idx, (c, _, _, valids) in enumerate(chains):
            carry = carry_ref[c]
            w_n = [None] * len(valids)
            for n in reversed(range(len(valids))):
                cs = sums[idx][n * 2 * KEY_TILE:(n + 1) * 2 * KEY_TILE]
                w = jnp.exp(log_betas[idx][n] + cs[:, :KEY_TILE] + carry)
                if valids[n] is not None:
                    w = jnp.where(valids[n], w, 0.0)
                w_n[n] = w.astype(BF16)
                carry = carry + cs[:, KEY_TILE:]
            carry_ref[c] = carry
            ws.append(jnp.concatenate(w_n, axis=1))
        for idx, (c, sl, rows, _) in enumerate(chains):
            acc_ref[c] += jnp.dot(ws[idx], vt_ref[rows, sl], preferred_element_type=F32)

    row2 = lax.broadcasted_iota(jnp.int32, (2 * KEY_TILE, KEY_TILE), 0) % KEY_TILE
    col2 = lax.broadcasted_iota(jnp.int32, (2 * KEY_TILE, KEY_TILE), 1)
    causal = col2 < row2

    def key_rows(tile, n_tiles):
        return pl.ds(pl.multiple_of(tile * KEY_TILE, KEY_TILE), n_tiles * KEY_TILE)

    i_first = j * SB_QTILES
    fused = j >= 1
    assert SB_FIRST_SWEEP - 1 <= SB_QTILES

    @pl.when(fused)
    def _():
        sweep(k_ref, v_ref, [(key_rows(i_first + t - (SB_FIRST_SWEEP - 1), SB_FIRST_SWEEP),
                              [None] * (SB_FIRST_SWEEP - 1) + [causal]) for t in range(SB_QTILES)])

    @pl.when(jnp.logical_not(fused))
    def _():
        sweep(k_ref, v_ref, [(key_rows(0, t + 1), [None] * t + [causal]) for t in range(SB_QTILES)])

    def live_rows():
        top = jnp.max(carry_ref[...].reshape(-1, 8, LANES), axis=0)
        return (jnp.max(top) > SB_UNDERFLOW).astype(jnp.int32)

    left = [jnp.where(fused, i_first + t - (SB_FIRST_SWEEP - 1), 0) for t in range(SB_QTILES)]

    def cond(c):
        it, live = c
        return jnp.logical_and(it < left[-1], live > 0)

    def body(c):
        it, _ = c
        jobs = []
        for t in range(SB_QTILES):
            has = it < left[t]
            tile = jnp.maximum(left[t] - 1 - it, 0)
            valid = None if t == SB_QTILES - 1 else jnp.logical_and(col2 >= 0, has)
            jobs.append((key_rows(tile, 1), [valid]))
        sweep(k_ref, v_ref, jobs)
        return it + 1, live_rows()

    _, live = lax.while_loop(cond, body, (jnp.int32(0), live_rows()))

    @pl.when(live > 0)
    def _():
        sweep(mk_ref, mv_ref, [(slice(None), [col2 >= META_ROWS - N_META])] * SB_QTILES)

    for t, qr in enumerate(q_rows):
        for p, sl in enumerate(pair_lanes):
            acc = acc_ref[t * SB_PAIRS + p]
            out = jnp.where(first, acc[:KEY_TILE], acc[KEY_TILE:])
            gate = g_ref[qr, sl].astype(F32)
            o_ref[qr, sl] = (out * (gate * _sigmoid(gate))).astype(BF16)


def _stick_breaking(proj, proj_meta, uw, batch, seq):
    assert SB_WIDTH == COL_TILE
    proj3 = proj.reshape(batch, seq, IN_COLS)
    q_blk = SB_QTILES * KEY_TILE
    assert seq % q_blk == 0
    return pl.pallas_call(
        _sb_kernel,
        grid=(batch, seq // q_blk),
        in_specs=[
            pl.BlockSpec((None, q_blk, SB_WIDTH), lambda b, i: (b, i, T_SB_Q)),
            pl.BlockSpec((None, seq, SB_WIDTH), lambda b, i: (b, 0, T_SB_K)),
            pl.BlockSpec((None, seq, SB_WIDTH), lambda b, i: (b, 0, T_SB_V)),
            pl.BlockSpec((None, q_blk, SB_WIDTH), lambda b, i: (b, i, T_SB_GATE)),
            pl.BlockSpec((META_ROWS, SB_WIDTH), lambda b, i: (0, T_SB_K)),
            pl.BlockSpec((META_ROWS, SB_WIDTH), lambda b, i: (0, T_SB_V)),
            pl.BlockSpec((2 * KEY_TILE, 2 * KEY_TILE), lambda b, i: (0, 0)),
        ],
        out_specs=pl.BlockSpec((None, q_blk, SB_WIDTH), lambda b, i: (b, i, 0)),
        out_shape=jax.ShapeDtypeStruct((batch, seq, SB_WIDTH), BF16),
        scratch_shapes=[pltpu.VMEM((SB_QTILES * SB_PAIRS, 2 * KEY_TILE, LANES), F32),
                        pltpu.VMEM((SB_QTILES * SB_PAIRS, 2 * KEY_TILE, LANES), F32)],
        compiler_params=pltpu.CompilerParams(
            dimension_semantics=("parallel", "arbitrary"),
            vmem_limit_bytes=VMEM_LIMIT_BYTES),
        name="stick_breaking",
    )(proj3, proj3, proj3, proj3, proj_meta, proj_meta, uw)


def _hg_decay(f_logit, lbv, sum_w, pad_rows):
    f = lbv + (1.0 - lbv) * _sigmoid(f_logit)
    if pad_rows:
        row = lax.broadcasted_iota(jnp.int32, f.shape, 0)
        f = jnp.where(row >= pad_rows, f, 1.0)
    g_hi, g_lo = _split_bf16(jnp.log(f))
    sums = jnp.dot(sum_w, jnp.concatenate([g_hi, g_lo], axis=0), preferred_element_type=F32)
    return sums, 1.0 - f


def _hg_next_state(state_t, b, kk, v):
    b_last = b[HG_CHUNK - 1:HG_CHUNK, :]
    k_dec = (kk * jnp.exp(b_last - b)).astype(BF16)
    return state_t * jnp.exp(b_last) + lax.dot_general(v, k_dec, _TN, preferred_element_type=F32)


def _hg_scores(qf, kk, sums, masks):
    a = jnp.where(masks[0], lax.dot_general(qf.astype(BF16), kk.astype(BF16), _NT,
                                            preferred_element_type=F32), 0.0)
    for lvl in range(len(HG_LEVELS)):
        e = jnp.exp(sums[(1 + lvl) * HG_CHUNK:(2 + lvl) * HG_CHUNK])
        part = lax.dot_general((qf * e).astype(BF16), (kk * e).astype(BF16), _NT,
                               preferred_element_type=F32)
        a = jnp.where(masks[1 + lvl], part, a)
    return a


def _hg_score_masks():
    row = lax.broadcasted_iota(jnp.int32, (HG_CHUNK, HG_CHUNK), 0)
    col = lax.broadcasted_iota(jnp.int32, (HG_CHUNK, HG_CHUNK), 1)
    return [row == col] + [
        ((row // (2 * m)) == (col // (2 * m))) & ((row // m) % 2 == 1) & ((col // m) % 2 == 0)
        for m in HG_LEVELS]


def _hg_kernel(q_ref, f_ref, v_ref, gate_ref, mf_ref, mv_ref, lbl_ref, gn_ref,
               sumw_ref, o_ref, state_ref, *, chunks_per_tile):
    t = pl.program_id(1)
    sum_w = sumw_ref[...]
    lbl = lbl_ref[...]
    ex = jnp.exp(lbl - jnp.max(lbl, axis=0, keepdims=True))
    lb = ex[0:1, :] / jnp.sum(ex, axis=0, keepdims=True)
    gn = gn_ref[...]
    head_lanes = [slice(h * HG_DIM, (h + 1) * HG_DIM) for h in range(HG_HEADS)]

    @pl.when(t == 0)
    def _():
        for h, sl in enumerate(head_lanes):
            b, kk = _hg_decay(mf_ref[:, sl], lb[:, sl], sum_w[:HG_CHUNK], HG_CHUNK - N_META)
            state_ref[h] = _hg_next_state(jnp.zeros((HG_DIM, HG_DIM), F32), b, kk, mv_ref[:, sl])

    masks = _hg_score_masks()

    def body(c, carry):
        base = c * (HG_UNROLL * HG_CHUNK)
        rows = [pl.ds(pl.multiple_of(base + u * HG_CHUNK, HG_CHUNK), HG_CHUNK)
                for u in range(HG_UNROLL)]
        decay = [[_hg_decay(f_ref[r, sl], lb[:, sl], sum_w, 0) for sl in head_lanes] for r in rows]
        states = [state_ref[h] for h in range(HG_HEADS)]
        scores, inter = [], []
        for u, r in enumerate(rows):
            for h, sl in enumerate(head_lanes):
                sums, kk = decay[u][h]
                b = sums[:HG_CHUNK]
                q = q_ref[r, sl].astype(F32)
                qf = q * _sigmoid(q) * HG_SCALE
                scores.append(_hg_scores(qf, kk, sums, masks))
                inter.append(lax.dot_general((qf * jnp.exp(b)).astype(BF16),
                                             states[h].astype(BF16), _NT,
                                             preferred_element_type=F32))
                states[h] = _hg_next_state(states[h], b, kk, v_ref[r, sl])
        for h in range(HG_HEADS):
            state_ref[h] = states[h]
        for u, r in enumerate(rows):
            for h, sl in enumerate(head_lanes):
                i = u * HG_HEADS + h
                o = inter[i] + jnp.dot(scores[i].astype(BF16), v_ref[r, sl],
                                       preferred_element_type=F32)
                o = o * lax.rsqrt(jnp.mean(o * o, axis=-1, keepdims=True) + EPS) * gn[:, sl]
                gate = gate_ref[r, sl].astype(F32)
                o_ref[r, sl] = (o * (gate * _sigmoid(gate))).astype(BF16)
        return carry

    lax.fori_loop(0, chunks_per_tile // HG_UNROLL, body, 0)


def _hgrn2(proj, f_proj, proj_meta, f_meta, lb_logits, hg_norm_g, sum_w, batch, seq, row_tile):
    assert seq % row_tile == 0 and row_tile % HG_CHUNK == 0
    proj3 = proj.reshape(batch, seq, IN_COLS)
    f3 = f_proj.reshape(batch, seq, HG_WIDTH)
    meta_blk = META_ROWS // HG_CHUNK - 1
    const = lambda b, t: (0, 0)
    return pl.pallas_call(
        functools.partial(_hg_kernel, chunks_per_tile=row_tile // HG_CHUNK),
        grid=(batch, seq // row_tile),
        in_specs=[
            pl.BlockSpec((None, row_tile, HG_WIDTH), lambda b, t: (b, t, T_HG_Q)),
            pl.BlockSpec((None, row_tile, HG_WIDTH), lambda b, t: (b, t, 0)),
            pl.BlockSpec((None, row_tile, HG_WIDTH), lambda b, t: (b, t, T_HG_I)),
            pl.BlockSpec((None, row_tile, HG_WIDTH), lambda b, t: (b, t, T_HG_GATE)),
            pl.BlockSpec((HG_CHUNK, HG_WIDTH), lambda b, t: (meta_blk, 0)),
            pl.BlockSpec((HG_CHUNK, HG_WIDTH), lambda b, t: (meta_blk, T_HG_I)),
            pl.BlockSpec(lb_logits.shape, const),
            pl.BlockSpec((1, HG_WIDTH), const),
            pl.BlockSpec(sum_w.shape, const),
        ],
        out_specs=pl.BlockSpec((None, row_tile, HG_WIDTH), lambda b, t: (b, t, 0)),
        out_shape=jax.ShapeDtypeStruct((batch, seq, HG_WIDTH), BF16),
        scratch_shapes=[pltpu.VMEM((HG_HEADS, HG_DIM, HG_DIM), F32)],
        compiler_params=pltpu.CompilerParams(
            dimension_semantics=("parallel", "arbitrary"),
            vmem_limit_bytes=VMEM_LIMIT_BYTES),
        name="hgrn2",
    )(proj3, f3, proj3, proj3, f_meta, proj_meta, lb_logits, hg_norm_g, sum_w)


MERGE_SLAB = 256


def _final_kernel(ysb_ref, yhg_ref, gsb_ref, ghg_ref, x_ref, wsb_ref, whg_ref, wout_ref, fg_ref,
                  o_ref, wsb_s, whg_s, wout_s):
    @pl.when(pl.program_id(0) == 0)
    def _():
        wsb_s[...] = wsb_ref[...].astype(BF16)
        whg_s[...] = whg_ref[...].astype(BF16)
        wout_s[...] = wout_ref[...].astype(BF16)

    n_slab = o_ref.shape[0] // MERGE_SLAB
    slabs = [pl.ds(s * MERGE_SLAB, MERGE_SLAB) for s in range(n_slab)]
    a = [jnp.dot(ysb_ref[sl, :], wsb_s[...], preferred_element_type=F32) for sl in slabs]
    b = [jnp.dot(yhg_ref[sl, :], whg_s[...], preferred_element_type=F32) for sl in slabs]
    merged = [(_sigmoid(gsb_ref[sl, :].astype(F32)) * a[s]
               + _sigmoid(ghg_ref[sl, :].astype(F32)) * b[s]).astype(BF16)
              for s, sl in enumerate(slabs)]
    h = [x_ref[sl, :] + jnp.dot(merged[s], wout_s[...], preferred_element_type=F32)
         for s, sl in enumerate(slabs)]
    for s, sl in enumerate(slabs):
        ms = jnp.mean(h[s] * h[s], axis=-1, keepdims=True)
        o_ref[sl, :] = h[s] * lax.rsqrt(ms + EPS) * fg_ref[...]


def _final(y_sb, y_hg, proj, x2d, w_sb_out, w_hg_out, w_out, final_norm_g, row_tile):
    rows = x2d.shape[0]
    assert rows % row_tile == 0
    gate_blk = (4 * SB_WIDTH + 4 * HG_WIDTH) // D_MODEL
    const = lambda r: (0, 0)
    return pl.pallas_call(
        _final_kernel,
        grid=(rows // row_tile,),
        in_specs=[
            pl.BlockSpec((row_tile, SB_WIDTH), lambda r: (r, 0)),
            pl.BlockSpec((row_tile, HG_WIDTH), lambda r: (r, 0)),
            pl.BlockSpec((row_tile, D_MODEL), lambda r: (r, gate_blk)),
            pl.BlockSpec((row_tile, D_MODEL), lambda r: (r, gate_blk + 1)),
            pl.BlockSpec((row_tile, D_MODEL), lambda r: (r, 0)),
            pl.BlockSpec((SB_WIDTH, D_MODEL), const),
            pl.BlockSpec((HG_WIDTH, D_MODEL), const),
            pl.BlockSpec((D_MODEL, D_MODEL), const),
            pl.BlockSpec((1, D_MODEL), const),
        ],
        out_specs=pl.BlockSpec((row_tile, D_MODEL), lambda r: (r, 0)),
        out_shape=jax.ShapeDtypeStruct((rows, D_MODEL), F32),
        scratch_shapes=[pltpu.VMEM((SB_WIDTH, D_MODEL), BF16),
                        pltpu.VMEM((HG_WIDTH, D_MODEL), BF16),
                        pltpu.VMEM((D_MODEL, D_MODEL), BF16)],
        compiler_params=pltpu.CompilerParams(
            dimension_semantics=("arbitrary",),
            vmem_limit_bytes=VMEM_LIMIT_BYTES),
        name="merge_out",
    )(y_sb, y_hg, proj, proj, x2d, w_sb_out, w_hg_out, w_out, final_norm_g)


def _suffix_sum_weights():
    j = np.arange(2 * KEY_TILE)[:, None] % KEY_TILE
    s = np.arange(2 * KEY_TILE)[None, :]
    return jnp.asarray(np.where(s < KEY_TILE, j > s, True), BF16)


def _hg_sum_weights():
    r = np.arange(HG_CHUNK)[:, None]
    c = np.arange(HG_CHUNK)[None, :]
    groups = [c <= r]
    for m in HG_LEVELS:
        ref = (r // (2 * m)) * 2 * m + m - 1
        groups.append(np.where(r > ref, (c > ref) & (c <= r), (c > r) & (c <= ref)))
    w = np.concatenate(groups, axis=0)
    return jnp.asarray(np.concatenate([w, w], axis=1), BF16)


def kernel(x, meta, norm_g, w_in, w_sb_out, w_hg_out, w_out, hg_norm_g, hg_lb_logits, final_norm_g):
    batch, seq, d = x.shape
    assert d == D_MODEL and meta.shape == (N_META, D_MODEL)
    assert norm_g.shape[0] == 1 and w_in.shape == (1, D_MODEL, IN_COLS)
    assert seq % 1024 == 0

    x2d = x.reshape(batch * seq, D_MODEL)
    meta_blk = jnp.concatenate(
        [jnp.zeros((META_ROWS - N_META, D_MODEL), x.dtype), meta.astype(x.dtype)], axis=0)
    w_in2 = w_in.reshape(D_MODEL, IN_COLS)

    proj, f_proj = _inproj(x2d, norm_g, w_in2, 2048)
    proj_meta, f_meta = _inproj(meta_blk, norm_g, w_in2, META_ROWS)

    y_sb = _stick_breaking(proj, proj_meta, _suffix_sum_weights(), batch, seq)
    y_hg = _hgrn2(proj, f_proj, proj_meta, f_meta, hg_lb_logits, hg_norm_g,
                  _hg_sum_weights(), batch, seq, 1024)

    out = _final(y_sb.reshape(batch * seq, SB_WIDTH), y_hg.reshape(batch * seq, HG_WIDTH), proj,
                 x2d, w_sb_out.reshape(SB_WIDTH, D_MODEL), w_hg_out.reshape(HG_WIDTH, D_MODEL),
                 w_out.reshape(D_MODEL, D_MODEL), final_norm_g.reshape(1, D_MODEL), 1024)
    return out.reshape(batch, seq, D_MODEL)
```

```python
import functools

import jax
import jax.numpy as jnp
import numpy as np
from jax import lax
from jax.experimental import pallas as pl
from jax.experimental.pallas import tpu as pltpu

F32 = jnp.float32
BF16 = jnp.bfloat16

D_MODEL = 1024
N_META = 16
SB_HEADS = 8
SB_HEAD_DIM = 64
SB_WIDTH = SB_HEADS * SB_HEAD_DIM
SB_SCALE = SB_HEAD_DIM ** -0.5
HG_HEADS = 4
HG_DIM = 128
HG_WIDTH = HG_HEADS * HG_DIM
HG_SCALE = HG_DIM ** -0.5
IN_COLS = 4 * SB_WIDTH + 4 * HG_WIDTH + 2 * D_MODEL
EPS = 1e-6

LANES = 128
VMEM_LIMIT_BYTES = 56 * 1024 * 1024

COL_TILE = 512
T_SB_Q, T_SB_K, T_SB_V, T_SB_GATE, T_HG_Q, T_HG_F, T_HG_I, T_HG_GATE = range(8)
N_COL_TILES = IN_COLS // COL_TILE

KEY_TILE = 128
META_ROWS = 128
HG_CHUNK = 64
HG_LEVELS = (1, 2, 4, 8, 16, 32)
HG_UNROLL = 2


_NT = (((1,), (1,)), ((), ()))
_TN = (((0,), (0,)), ((), ()))


def _sigmoid(x):
    return 1.0 / (1.0 + jnp.exp(-x))


def _split_bf16(x):
    hi = x.astype(BF16)
    lo = (x - hi.astype(F32)).astype(BF16)
    return hi, lo


INPROJ_COLS = 2 * COL_TILE
assert (T_SB_Q * COL_TILE) % INPROJ_COLS == 0


def _inproj_kernel(x_ref, g_ref, w_ref, o_ref, f_ref, u_ref):
    n = pl.program_id(1)

    @pl.when(n == 0)
    def _():
        x = x_ref[...]
        ms = jnp.mean(x * x, axis=-1, keepdims=True)
        u_ref[...] = (x * lax.rsqrt(ms + EPS) * g_ref[...]).astype(BF16)

    lane = lax.broadcasted_iota(jnp.int32, (1, INPROJ_COLS), 1)
    is_q = jnp.logical_and(n == (T_SB_Q * COL_TILE) // INPROJ_COLS, lane < SB_WIDTH)
    scale = jnp.where(is_q, SB_SCALE, 1.0).astype(F32)
    p = jnp.dot(u_ref[...], (w_ref[...] * scale).astype(BF16), preferred_element_type=F32)
    o_ref[...] = p.astype(BF16)

    f_col = (T_HG_F * COL_TILE) % INPROJ_COLS

    @pl.when(n == (T_HG_F * COL_TILE) // INPROJ_COLS)
    def _():
        f_ref[...] = p[:, f_col:f_col + HG_WIDTH]


def _inproj(x2d, norm_g, w_in, row_tile):
    rows = x2d.shape[0]
    assert rows % row_tile == 0
    return pl.pallas_call(
        _inproj_kernel,
        grid=(rows // row_tile, IN_COLS // INPROJ_COLS),
        in_specs=[
            pl.BlockSpec((row_tile, D_MODEL), lambda r, n: (r, 0)),
            pl.BlockSpec((1, D_MODEL), lambda r, n: (0, 0)),
            pl.BlockSpec((D_MODEL, INPROJ_COLS), lambda r, n: (0, n)),
        ],
        out_specs=[
            pl.BlockSpec((row_tile, INPROJ_COLS), lambda r, n: (r, n)),
            pl.BlockSpec((row_tile, HG_WIDTH), lambda r, n: (r, 0)),
        ],
        out_shape=[
            jax.ShapeDtypeStruct((rows, IN_COLS), BF16),
            jax.ShapeDtypeStruct((rows, HG_WIDTH), F32),
        ],
        scratch_shapes=[pltpu.VMEM((row_tile, D_MODEL), BF16)],
        compiler_params=pltpu.CompilerParams(
            dimension_semantics=("parallel", "arbitrary"),
            vmem_limit_bytes=VMEM_LIMIT_BYTES),
        name="inproj",
    )(x2d, norm_g, w_in)


SB_PAIRS = SB_WIDTH // LANES
SB_UNDERFLOW = -104.0
SB_FIRST_SWEEP = 3
SB_QTILES = 4


def _sb_kernel(q_ref, k_ref, v_ref, g_ref, mk_ref, mv_ref, uw_ref, o_ref, acc_ref, carry_ref):
    j = pl.program_id(1)
    lane = lax.broadcasted_iota(jnp.int32, (KEY_TILE, LANES), 1)
    first = lane < SB_HEAD_DIM
    uw = uw_ref[...]
    pair_lanes = [slice(p * LANES, (p + 1) * LANES) for p in range(SB_PAIRS)]
    q_rows = [slice(t * KEY_TILE, (t + 1) * KEY_TILE) for t in range(SB_QTILES)]
    q2 = []
    for qr in q_rows:
        for sl in pair_lanes:
            q = q_ref[qr, sl]
            zero = jnp.zeros_like(q)
            q2.append(jnp.concatenate([jnp.where(first, q, zero), jnp.where(first, zero, q)], axis=0))

    def sweep(kt_ref, vt_ref, jobs, first_pass=False):
        chains = [(t * SB_PAIRS + p, sl, rows, valids)
                  for t, (rows, valids) in enumerate(jobs) for p, sl in enumerate(pair_lanes)]
        zs = [lax.dot_general(q2[c], kt_ref[rows, sl], _NT, preferred_element_type=F32)
              for c, sl, rows, _ in chains]
        log_betas, sums = [], []
        for z, (_, _, _, valids) in zip(zs, chains):
            log_beta_n, hi_lo = [], []
            for n, valid in enumerate(valids):
                zn = z[:, n * KEY_TILE:(n + 1) * KEY_TILE]
                log_beta = jnp.minimum(zn, 0.0) - jnp.log(1.0 + jnp.exp(-jnp.abs(zn)))
                log_1mb = log_beta - zn
                if valid is not None:
                    log_1mb = jnp.where(valid, log_1mb, 0.0)
                hi, lo = _split_bf16(log_1mb)
                hi_lo.append(jnp.concatenate([hi, lo], axis=1))
                log_beta_n.append(log_beta)
            sums.append(jnp.dot(jnp.concatenate(hi_lo, axis=0), uw, preferred_element_type=F32))
            log_betas.append(log_beta_n)
        ws = []
        for idx, (c, _, _, valids) in enumerate(chains):
            carry = None if first_pass else carry_ref[c]
            w_n = [None] * len(valids)
            for n in reversed(range(len(valids))):
                cs = sums[idx][n * 2 * KEY_TILE:(n + 1) * 2 * KEY_TILE]
                log_w = log_betas[idx][n] + cs[:, :KEY_TILE]
                w = jnp.exp(log_w if carry is None else log_w + carry)
                if valids[n] is not None:
                    w = jnp.where(valids[n], w, 0.0)
                w_n[n] = w.astype(BF16)
                carry = cs[:, KEY_TILE:] if carry is None else carry + cs[:, KEY_TILE:]
            carry_ref[c] = carry
            ws.append(jnp.concatenate(w_n, axis=1))
        for idx, (c, sl, rows, _) in enumerate(chains):
            pv = jnp.dot(ws[idx], vt_ref[rows, sl], preferred_element_type=F32)
            if first_pass:
                acc_ref[c] = pv
            else:
                acc_ref[c] += pv

    row2 = lax.broadcasted_iota(jnp.int32, (2 * KEY_TILE, KEY_TILE), 0) % KEY_TILE
    col2 = lax.broadcasted_iota(jnp.int32, (2 * KEY_TILE, KEY_TILE), 1)
    causal = col2 < row2

    def key_rows(tile, n_tiles):
        return pl.ds(pl.multiple_of(tile * KEY_TILE, KEY_TILE), n_tiles * KEY_TILE)

    i_first = j * SB_QTILES
    fused = j >= 1
    assert SB_FIRST_SWEEP - 1 <= SB_QTILES

    @pl.when(fused)
    def _():
        sweep(k_ref, v_ref, [(key_rows(i_first + t - (SB_FIRST_SWEEP - 1), SB_FIRST_SWEEP),
                              [None] * (SB_FIRST_SWEEP - 1) + [causal]) for t in range(SB_QTILES)],
              first_pass=True)

    @pl.when(jnp.logical_not(fused))
    def _():
        sweep(k_ref, v_ref, [(key_rows(0, t + 1), [None] * t + [causal]) for t in range(SB_QTILES)],
              first_pass=True)

    def live_rows():
        top = jnp.max(carry_ref[...].reshape(-1, 8, LANES), axis=0)
        return (jnp.max(top) > SB_UNDERFLOW).astype(jnp.int32)

    left = [jnp.where(fused, i_first + t - (SB_FIRST_SWEEP - 1), 0) for t in range(SB_QTILES)]

    def cond(c):
        it, live = c
        return jnp.logical_and(it < left[-1], live > 0)

    def body(c):
        it, _ = c
        jobs = []
        for t in range(SB_QTILES):
            has = it < left[t]
            tile = jnp.maximum(left[t] - 1 - it, 0)
            valid = None if t == SB_QTILES - 1 else jnp.logical_and(col2 >= 0, has)
            jobs.append((key_rows(tile, 1), [valid]))
        sweep(k_ref, v_ref, jobs)
        return it + 1, live_rows()

    _, live = lax.while_loop(cond, body, (jnp.int32(0), live_rows()))

    @pl.when(live > 0)
    def _():
        sweep(mk_ref, mv_ref, [(slice(None), [col2 >= META_ROWS - N_META])] * SB_QTILES)

    for t, qr in enumerate(q_rows):
        for p, sl in enumerate(pair_lanes):
            acc = acc_ref[t * SB_PAIRS + p]
            out = jnp.where(first, acc[:KEY_TILE], acc[KEY_TILE:])
            gate = g_ref[qr, sl].astype(F32)
            o_ref[qr, sl] = (out * (gate * _sigmoid(gate))).astype(BF16)


def _stick_breaking(proj, proj_meta, uw, batch, seq):
    assert SB_WIDTH == COL_TILE
    proj3 = proj.reshape(batch, seq, IN_COLS)
    q_blk = SB_QTILES * KEY_TILE
    assert seq % q_blk == 0
    return pl.pallas_call(
        _sb_kernel,
        grid=(batch, seq // q_blk),
        in_specs=[
            pl.BlockSpec((None, q_blk, SB_WIDTH), lambda b, i: (b, i, T_SB_Q)),
            pl.BlockSpec((None, seq, SB_WIDTH), lambda b, i: (b, 0, T_SB_K)),
            pl.BlockSpec((None, seq, SB_WIDTH), lambda b, i: (b, 0, T_SB_V)),
            pl.BlockSpec((None, q_blk, SB_WIDTH), lambda b, i: (b, i, T_SB_GATE)),
            pl.BlockSpec((META_ROWS, SB_WIDTH), lambda b, i: (0, T_SB_K)),
            pl.BlockSpec((META_ROWS, SB_WIDTH), lambda b, i: (0, T_SB_V)),
            pl.BlockSpec((2 * KEY_TILE, 2 * KEY_TILE), lambda b, i: (0, 0)),
        ],
        out_specs=pl.BlockSpec((None, q_blk, SB_WIDTH), lambda b, i: (b, i, 0)),
        out_shape=jax.ShapeDtypeStruct((batch, seq, SB_WIDTH), BF16),
        scratch_shapes=[pltpu.VMEM((SB_QTILES * SB_PAIRS, 2 * KEY_TILE, LANES), F32),
                        pltpu.VMEM((SB_QTILES * SB_PAIRS, 2 * KEY_TILE, LANES), F32)],
        compiler_params=pltpu.CompilerParams(
            dimension_semantics=("parallel", "arbitrary"),
            vmem_limit_bytes=VMEM_LIMIT_BYTES),
        name="stick_breaking",
    )(proj3, proj3, proj3, proj3, proj_meta, proj_meta, uw)


def _hg_decay(f_logit, lbv, sum_w, pad_rows):
    f = lbv + (1.0 - lbv) * _sigmoid(f_logit)
    if pad_rows:
        row = lax.broadcasted_iota(jnp.int32, f.shape, 0)
        f = jnp.where(row >= pad_rows, f, 1.0)
    g_hi, g_lo = _split_bf16(jnp.log(f))
    sums = jnp.dot(sum_w, jnp.concatenate([g_hi, g_lo], axis=0), preferred_element_type=F32)
    return sums, 1.0 - f


def _hg_next_state(state_t, b, kk, v):
    b_last = b[HG_CHUNK - 1:HG_CHUNK, :]
    k_dec = (kk * jnp.exp(b_last - b)).astype(BF16)
    return state_t * jnp.exp(b_last) + lax.dot_general(v, k_dec, _TN, preferred_element_type=F32)


def _hg_scores(qf, kk, sums, masks):
    a = jnp.where(masks[0], lax.dot_general(qf.astype(BF16), kk.astype(BF16), _NT,
                                            preferred_element_type=F32), 0.0)
    for lvl in range(len(HG_LEVELS)):
        e = jnp.exp(sums[(1 + lvl) * HG_CHUNK:(2 + lvl) * HG_CHUNK])
        part = lax.dot_general((qf * e).astype(BF16), (kk * e).astype(BF16), _NT,
                               preferred_element_type=F32)
        a = jnp.where(masks[1 + lvl], part, a)
    return a


def _hg_score_masks():
    row = lax.broadcasted_iota(jnp.int32, (HG_CHUNK, HG_CHUNK), 0)
    col = lax.broadcasted_iota(jnp.int32, (HG_CHUNK, HG_CHUNK), 1)
    return [row == col] + [
        ((row // (2 * m)) == (col // (2 * m))) & ((row // m) % 2 == 1) & ((col // m) % 2 == 0)
        for m in HG_LEVELS]


def _hg_kernel(q_ref, f_ref, v_ref, gate_ref, mf_ref, mv_ref, lbl_ref, gn_ref,
               sumw_ref, o_ref, state_ref, *, chunks_per_tile):
    t = pl.program_id(1)
    sum_w = sumw_ref[...]
    lbl = lbl_ref[...]
    ex = jnp.exp(lbl - jnp.max(lbl, axis=0, keepdims=True))
    lb = ex[0:1, :] / jnp.sum(ex, axis=0, keepdims=True)
    gn = gn_ref[...]
    head_lanes = [slice(h * HG_DIM, (h + 1) * HG_DIM) for h in range(HG_HEADS)]

    @pl.when(t == 0)
    def _():
        for h, sl in enumerate(head_lanes):
            b, kk = _hg_decay(mf_ref[:, sl], lb[:, sl], sum_w[:HG_CHUNK], HG_CHUNK - N_META)
            state_ref[h] = _hg_next_state(jnp.zeros((HG_DIM, HG_DIM), F32), b, kk, mv_ref[:, sl])

    masks = _hg_score_masks()

    def body(c, carry):
        base = c * (HG_UNROLL * HG_CHUNK)
        rows = [pl.ds(pl.multiple_of(base + u * HG_CHUNK, HG_CHUNK), HG_CHUNK)
                for u in range(HG_UNROLL)]
        decay = [[_hg_decay(f_ref[r, sl], lb[:, sl], sum_w, 0) for sl in head_lanes] for r in rows]
        states = [state_ref[h] for h in range(HG_HEADS)]
        scores, inter = [], []
        for u, r in enumerate(rows):
            for h, sl in enumerate(head_lanes):
                sums, kk = decay[u][h]
                b = sums[:HG_CHUNK]
                q = q_ref[r, sl].astype(F32)
                qf = q * _sigmoid(q) * HG_SCALE
                scores.append(_hg_scores(qf, kk, sums, masks))
                inter.append(lax.dot_general((qf * jnp.exp(b)).astype(BF16),
                                             states[h].astype(BF16), _NT,
                                             preferred_element_type=F32))
                states[h] = _hg_next_state(states[h], b, kk, v_ref[r, sl])
        for h in range(HG_HEADS):
            state_ref[h] = states[h]
        for u, r in enumerate(rows):
            for h, sl in enumerate(head_lanes):
                i = u * HG_HEADS + h
                o = inter[i] + jnp.dot(scores[i].astype(BF16), v_ref[r, sl],
                                       preferred_element_type=F32)
                o = o * lax.rsqrt(jnp.mean(o * o, axis=-1, keepdims=True) + EPS) * gn[:, sl]
                gate = gate_ref[r, sl].astype(F32)
                o_ref[r, sl] = (o * (gate * _sigmoid(gate))).astype(BF16)
        return carry

    lax.fori_loop(0, chunks_per_tile // HG_UNROLL, body, 0)


def _hgrn2(proj, f_proj, proj_meta, f_meta, lb_logits, hg_norm_g, sum_w, batch, seq, row_tile):
    assert seq % row_tile == 0 and row_tile % HG_CHUNK == 0
    proj3 = proj.reshape(batch, seq, IN_COLS)
    f3 = f_proj.reshape(batch, seq, HG_WIDTH)
    meta_blk = META_ROWS // HG_CHUNK - 1
    const = lambda b, t: (0, 0)
    return pl.pallas_call(
        functools.partial(_hg_kernel, chunks_per_tile=row_tile // HG_CHUNK),
        grid=(batch, seq // row_tile),
        in_specs=[
            pl.BlockSpec((None, row_tile, HG_WIDTH), lambda b, t: (b, t, T_HG_Q)),
            pl.BlockSpec((None, row_tile, HG_WIDTH), lambda b, t: (b, t, 0)),
            pl.BlockSpec((None, row_tile, HG_WIDTH), lambda b, t: (b, t, T_HG_I)),
            pl.BlockSpec((None, row_tile, HG_WIDTH), lambda b, t: (b, t, T_HG_GATE)),
            pl.BlockSpec((HG_CHUNK, HG_WIDTH), lambda b, t: (meta_blk, 0)),
            pl.BlockSpec((HG_CHUNK, HG_WIDTH), lambda b, t: (meta_blk, T_HG_I)),
            pl.BlockSpec(lb_logits.shape, const),
            pl.BlockSpec((1, HG_WIDTH), const),
            pl.BlockSpec(sum_w.shape, const),
        ],
        out_specs=pl.BlockSpec((None, row_tile, HG_WIDTH), lambda b, t: (b, t, 0)),
        out_shape=jax.ShapeDtypeStruct((batch, seq, HG_WIDTH), BF16),
        scratch_shapes=[pltpu.VMEM((HG_HEADS, HG_DIM, HG_DIM), F32)],
        compiler_params=pltpu.CompilerParams(
            dimension_semantics=("parallel", "arbitrary"),
            vmem_limit_bytes=VMEM_LIMIT_BYTES),
        name="hgrn2",
    )(proj3, f3, proj3, proj3, f_meta, proj_meta, lb_logits, hg_norm_g, sum_w)


MERGE_SLAB = 256


def _final_kernel(ysb_ref, yhg_ref, gsb_ref, ghg_ref, x_ref, wsb_ref, whg_ref, wout_ref, fg_ref,
                  o_ref, wsb_s, whg_s, wout_s):
    @pl.when(pl.program_id(0) == 0)
    def _():
        wsb_s[...] = wsb_ref[...].astype(BF16)
        whg_s[...] = whg_ref[...].astype(BF16)
        wout_s[...] = wout_ref[...].astype(BF16)

    n_slab = o_ref.shape[0] // MERGE_SLAB
    slabs = [pl.ds(s * MERGE_SLAB, MERGE_SLAB) for s in range(n_slab)]
    a = [jnp.dot(ysb_ref[sl, :], wsb_s[...], preferred_element_type=F32) for sl in slabs]
    b = [jnp.dot(yhg_ref[sl, :], whg_s[...], preferred_element_type=F32) for sl in slabs]
    merged = [(_sigmoid(gsb_ref[sl, :].astype(F32)) * a[s]
               + _sigmoid(ghg_ref[sl, :].astype(F32)) * b[s]).astype(BF16)
              for s, sl in enumerate(slabs)]
    h = [x_ref[sl, :] + jnp.dot(merged[s], wout_s[...], preferred_element_type=F32)
         for s, sl in enumerate(slabs)]
    for s, sl in enumerate(slabs):
        ms = jnp.mean(h[s] * h[s], axis=-1, keepdims=True)
        o_ref[sl, :] = h[s] * lax.rsqrt(ms + EPS) * fg_ref[...]


def _final(y_sb, y_hg, proj, x2d, w_sb_out, w_hg_out, w_out, final_norm_g, row_tile):
    rows = x2d.shape[0]
    assert rows % row_tile == 0
    gate_blk = (4 * SB_WIDTH + 4 * HG_WIDTH) // D_MODEL
    const = lambda r: (0, 0)
    return pl.pallas_call(
        _final_kernel,
        grid=(rows // row_tile,),
        in_specs=[
            pl.BlockSpec((row_tile, SB_WIDTH), lambda r: (r, 0)),
            pl.BlockSpec((row_tile, HG_WIDTH), lambda r: (r, 0)),
            pl.BlockSpec((row_tile, D_MODEL), lambda r: (r, gate_blk)),
            pl.BlockSpec((row_tile, D_MODEL), lambda r: (r, gate_blk + 1)),
            pl.BlockSpec((row_tile, D_MODEL), lambda r: (r, 0)),
            pl.BlockSpec((SB_WIDTH, D_MODEL), const),
            pl.BlockSpec((HG_WIDTH, D_MODEL), const),
            pl.BlockSpec((D_MODEL, D_MODEL), const),
            pl.BlockSpec((1, D_MODEL), const),
        ],
        out_specs=pl.BlockSpec((row_tile, D_MODEL), lambda r: (r, 0)),
        out_shape=jax.ShapeDtypeStruct((rows, D_MODEL), F32),
        scratch_shapes=[pltpu.VMEM((SB_WIDTH, D_MODEL), BF16),
                        pltpu.VMEM((HG_WIDTH, D_MODEL), BF16),
                        pltpu.VMEM((D_MODEL, D_MODEL), BF16)],
        compiler_params=pltpu.CompilerParams(
            dimension_semantics=("arbitrary",),
            vmem_limit_bytes=VMEM_LIMIT_BYTES),
        name="merge_out",
    )(y_sb, y_hg, proj, proj, x2d, w_sb_out, w_hg_out, w_out, final_norm_g)


def _suffix_sum_weights():
    j = np.arange(2 * KEY_TILE)[:, None] % KEY_TILE
    s = np.arange(2 * KEY_TILE)[None, :]
    return jnp.asarray(np.where(s < KEY_TILE, j > s, True), BF16)


def _hg_sum_weights():
    r = np.arange(HG_CHUNK)[:, None]
    c = np.arange(HG_CHUNK)[None, :]
    groups = [c <= r]
    for m in HG_LEVELS:
        ref = (r // (2 * m)) * 2 * m + m - 1
        groups.append(np.where(r > ref, (c > ref) & (c <= r), (c > r) & (c <= ref)))
    w = np.concatenate(groups, axis=0)
    return jnp.asarray(np.concatenate([w, w], axis=1), BF16)


def kernel(x, meta, norm_g, w_in, w_sb_out, w_hg_out, w_out, hg_norm_g, hg_lb_logits, final_norm_g):
    batch, seq, d = x.shape
    assert d == D_MODEL and meta.shape == (N_META, D_MODEL)
    assert norm_g.shape[0] == 1 and w_in.shape == (1, D_MODEL, IN_COLS)
    assert seq % 1024 == 0

    x2d = x.reshape(batch * seq, D_MODEL)
    meta_blk = jnp.concatenate(
        [jnp.zeros((META_ROWS - N_META, D_MODEL), x.dtype), meta.astype(x.dtype)], axis=0)
    w_in2 = w_in.reshape(D_MODEL, IN_COLS)

    proj, f_proj = _inproj(x2d, norm_g, w_in2, 2048)
    proj_meta, f_meta = _inproj(meta_blk, norm_g, w_in2, META_ROWS)

    y_sb = _stick_breaking(proj, proj_meta, _suffix_sum_weights(), batch, seq)
    y_hg = _hgrn2(proj, f_proj, proj_meta, f_meta, hg_lb_logits, hg_norm_g,
                  _hg_sum_weights(), batch, seq, 1024)

    out = _final(y_sb.reshape(batch * seq, SB_WIDTH), y_hg.reshape(batch * seq, HG_WIDTH), proj,
                 x2d, w_sb_out.reshape(SB_WIDTH, D_MODEL), w_hg_out.reshape(HG_WIDTH, D_MODEL),
                 w_out.reshape(D_MODEL, D_MODEL), final_norm_g.reshape(1, D_MODEL), 1024)
    return out.reshape(batch, seq, D_MODEL)
```

```python
import functools

import jax
import jax.numpy as jnp
import numpy as np
from jax import lax
from jax.experimental import pallas as pl
from jax.experimental.pallas import tpu as pltpu

F32 = jnp.float32
BF16 = jnp.bfloat16

D_MODEL = 1024
N_META = 16
SB_HEADS = 8
SB_HEAD_DIM = 64
SB_WIDTH = SB_HEADS * SB_HEAD_DIM
SB_SCALE = SB_HEAD_DIM ** -0.5
HG_HEADS = 4
HG_DIM = 128
HG_WIDTH = HG_HEADS * HG_DIM
HG_SCALE = HG_DIM ** -0.5
IN_COLS = 4 * SB_WIDTH + 4 * HG_WIDTH + 2 * D_MODEL
EPS = 1e-6

LANES = 128
VMEM_LIMIT_BYTES = 56 * 1024 * 1024

COL_TILE = 512
T_SB_Q, T_SB_K, T_SB_V, T_SB_GATE, T_HG_Q, T_HG_F, T_HG_I, T_HG_GATE = range(8)
N_COL_TILES = IN_COLS // COL_TILE

KEY_TILE = 128
META_ROWS = 128
HG_CHUNK = 64
HG_LEVELS = (1, 2, 4, 8, 16, 32)
HG_UNROLL = 2


_NT = (((1,), (1,)), ((), ()))
_TN = (((0,), (0,)), ((), ()))


def _sigmoid(x):
    return 1.0 / (1.0 + jnp.exp(-x))


def _split_bf16(x):
    hi = x.astype(BF16)
    lo = (x - hi.astype(F32)).astype(BF16)
    return hi, lo


INPROJ_COLS = 2 * COL_TILE
assert (T_SB_Q * COL_TILE) % INPROJ_COLS == 0


def _inproj_kernel(x_ref, g_ref, w_ref, o_ref, f_ref, u_ref):
    n = pl.program_id(1)

    @pl.when(n == 0)
    def _():
        x = x_ref[...]
        ms = jnp.mean(x * x, axis=-1, keepdims=True)
        u_ref[...] = (x * lax.rsqrt(ms + EPS) * g_ref[...]).astype(BF16)

    lane = lax.broadcasted_iota(jnp.int32, (1, INPROJ_COLS), 1)
    is_q = jnp.logical_and(n == (T_SB_Q * COL_TILE) // INPROJ_COLS, lane < SB_WIDTH)
    scale = jnp.where(is_q, SB_SCALE, 1.0).astype(F32)
    p = jnp.dot(u_ref[...], (w_ref[...] * scale).astype(BF16), preferred_element_type=F32)
    o_ref[...] = p.astype(BF16)

    f_col = (T_HG_F * COL_TILE) % INPROJ_COLS

    @pl.when(n == (T_HG_F * COL_TILE) // INPROJ_COLS)
    def _():
        f_ref[...] = p[:, f_col:f_col + HG_WIDTH]


def _inproj(x2d, norm_g, w_in, row_tile):
    rows = x2d.shape[0]
    assert rows % row_tile == 0
    return pl.pallas_call(
        _inproj_kernel,
        grid=(rows // row_tile, IN_COLS // INPROJ_COLS),
        in_specs=[
            pl.BlockSpec((row_tile, D_MODEL), lambda r, n: (r, 0)),
            pl.BlockSpec((1, D_MODEL), lambda r, n: (0, 0)),
            pl.BlockSpec((D_MODEL, INPROJ_COLS), lambda r, n: (0, n)),
        ],
        out_specs=[
            pl.BlockSpec((row_tile, INPROJ_COLS), lambda r, n: (r, n)),
            pl.BlockSpec((row_tile, HG_WIDTH), lambda r, n: (r, 0)),
        ],
        out_shape=[
            jax.ShapeDtypeStruct((rows, IN_COLS), BF16),
            jax.ShapeDtypeStruct((rows, HG_WIDTH), F32),
        ],
        scratch_shapes=[pltpu.VMEM((row_tile, D_MODEL), BF16)],
        compiler_params=pltpu.CompilerParams(
            dimension_semantics=("parallel", "arbitrary"),
            vmem_limit_bytes=VMEM_LIMIT_BYTES),
        name="inproj",
    )(x2d, norm_g, w_in)


SB_PAIRS = SB_WIDTH // LANES
SB_UNDERFLOW = -104.0
SB_QTILES = 4
SB_EARLY = 64


def _sb_kernel(q_ref, k_ref, v_ref, g_ref, mk_ref, mv_ref, uw_ref, o_ref, acc_ref, carry_ref):
    j = pl.program_id(1)
    lane = lax.broadcasted_iota(jnp.int32, (KEY_TILE, LANES), 1)
    first = lane < SB_HEAD_DIM
    uw = uw_ref[...]
    pair_lanes = [slice(p * LANES, (p + 1) * LANES) for p in range(SB_PAIRS)]
    q_rows = [slice(t * KEY_TILE, (t + 1) * KEY_TILE) for t in range(SB_QTILES)]
    q2 = []
    for qr in q_rows:
        for sl in pair_lanes:
            q = q_ref[qr, sl]
            zero = jnp.zeros_like(q)
            q2.append(jnp.concatenate([jnp.where(first, q, zero), jnp.where(first, zero, q)], axis=0))

    def early(x):
        return jnp.concatenate([x[:SB_EARLY], x[KEY_TILE:KEY_TILE + SB_EARLY]], axis=0)

    def with_early(x, e, combine):
        return jnp.concatenate([combine(x[:SB_EARLY], e[:SB_EARLY]), x[SB_EARLY:KEY_TILE],
                                combine(x[KEY_TILE:KEY_TILE + SB_EARLY], e[SB_EARLY:]),
                                x[KEY_TILE + SB_EARLY:]], axis=0)

    def log_terms(z, valid):
        log_beta = jnp.minimum(z, 0.0) - jnp.log(1.0 + jnp.exp(-jnp.abs(z)))
        log_1mb = log_beta - z
        if valid is not None:
            log_1mb = jnp.where(valid, log_1mb, 0.0)
        hi, lo = _split_bf16(log_1mb)
        return log_beta, jnp.concatenate([hi, lo], axis=1)

    def sweep(kt_ref, vt_ref, jobs, first_pass=False):
        chains = [(t * SB_PAIRS + p, sl) + job
                  for t, job in enumerate(jobs) for p, sl in enumerate(pair_lanes)]
        zs = [lax.dot_general(q2[c], kt_ref[rows, sl], _NT, preferred_element_type=F32)
              for c, sl, rows, _, _ in chains]
        zes = [None if erows is None else
               lax.dot_general(early(q2[c]), kt_ref[erows, sl], _NT, preferred_element_type=F32)
               for c, sl, _, _, erows in chains]
        log_betas, sums = [], []
        for z, ze, (_, _, _, valids, _) in zip(zs, zes, chains):
            terms = [] if ze is None else [log_terms(ze, None)]
            terms += [log_terms(z[:, n * KEY_TILE:(n + 1) * KEY_TILE], valid)
                      for n, valid in enumerate(valids)]
            sums.append(jnp.dot(jnp.concatenate([hl for _, hl in terms], axis=0), uw,
                                preferred_element_type=F32))
            log_betas.append([lb for lb, _ in terms])
        ws, wes = [], []
        for idx, (c, _, _, valids, erows) in enumerate(chains):
            off = 0 if erows is None else 2 * SB_EARLY
            lbs = log_betas[idx] if erows is None else log_betas[idx][1:]
            carry = None if first_pass else carry_ref[c]
            w_n = [None] * len(valids)
            for n in reversed(range(len(valids))):
                cs = sums[idx][off + n * 2 * KEY_TILE:off + (n + 1) * 2 * KEY_TILE]
                log_w = lbs[n] + cs[:, :KEY_TILE]
                w = jnp.exp(log_w if carry is None else log_w + carry)
                if valids[n] is not None:
                    w = jnp.where(valids[n], w, 0.0)
                w_n[n] = w.astype(BF16)
                carry = cs[:, KEY_TILE:] if carry is None else carry + cs[:, KEY_TILE:]
            if erows is None:
                wes.append(None)
            else:
                cs = sums[idx][:off]
                wes.append(jnp.exp(log_betas[idx][0] + cs[:, :KEY_TILE] + early(carry)).astype(BF16))
                carry = with_early(carry, cs[:, KEY_TILE:], jnp.add)
            carry_ref[c] = carry
            ws.append(jnp.concatenate(w_n, axis=1))
        for idx, (c, sl, rows, _, erows) in enumerate(chains):
            pv = jnp.dot(ws[idx], vt_ref[rows, sl], preferred_element_type=F32)
            if erows is not None:
                pv = with_early(pv, jnp.dot(wes[idx], vt_ref[erows, sl], preferred_element_type=F32),
                                jnp.add)
            if first_pass:
                acc_ref[c] = pv
            else:
                acc_ref[c] += pv

    row2 = lax.broadcasted_iota(jnp.int32, (2 * KEY_TILE, KEY_TILE), 0) % KEY_TILE
    col2 = lax.broadcasted_iota(jnp.int32, (2 * KEY_TILE, KEY_TILE), 1)
    causal = col2 < row2

    def key_rows(tile, n_tiles):
        return pl.ds(pl.multiple_of(tile * KEY_TILE, KEY_TILE), n_tiles * KEY_TILE)

    i_first = j * SB_QTILES
    fused = j >= 1
    assert SB_QTILES >= 2

    @pl.when(fused)
    def _():
        sweep(k_ref, v_ref, [(key_rows(i_first + t - 1, 2), [None, causal],
                              key_rows(i_first + t - 2, 1)) for t in range(SB_QTILES)],
              first_pass=True)

    @pl.when(jnp.logical_not(fused))
    def _():
        sweep(k_ref, v_ref, [(key_rows(0, t + 1), [None] * t + [causal], None)
                             for t in range(SB_QTILES)], first_pass=True)

    def live_rows():
        top = jnp.max(carry_ref[...].reshape(-1, 8, LANES), axis=0)
        return (jnp.max(top) > SB_UNDERFLOW).astype(jnp.int32)

    left = [jnp.where(fused, i_first + t - 2, 0) for t in range(SB_QTILES)]
    late = row2 >= SB_EARLY

    def cond(c):
        it, live = c
        return jnp.logical_and(it < left[-1], live > 0)

    def body(c):
        it, _ = c
        catch_up = it < 0
        jobs = []
        for t in range(SB_QTILES):
            has = jnp.logical_or(catch_up, it < left[t])
            tile = jnp.where(catch_up, left[t], jnp.maximum(left[t] - 1 - it, 0))
            valid = jnp.logical_and(has, jnp.logical_or(late, jnp.logical_not(catch_up)))
            jobs.append((key_rows(tile, 1), [valid], None))
        sweep(k_ref, v_ref, jobs)
        return it + 1, live_rows()

    start = jnp.where(fused, -1, 0).astype(jnp.int32)
    _, live = lax.while_loop(cond, body, (start, live_rows()))

    @pl.when(live > 0)
    def _():
        sweep(mk_ref, mv_ref, [(slice(None), [col2 >= META_ROWS - N_META], None)] * SB_QTILES)

    for t, qr in enumerate(q_rows):
        for p, sl in enumerate(pair_lanes):
            acc = acc_ref[t * SB_PAIRS + p]
            out = jnp.where(first, acc[:KEY_TILE], acc[KEY_TILE:])
            gate = g_ref[qr, sl].astype(F32)
            o_ref[qr, sl] = (out * (gate * _sigmoid(gate))).astype(BF16)


def _stick_breaking(proj, proj_meta, uw, batch, seq):
    assert SB_WIDTH == COL_TILE
    proj3 = proj.reshape(batch, seq, IN_COLS)
    q_blk = SB_QTILES * KEY_TILE
    assert seq % q_blk == 0
    return pl.pallas_call(
        _sb_kernel,
        grid=(batch, seq // q_blk),
        in_specs=[
            pl.BlockSpec((None, q_blk, SB_WIDTH), lambda b, i: (b, i, T_SB_Q)),
            pl.BlockSpec((None, seq, SB_WIDTH), lambda b, i: (b, 0, T_SB_K)),
            pl.BlockSpec((None, seq, SB_WIDTH), lambda b, i: (b, 0, T_SB_V)),
            pl.BlockSpec((None, q_blk, SB_WIDTH), lambda b, i: (b, i, T_SB_GATE)),
            pl.BlockSpec((META_ROWS, SB_WIDTH), lambda b, i: (0, T_SB_K)),
            pl.BlockSpec((META_ROWS, SB_WIDTH), lambda b, i: (0, T_SB_V)),
            pl.BlockSpec((2 * KEY_TILE, 2 * KEY_TILE), lambda b, i: (0, 0)),
        ],
        out_specs=pl.BlockSpec((None, q_blk, SB_WIDTH), lambda b, i: (b, i, 0)),
        out_shape=jax.ShapeDtypeStruct((batch, seq, SB_WIDTH), BF16),
        scratch_shapes=[pltpu.VMEM((SB_QTILES * SB_PAIRS, 2 * KEY_TILE, LANES), F32),
                        pltpu.VMEM((SB_QTILES * SB_PAIRS, 2 * KEY_TILE, LANES), F32)],
        compiler_params=pltpu.CompilerParams(
            dimension_semantics=("parallel", "arbitrary"),
            vmem_limit_bytes=VMEM_LIMIT_BYTES),
        name="stick_breaking",
    )(proj3, proj3, proj3, proj3, proj_meta, proj_meta, uw)


def _hg_decay(f_logit, lbv, sum_w, pad_rows):
    f = lbv + (1.0 - lbv) * _sigmoid(f_logit)
    if pad_rows:
        row = lax.broadcasted_iota(jnp.int32, f.shape, 0)
        f = jnp.where(row >= pad_rows, f, 1.0)
    g_hi, g_lo = _split_bf16(jnp.log(f))
    sums = jnp.dot(sum_w, jnp.concatenate([g_hi, g_lo], axis=0), preferred_element_type=F32)
    return sums, 1.0 - f


def _hg_next_state(state_t, b, kk, v):
    b_last = b[HG_CHUNK - 1:HG_CHUNK, :]
    k_dec = (kk * jnp.exp(b_last - b)).astype(BF16)
    return state_t * jnp.exp(b_last) + lax.dot_general(v, k_dec, _TN, preferred_element_type=F32)


def _hg_scores(qf, kk, sums, masks):
    a = jnp.where(masks[0], lax.dot_general(qf.astype(BF16), kk.astype(BF16), _NT,
                                            preferred_element_type=F32), 0.0)
    for lvl in range(len(HG_LEVELS)):
        e = jnp.exp(sums[(1 + lvl) * HG_CHUNK:(2 + lvl) * HG_CHUNK])
        part = lax.dot_general((qf * e).astype(BF16), (kk * e).astype(BF16), _NT,
                               preferred_element_type=F32)
        a = jnp.where(masks[1 + lvl], part, a)
    return a


def _hg_score_masks():
    row = lax.broadcasted_iota(jnp.int32, (HG_CHUNK, HG_CHUNK), 0)
    col = lax.broadcasted_iota(jnp.int32, (HG_CHUNK, HG_CHUNK), 1)
    return [row == col] + [
        ((row // (2 * m)) == (col // (2 * m))) & ((row // m) % 2 == 1) & ((col // m) % 2 == 0)
        for m in HG_LEVELS]


def _hg_kernel(q_ref, f_ref, v_ref, gate_ref, mf_ref, mv_ref, lbl_ref, gn_ref,
               sumw_ref, o_ref, state_ref, *, chunks_per_tile):
    t = pl.program_id(1)
    sum_w = sumw_ref[...]
    lbl = lbl_ref[...]
    ex = jnp.exp(lbl - jnp.max(lbl, axis=0, keepdims=True))
    lb = ex[0:1, :] / jnp.sum(ex, axis=0, keepdims=True)
    gn = gn_ref[...]
    head_lanes = [slice(h * HG_DIM, (h + 1) * HG_DIM) for h in range(HG_HEADS)]

    @pl.when(t == 0)
    def _():
        for h, sl in enumerate(head_lanes):
            b, kk = _hg_decay(mf_ref[:, sl], lb[:, sl], sum_w[:HG_CHUNK], HG_CHUNK - N_META)
            state_ref[h] = _hg_next_state(jnp.zeros((HG_DIM, HG_DIM), F32), b, kk, mv_ref[:, sl])

    masks = _hg_score_masks()

    def body(c, carry):
        base = c * (HG_UNROLL * HG_CHUNK)
        rows = [pl.ds(pl.multiple_of(base + u * HG_CHUNK, HG_CHUNK), HG_CHUNK)
                for u in range(HG_UNROLL)]
        decay = [[_hg_decay(f_ref[r, sl], lb[:, sl], sum_w, 0) for sl in head_lanes] for r in rows]
        states = [state_ref[h] for h in range(HG_HEADS)]
        scores, inter = [], []
        for u, r in enumerate(rows):
            for h, sl in enumerate(head_lanes):
                sums, kk = decay[u][h]
                b = sums[:HG_CHUNK]
                q = q_ref[r, sl].astype(F32)
                qf = q * _sigmoid(q) * HG_SCALE
                scores.append(_hg_scores(qf, kk, sums, masks))
                inter.append(lax.dot_general((qf * jnp.exp(b)).astype(BF16),
                                             states[h].astype(BF16), _NT,
                                             preferred_element_type=F32))
                states[h] = _hg_next_state(states[h], b, kk, v_ref[r, sl])
        for h in range(HG_HEADS):
            state_ref[h] = states[h]
        for u, r in enumerate(rows):
            for h, sl in enumerate(head_lanes):
                i = u * HG_HEADS + h
                o = inter[i] + jnp.dot(scores[i].astype(BF16), v_ref[r, sl],
                                       preferred_element_type=F32)
                o = o * lax.rsqrt(jnp.mean(o * o, axis=-1, keepdims=True) + EPS) * gn[:, sl]
                gate = gate_ref[r, sl].astype(F32)
                o_ref[r, sl] = (o * (gate * _sigmoid(gate))).astype(BF16)
        return carry

    lax.fori_loop(0, chunks_per_tile // HG_UNROLL, body, 0)


def _hgrn2(proj, f_proj, proj_meta, f_meta, lb_logits, hg_norm_g, sum_w, batch, seq, row_tile):
    assert seq % row_tile == 0 and row_tile % HG_CHUNK == 0
    proj3 = proj.reshape(batch, seq, IN_COLS)
    f3 = f_proj.reshape(batch, seq, HG_WIDTH)
    meta_blk = META_ROWS // HG_CHUNK - 1
    const = lambda b, t: (0, 0)
    return pl.pallas_call(
        functools.partial(_hg_kernel, chunks_per_tile=row_tile // HG_CHUNK),
        grid=(batch, seq // row_tile),
        in_specs=[
            pl.BlockSpec((None, row_tile, HG_WIDTH), lambda b, t: (b, t, T_HG_Q)),
            pl.BlockSpec((None, row_tile, HG_WIDTH), lambda b, t: (b, t, 0)),
            pl.BlockSpec((None, row_tile, HG_WIDTH), lambda b, t: (b, t, T_HG_I)),
            pl.BlockSpec((None, row_tile, HG_WIDTH), lambda b, t: (b, t, T_HG_GATE)),
            pl.BlockSpec((HG_CHUNK, HG_WIDTH), lambda b, t: (meta_blk, 0)),
            pl.BlockSpec((HG_CHUNK, HG_WIDTH), lambda b, t: (meta_blk, T_HG_I)),
            pl.BlockSpec(lb_logits.shape, const),
            pl.BlockSpec((1, HG_WIDTH), const),
            pl.BlockSpec(sum_w.shape, const),
        ],
        out_specs=pl.BlockSpec((None, row_tile, HG_WIDTH), lambda b, t: (b, t, 0)),
        out_shape=jax.ShapeDtypeStruct((batch, seq, HG_WIDTH), BF16),
        scratch_shapes=[pltpu.VMEM((HG_HEADS, HG_DIM, HG_DIM), F32)],
        compiler_params=pltpu.CompilerParams(
            dimension_semantics=("parallel", "arbitrary"),
            vmem_limit_bytes=VMEM_LIMIT_BYTES),
        name="hgrn2",
    )(proj3, f3, proj3, proj3, f_meta, proj_meta, lb_logits, hg_norm_g, sum_w)


MERGE_SLAB = 256


def _final_kernel(ysb_ref, yhg_ref, gsb_ref, ghg_ref, x_ref, wsb_ref, whg_ref, wout_ref, fg_ref,
                  o_ref, wsb_s, whg_s, wout_s):
    @pl.when(pl.program_id(0) == 0)
    def _():
        wsb_s[...] = wsb_ref[...].astype(BF16)
        whg_s[...] = whg_ref[...].astype(BF16)
        wout_s[...] = wout_ref[...].astype(BF16)

    n_slab = o_ref.shape[0] // MERGE_SLAB
    slabs = [pl.ds(s * MERGE_SLAB, MERGE_SLAB) for s in range(n_slab)]
    a = [jnp.dot(ysb_ref[sl, :], wsb_s[...], preferred_element_type=F32) for sl in slabs]
    b = [jnp.dot(yhg_ref[sl, :], whg_s[...], preferred_element_type=F32) for sl in slabs]
    merged = [(_sigmoid(gsb_ref[sl, :].astype(F32)) * a[s]
               + _sigmoid(ghg_ref[sl, :].astype(F32)) * b[s]).astype(BF16)
              for s, sl in enumerate(slabs)]
    h = [x_ref[sl, :] + jnp.dot(merged[s], wout_s[...], preferred_element_type=F32)
         for s, sl in enumerate(slabs)]
    for s, sl in enumerate(slabs):
        ms = jnp.mean(h[s] * h[s], axis=-1, keepdims=True)
        o_ref[sl, :] = h[s] * lax.rsqrt(ms + EPS) * fg_ref[...]


def _final(y_sb, y_hg, proj, x2d, w_sb_out, w_hg_out, w_out, final_norm_g, row_tile):
    rows = x2d.shape[0]
    assert rows % row_tile == 0
    gate_blk = (4 * SB_WIDTH + 4 * HG_WIDTH) // D_MODEL
    const = lambda r: (0, 0)
    return pl.pallas_call(
        _final_kernel,
        grid=(rows // row_tile,),
        in_specs=[
            pl.BlockSpec((row_tile, SB_WIDTH), lambda r: (r, 0)),
            pl.BlockSpec((row_tile, HG_WIDTH), lambda r: (r, 0)),
            pl.BlockSpec((row_tile, D_MODEL), lambda r: (r, gate_blk)),
            pl.BlockSpec((row_tile, D_MODEL), lambda r: (r, gate_blk + 1)),
            pl.BlockSpec((row_tile, D_MODEL), lambda r: (r, 0)),
            pl.BlockSpec((SB_WIDTH, D_MODEL), const),
            pl.BlockSpec((HG_WIDTH, D_MODEL), const),
            pl.BlockSpec((D_MODEL, D_MODEL), const),
            pl.BlockSpec((1, D_MODEL), const),
        ],
        out_specs=pl.BlockSpec((row_tile, D_MODEL), lambda r: (r, 0)),
        out_shape=jax.ShapeDtypeStruct((rows, D_MODEL), F32),
        scratch_shapes=[pltpu.VMEM((SB_WIDTH, D_MODEL), BF16),
                        pltpu.VMEM((HG_WIDTH, D_MODEL), BF16),
                        pltpu.VMEM((D_MODEL, D_MODEL), BF16)],
        compiler_params=pltpu.CompilerParams(
            dimension_semantics=("arbitrary",),
            vmem_limit_bytes=VMEM_LIMIT_BYTES),
        name="merge_out",
    )(y_sb, y_hg, proj, proj, x2d, w_sb_out, w_hg_out, w_out, final_norm_g)


def _suffix_sum_weights():
    j = np.arange(2 * KEY_TILE)[:, None] % KEY_TILE
    s = np.arange(2 * KEY_TILE)[None, :]
    return jnp.asarray(np.where(s < KEY_TILE, j > s, True), BF16)


def _hg_sum_weights():
    r = np.arange(HG_CHUNK)[:, None]
    c = np.arange(HG_CHUNK)[None, :]
    groups = [c <= r]
    for m in HG_LEVELS:
        ref = (r // (2 * m)) * 2 * m + m - 1
        groups.append(np.where(r > ref, (c > ref) & (c <= r), (c > r) & (c <= ref)))
    w = np.concatenate(groups, axis=0)
    return jnp.asarray(np.concatenate([w, w], axis=1), BF16)


def kernel(x, meta, norm_g, w_in, w_sb_out, w_hg_out, w_out, hg_norm_g, hg_lb_logits, final_norm_g):
    batch, seq, d = x.shape
    assert d == D_MODEL and meta.shape == (N_META, D_MODEL)
    assert norm_g.shape[0] == 1 and w_in.shape == (1, D_MODEL, IN_COLS)
    assert seq % 1024 == 0

    x2d = x.reshape(batch * seq, D_MODEL)
    meta_blk = jnp.concatenate(
        [jnp.zeros((META_ROWS - N_META, D_MODEL), x.dtype), meta.astype(x.dtype)], axis=0)
    w_in2 = w_in.reshape(D_MODEL, IN_COLS)

    proj, f_proj = _inproj(x2d, norm_g, w_in2, 2048)
    proj_meta, f_meta = _inproj(meta_blk, norm_g, w_in2, META_ROWS)

    y_sb = _stick_breaking(proj, proj_meta, _suffix_sum_weights(), batch, seq)
    y_hg = _hgrn2(proj, f_proj, proj_meta, f_meta, hg_lb_logits, hg_norm_g,
                  _hg_sum_weights(), batch, seq, 1024)

    out = _final(y_sb.reshape(batch * seq, SB_WIDTH), y_hg.reshape(batch * seq, HG_WIDTH), proj,
                 x2d, w_sb_out.reshape(SB_WIDTH, D_MODEL), w_hg_out.reshape(HG_WIDTH, D_MODEL),
                 w_out.reshape(D_MODEL, D_MODEL), final_norm_g.reshape(1, D_MODEL), 1024)
    return out.reshape(batch, seq, D_MODEL)
```

```python
import functools

import jax
import jax.numpy as jnp
import numpy as np
from jax import lax
from jax.experimental import pallas as pl
from jax.experimental.pallas import tpu as pltpu

F32 = jnp.float32
BF16 = jnp.bfloat16

D_MODEL = 1024
N_META = 16
SB_HEADS = 8
SB_HEAD_DIM = 64
SB_WIDTH = SB_HEADS * SB_HEAD_DIM
SB_SCALE = SB_HEAD_DIM ** -0.5
HG_HEADS = 4
HG_DIM = 128
HG_WIDTH = HG_HEADS * HG_DIM
HG_SCALE = HG_DIM ** -0.5
IN_COLS = 4 * SB_WIDTH + 4 * HG_WIDTH + 2 * D_MODEL
EPS = 1e-6

LANES = 128
VMEM_LIMIT_BYTES = 56 * 1024 * 1024

COL_TILE = 512
T_SB_Q, T_SB_K, T_SB_V, T_SB_GATE, T_HG_Q, T_HG_F, T_HG_I, T_HG_GATE = range(8)
N_COL_TILES = IN_COLS // COL_TILE

KEY_TILE = 128
META_ROWS = 128
HG_CHUNK = 64
HG_LEVELS = (1, 2, 4, 8, 16, 32)
HG_UNROLL = 4


_NT = (((1,), (1,)), ((), ()))
_TN = (((0,), (0,)), ((), ()))


def _sigmoid(x):
    return 1.0 / (1.0 + jnp.exp(-x))


def _split_bf16(x):
    hi = x.astype(BF16)
    lo = (x - hi.astype(F32)).astype(BF16)
    return hi, lo


INPROJ_COLS = 2 * COL_TILE
assert (T_SB_Q * COL_TILE) % INPROJ_COLS == 0


def _inproj_kernel(x_ref, g_ref, w_ref, o_ref, f_ref, u_ref):
    n = pl.program_id(1)

    @pl.when(n == 0)
    def _():
        x = x_ref[...]
        ms = jnp.mean(x * x, axis=-1, keepdims=True)
        u_ref[...] = (x * lax.rsqrt(ms + EPS) * g_ref[...]).astype(BF16)

    lane = lax.broadcasted_iota(jnp.int32, (1, INPROJ_COLS), 1)
    is_q = jnp.logical_and(n == (T_SB_Q * COL_TILE) // INPROJ_COLS, lane < SB_WIDTH)
    scale = jnp.where(is_q, SB_SCALE, 1.0).astype(F32)
    p = jnp.dot(u_ref[...], (w_ref[...] * scale).astype(BF16), preferred_element_type=F32)
    o_ref[...] = p.astype(BF16)

    f_col = (T_HG_F * COL_TILE) % INPROJ_COLS

    @pl.when(n == (T_HG_F * COL_TILE) // INPROJ_COLS)
    def _():
        f_ref[...] = p[:, f_col:f_col + HG_WIDTH]


def _inproj(x2d, norm_g, w_in, row_tile, n_cols=IN_COLS):
    rows = x2d.shape[0]
    assert rows % row_tile == 0 and n_cols % INPROJ_COLS == 0 and n_cols >= (T_HG_F + 1) * COL_TILE
    return pl.pallas_call(
        _inproj_kernel,
        grid=(rows // row_tile, n_cols // INPROJ_COLS),
        in_specs=[
            pl.BlockSpec((row_tile, D_MODEL), lambda r, n: (r, 0)),
            pl.BlockSpec((1, D_MODEL), lambda r, n: (0, 0)),
            pl.BlockSpec((D_MODEL, INPROJ_COLS), lambda r, n: (0, n)),
        ],
        out_specs=[
            pl.BlockSpec((row_tile, INPROJ_COLS), lambda r, n: (r, n)),
            pl.BlockSpec((row_tile, HG_WIDTH), lambda r, n: (r, 0)),
        ],
        out_shape=[
            jax.ShapeDtypeStruct((rows, n_cols), BF16),
            jax.ShapeDtypeStruct((rows, HG_WIDTH), F32),
        ],
        scratch_shapes=[pltpu.VMEM((row_tile, D_MODEL), BF16)],
        compiler_params=pltpu.CompilerParams(
            dimension_semantics=("parallel", "arbitrary"),
            vmem_limit_bytes=VMEM_LIMIT_BYTES),
        name="inproj",
    )(x2d, norm_g, w_in)


SB_PAIRS = SB_WIDTH // LANES
SB_UNDERFLOW = -104.0
SB_QTILES = 4
SB_EARLY = 48


def _sb_kernel(q_ref, k_ref, v_ref, g_ref, mk_ref, mv_ref, uw_ref, o_ref, acc_ref, carry_ref):
    j = pl.program_id(1)
    lane = lax.broadcasted_iota(jnp.int32, (KEY_TILE, LANES), 1)
    first = lane < SB_HEAD_DIM
    uw = uw_ref[...]
    pair_lanes = [slice(p * LANES, (p + 1) * LANES) for p in range(SB_PAIRS)]
    q_rows = [slice(t * KEY_TILE, (t + 1) * KEY_TILE) for t in range(SB_QTILES)]
    q2 = []
    for qr in q_rows:
        for sl in pair_lanes:
            q = q_ref[qr, sl]
            zero = jnp.zeros_like(q)
            q2.append(jnp.concatenate([jnp.where(first, q, zero), jnp.where(first, zero, q)], axis=0))

    def early(x):
        return jnp.concatenate([x[:SB_EARLY], x[KEY_TILE:KEY_TILE + SB_EARLY]], axis=0)

    def with_early(x, e, combine):
        return jnp.concatenate([combine(x[:SB_EARLY], e[:SB_EARLY]), x[SB_EARLY:KEY_TILE],
                                combine(x[KEY_TILE:KEY_TILE + SB_EARLY], e[SB_EARLY:]),
                                x[KEY_TILE + SB_EARLY:]], axis=0)

    def log_terms(z, valid):
        log_beta = jnp.minimum(z, 0.0) - jnp.log(1.0 + jnp.exp(-jnp.abs(z)))
        log_1mb = log_beta - z
        if valid is not None:
            log_1mb = jnp.where(valid, log_1mb, 0.0)
        hi, lo = _split_bf16(log_1mb)
        return log_beta, jnp.concatenate([hi, lo], axis=1)

    def sweep(kt_ref, vt_ref, jobs, first_pass=False):
        chains = [(t * SB_PAIRS + p, sl) + job
                  for t, job in enumerate(jobs) for p, sl in enumerate(pair_lanes)]
        zs = [lax.dot_general(q2[c], kt_ref[rows, sl], _NT, preferred_element_type=F32)
              for c, sl, rows, _, _ in chains]
        zes = [None if erows is None else
               lax.dot_general(early(q2[c]), kt_ref[erows, sl], _NT, preferred_element_type=F32)
               for c, sl, _, _, erows in chains]
        log_betas, sums = [], []
        for z, ze, (_, _, _, valids, _) in zip(zs, zes, chains):
            terms = [] if ze is None else [log_terms(ze, None)]
            terms += [log_terms(z[:, n * KEY_TILE:(n + 1) * KEY_TILE], valid)
                      for n, valid in enumerate(valids)]
            sums.append(jnp.dot(jnp.concatenate([hl for _, hl in terms], axis=0), uw,
                                preferred_element_type=F32))
            log_betas.append([lb for lb, _ in terms])
        ws, wes = [], []
        for idx, (c, _, _, valids, erows) in enumerate(chains):
            off = 0 if erows is None else 2 * SB_EARLY
            lbs = log_betas[idx] if erows is None else log_betas[idx][1:]
            carry = None if first_pass else carry_ref[c]
            w_n = [None] * len(valids)
            for n in reversed(range(len(valids))):
                cs = sums[idx][off + n * 2 * KEY_TILE:off + (n + 1) * 2 * KEY_TILE]
                log_w = lbs[n] + cs[:, :KEY_TILE]
                w = jnp.exp(log_w if carry is None else log_w + carry)
                if valids[n] is not None:
                    w = jnp.where(valids[n], w, 0.0)
                w_n[n] = w.astype(BF16)
                carry = cs[:, KEY_TILE:] if carry is None else carry + cs[:, KEY_TILE:]
            if erows is None:
                wes.append(None)
            else:
                cs = sums[idx][:off]
                wes.append(jnp.exp(log_betas[idx][0] + cs[:, :KEY_TILE] + early(carry)).astype(BF16))
                carry = with_early(carry, cs[:, KEY_TILE:], jnp.add)
            carry_ref[c] = carry
            ws.append(jnp.concatenate(w_n, axis=1))
        for idx, (c, sl, rows, _, erows) in enumerate(chains):
            pv = jnp.dot(ws[idx], vt_ref[rows, sl], preferred_element_type=F32)
            if erows is not None:
                pv = with_early(pv, jnp.dot(wes[idx], vt_ref[erows, sl], preferred_element_type=F32),
                                jnp.add)
            if first_pass:
                acc_ref[c] = pv
            else:
                acc_ref[c] += pv

    row2 = lax.broadcasted_iota(jnp.int32, (2 * KEY_TILE, KEY_TILE), 0) % KEY_TILE
    col2 = lax.broadcasted_iota(jnp.int32, (2 * KEY_TILE, KEY_TILE), 1)
    causal = col2 < row2

    def key_rows(tile, n_tiles):
        return pl.ds(pl.multiple_of(tile * KEY_TILE, KEY_TILE), n_tiles * KEY_TILE)

    i_first = j * SB_QTILES
    fused = j >= 1
    assert SB_QTILES >= 2

    @pl.when(fused)
    def _():
        sweep(k_ref, v_ref, [(key_rows(i_first + t - 1, 2), [None, causal],
                              key_rows(i_first + t - 2, 1)) for t in range(SB_QTILES)],
              first_pass=True)

    @pl.when(jnp.logical_not(fused))
    def _():
        sweep(k_ref, v_ref, [(key_rows(0, t + 1), [None] * t + [causal], None)
                             for t in range(SB_QTILES)], first_pass=True)

    def live_rows():
        top = jnp.max(carry_ref[...].reshape(-1, 8, LANES), axis=0)
        return (jnp.max(top) > SB_UNDERFLOW).astype(jnp.int32)

    left = [jnp.where(fused, i_first + t - 2, 0) for t in range(SB_QTILES)]
    late = row2 >= SB_EARLY

    def cond(c):
        it, live = c
        return jnp.logical_and(it < left[-1], live > 0)

    def body(c):
        it, _ = c
        catch_up = it < 0
        jobs = []
        for t in range(SB_QTILES):
            has = jnp.logical_or(catch_up, it < left[t])
            tile = jnp.where(catch_up, left[t], jnp.maximum(left[t] - 1 - it, 0))
            valid = jnp.logical_and(has, jnp.logical_or(late, jnp.logical_not(catch_up)))
            jobs.append((key_rows(tile, 1), [valid], None))
        sweep(k_ref, v_ref, jobs)
        return it + 1, live_rows()

    start = jnp.where(fused, -1, 0).astype(jnp.int32)
    _, live = lax.while_loop(cond, body, (start, live_rows()))

    @pl.when(live > 0)
    def _():
        sweep(mk_ref, mv_ref, [(slice(None), [col2 >= META_ROWS - N_META], None)] * SB_QTILES)

    for t, qr in enumerate(q_rows):
        for p, sl in enumerate(pair_lanes):
            acc = acc_ref[t * SB_PAIRS + p]
            out = jnp.where(first, acc[:KEY_TILE], acc[KEY_TILE:])
            gate = g_ref[qr, sl].astype(F32)
            o_ref[qr, sl] = (out * (gate * _sigmoid(gate))).astype(BF16)


def _stick_breaking(proj, proj_meta, uw, batch, seq):
    assert SB_WIDTH == COL_TILE
    proj3 = proj.reshape(batch, seq, IN_COLS)
    q_blk = SB_QTILES * KEY_TILE
    assert seq % q_blk == 0
    return pl.pallas_call(
        _sb_kernel,
        grid=(batch, seq // q_blk),
        in_specs=[
            pl.BlockSpec((None, q_blk, SB_WIDTH), lambda b, i: (b, i, T_SB_Q)),
            pl.BlockSpec((None, seq, SB_WIDTH), lambda b, i: (b, 0, T_SB_K)),
            pl.BlockSpec((None, seq, SB_WIDTH), lambda b, i: (b, 0, T_SB_V)),
            pl.BlockSpec((None, q_blk, SB_WIDTH), lambda b, i: (b, i, T_SB_GATE)),
            pl.BlockSpec((META_ROWS, SB_WIDTH), lambda b, i: (0, T_SB_K)),
            pl.BlockSpec((META_ROWS, SB_WIDTH), lambda b, i: (0, T_SB_V)),
            pl.BlockSpec((2 * KEY_TILE, 2 * KEY_TILE), lambda b, i: (0, 0)),
        ],
        out_specs=pl.BlockSpec((None, q_blk, SB_WIDTH), lambda b, i: (b, i, 0)),
        out_shape=jax.ShapeDtypeStruct((batch, seq, SB_WIDTH), BF16),
        scratch_shapes=[pltpu.VMEM((SB_QTILES * SB_PAIRS, 2 * KEY_TILE, LANES), F32),
                        pltpu.VMEM((SB_QTILES * SB_PAIRS, 2 * KEY_TILE, LANES), F32)],
        compiler_params=pltpu.CompilerParams(
            dimension_semantics=("parallel", "arbitrary"),
            vmem_limit_bytes=VMEM_LIMIT_BYTES),
        name="stick_breaking",
    )(proj3, proj3, proj3, proj3, proj_meta, proj_meta, uw)


def _hg_decay(f_logit, lbv, sum_w, pad_rows):
    f = lbv + (1.0 - lbv) * _sigmoid(f_logit)
    if pad_rows:
        row = lax.broadcasted_iota(jnp.int32, f.shape, 0)
        f = jnp.where(row >= pad_rows, f, 1.0)
    g_hi, g_lo = _split_bf16(jnp.log(f))
    sums = jnp.dot(sum_w, jnp.concatenate([g_hi, g_lo], axis=0), preferred_element_type=F32)
    return sums, 1.0 - f


def _hg_next_state(state_t, b, kk, v):
    b_last = b[HG_CHUNK - 1:HG_CHUNK, :]
    k_dec = (kk * jnp.exp(b_last - b)).astype(BF16)
    return state_t * jnp.exp(b_last) + lax.dot_general(v, k_dec, _TN, preferred_element_type=F32)


def _hg_scores(qf, kk, sums, masks):
    a = jnp.where(masks[0], lax.dot_general(qf.astype(BF16), kk.astype(BF16), _NT,
                                            preferred_element_type=F32), 0.0)
    for lvl in range(len(HG_LEVELS)):
        e = jnp.exp(sums[(1 + lvl) * HG_CHUNK:(2 + lvl) * HG_CHUNK])
        part = lax.dot_general((qf * e).astype(BF16), (kk * e).astype(BF16), _NT,
                               preferred_element_type=F32)
        a = jnp.where(masks[1 + lvl], part, a)
    return a


def _hg_score_masks():
    row = lax.broadcasted_iota(jnp.int32, (HG_CHUNK, HG_CHUNK), 0)
    col = lax.broadcasted_iota(jnp.int32, (HG_CHUNK, HG_CHUNK), 1)
    return [row == col] + [
        ((row // (2 * m)) == (col // (2 * m))) & ((row // m) % 2 == 1) & ((col // m) % 2 == 0)
        for m in HG_LEVELS]


def _hg_kernel(q_ref, f_ref, v_ref, gate_ref, mf_ref, mv_ref, lbl_ref, gn_ref,
               sumw_ref, o_ref, state_ref, *, chunks_per_tile):
    t = pl.program_id(1)
    sum_w = sumw_ref[...]
    lbl = lbl_ref[...]
    ex = jnp.exp(lbl - jnp.max(lbl, axis=0, keepdims=True))
    lb = ex[0:1, :] / jnp.sum(ex, axis=0, keepdims=True)
    gn = gn_ref[...]
    head_lanes = [slice(h * HG_DIM, (h + 1) * HG_DIM) for h in range(HG_HEADS)]

    @pl.when(t == 0)
    def _():
        for h, sl in enumerate(head_lanes):
            b, kk = _hg_decay(mf_ref[:, sl], lb[:, sl], sum_w[:HG_CHUNK], HG_CHUNK - N_META)
            state_ref[h] = _hg_next_state(jnp.zeros((HG_DIM, HG_DIM), F32), b, kk, mv_ref[:, sl])

    masks = _hg_score_masks()

    def body(c, carry):
        base = c * (HG_UNROLL * HG_CHUNK)
        rows = [pl.ds(pl.multiple_of(base + u * HG_CHUNK, HG_CHUNK), HG_CHUNK)
                for u in range(HG_UNROLL)]
        decay = [[_hg_decay(f_ref[r, sl], lb[:, sl], sum_w, 0) for sl in head_lanes] for r in rows]
        states = [state_ref[h] for h in range(HG_HEADS)]
        scores, inter = [], []
        for u, r in enumerate(rows):
            for h, sl in enumerate(head_lanes):
                sums, kk = decay[u][h]
                b = sums[:HG_CHUNK]
                q = q_ref[r, sl].astype(F32)
                qf = q * _sigmoid(q) * HG_SCALE
                scores.append(_hg_scores(qf, kk, sums, masks))
                inter.append(lax.dot_general((qf * jnp.exp(b)).astype(BF16),
                                             states[h].astype(BF16), _NT,
                                             preferred_element_type=F32))
                states[h] = _hg_next_state(states[h], b, kk, v_ref[r, sl])
        for h in range(HG_HEADS):
            state_ref[h] = states[h]
        for u, r in enumerate(rows):
            for h, sl in enumerate(head_lanes):
                i = u * HG_HEADS + h
                o = inter[i] + jnp.dot(scores[i].astype(BF16), v_ref[r, sl],
                                       preferred_element_type=F32)
                o = o * lax.rsqrt(jnp.mean(o * o, axis=-1, keepdims=True) + EPS) * gn[:, sl]
                gate = gate_ref[r, sl].astype(F32)
                o_ref[r, sl] = (o * (gate * _sigmoid(gate))).astype(BF16)
        return carry

    lax.fori_loop(0, chunks_per_tile // HG_UNROLL, body, 0)


def _hgrn2(proj, f_proj, proj_meta, f_meta, lb_logits, hg_norm_g, sum_w, batch, seq, row_tile):
    assert seq % row_tile == 0 and row_tile % HG_CHUNK == 0
    proj3 = proj.reshape(batch, seq, IN_COLS)
    f3 = f_proj.reshape(batch, seq, HG_WIDTH)
    meta_blk = META_ROWS // HG_CHUNK - 1
    const = lambda b, t: (0, 0)
    return pl.pallas_call(
        functools.partial(_hg_kernel, chunks_per_tile=row_tile // HG_CHUNK),
        grid=(batch, seq // row_tile),
        in_specs=[
            pl.BlockSpec((None, row_tile, HG_WIDTH), lambda b, t: (b, t, T_HG_Q)),
            pl.BlockSpec((None, row_tile, HG_WIDTH), lambda b, t: (b, t, 0)),
            pl.BlockSpec((None, row_tile, HG_WIDTH), lambda b, t: (b, t, T_HG_I)),
            pl.BlockSpec((None, row_tile, HG_WIDTH), lambda b, t: (b, t, T_HG_GATE)),
            pl.BlockSpec((HG_CHUNK, HG_WIDTH), lambda b, t: (meta_blk, 0)),
            pl.BlockSpec((HG_CHUNK, HG_WIDTH), lambda b, t: (meta_blk, T_HG_I)),
            pl.BlockSpec(lb_logits.shape, const),
            pl.BlockSpec((1, HG_WIDTH), const),
            pl.BlockSpec(sum_w.shape, const),
        ],
        out_specs=pl.BlockSpec((None, row_tile, HG_WIDTH), lambda b, t: (b, t, 0)),
        out_shape=jax.ShapeDtypeStruct((batch, seq, HG_WIDTH), BF16),
        scratch_shapes=[pltpu.VMEM((HG_HEADS, HG_DIM, HG_DIM), F32)],
        compiler_params=pltpu.CompilerParams(
            dimension_semantics=("parallel", "arbitrary"),
            vmem_limit_bytes=VMEM_LIMIT_BYTES),
        name="hgrn2",
    )(proj3, f3, proj3, proj3, f_meta, proj_meta, lb_logits, hg_norm_g, sum_w)


MERGE_SLAB = 256


def _final_kernel(ysb_ref, yhg_ref, gsb_ref, ghg_ref, x_ref, wsb_ref, whg_ref, wout_ref, fg_ref,
                  o_ref, wsb_s, whg_s, wout_s):
    @pl.when(pl.program_id(0) == 0)
    def _():
        wsb_s[...] = wsb_ref[...].astype(BF16)
        whg_s[...] = whg_ref[...].astype(BF16)
        wout_s[...] = wout_ref[...].astype(BF16)

    n_slab = o_ref.shape[0] // MERGE_SLAB
    slabs = [pl.ds(s * MERGE_SLAB, MERGE_SLAB) for s in range(n_slab)]
    a = [jnp.dot(ysb_ref[sl, :], wsb_s[...], preferred_element_type=F32) for sl in slabs]
    b = [jnp.dot(yhg_ref[sl, :], whg_s[...], preferred_element_type=F32) for sl in slabs]
    merged = [(_sigmoid(gsb_ref[sl, :].astype(F32)) * a[s]
               + _sigmoid(ghg_ref[sl, :].astype(F32)) * b[s]).astype(BF16)
              for s, sl in enumerate(slabs)]
    h = [x_ref[sl, :] + jnp.dot(merged[s], wout_s[...], preferred_element_type=F32)
         for s, sl in enumerate(slabs)]
    for s, sl in enumerate(slabs):
        ms = jnp.mean(h[s] * h[s], axis=-1, keepdims=True)
        o_ref[sl, :] = h[s] * lax.rsqrt(ms + EPS) * fg_ref[...]


def _final(y_sb, y_hg, proj, x2d, w_sb_out, w_hg_out, w_out, final_norm_g, row_tile):
    rows = x2d.shape[0]
    assert rows % row_tile == 0
    gate_blk = (4 * SB_WIDTH + 4 * HG_WIDTH) // D_MODEL
    const = lambda r: (0, 0)
    return pl.pallas_call(
        _final_kernel,
        grid=(rows // row_tile,),
        in_specs=[
            pl.BlockSpec((row_tile, SB_WIDTH), lambda r: (r, 0)),
            pl.BlockSpec((row_tile, HG_WIDTH), lambda r: (r, 0)),
            pl.BlockSpec((row_tile, D_MODEL), lambda r: (r, gate_blk)),
            pl.BlockSpec((row_tile, D_MODEL), lambda r: (r, gate_blk + 1)),
            pl.BlockSpec((row_tile, D_MODEL), lambda r: (r, 0)),
            pl.BlockSpec((SB_WIDTH, D_MODEL), const),
            pl.BlockSpec((HG_WIDTH, D_MODEL), const),
            pl.BlockSpec((D_MODEL, D_MODEL), const),
            pl.BlockSpec((1, D_MODEL), const),
        ],
        out_specs=pl.BlockSpec((row_tile, D_MODEL), lambda r: (r, 0)),
        out_shape=jax.ShapeDtypeStruct((rows, D_MODEL), F32),
        scratch_shapes=[pltpu.VMEM((SB_WIDTH, D_MODEL), BF16),
                        pltpu.VMEM((HG_WIDTH, D_MODEL), BF16),
                        pltpu.VMEM((D_MODEL, D_MODEL), BF16)],
        compiler_params=pltpu.CompilerParams(
            dimension_semantics=("arbitrary",),
            vmem_limit_bytes=VMEM_LIMIT_BYTES),
        name="merge_out",
    )(y_sb, y_hg, proj, proj, x2d, w_sb_out, w_hg_out, w_out, final_norm_g)


def _suffix_sum_weights():
    j = np.arange(2 * KEY_TILE)[:, None] % KEY_TILE
    s = np.arange(2 * KEY_TILE)[None, :]
    return jnp.asarray(np.where(s < KEY_TILE, j > s, True), BF16)


def _hg_sum_weights():
    r = np.arange(HG_CHUNK)[:, None]
    c = np.arange(HG_CHUNK)[None, :]
    groups = [c <= r]
    for m in HG_LEVELS:
        ref = (r // (2 * m)) * 2 * m + m - 1
        groups.append(np.where(r > ref, (c > ref) & (c <= r), (c > r) & (c <= ref)))
    w = np.concatenate(groups, axis=0)
    return jnp.asarray(np.concatenate([w, w], axis=1), BF16)


def kernel(x, meta, norm_g, w_in, w_sb_out, w_hg_out, w_out, hg_norm_g, hg_lb_logits, final_norm_g):
    batch, seq, d = x.shape
    assert d == D_MODEL and meta.shape == (N_META, D_MODEL)
    assert norm_g.shape[0] == 1 and w_in.shape == (1, D_MODEL, IN_COLS)
    assert seq % 1024 == 0

    x2d = x.reshape(batch * seq, D_MODEL)
    meta_blk = jnp.concatenate(
        [jnp.zeros((META_ROWS - N_META, D_MODEL), x.dtype), meta.astype(x.dtype)], axis=0)
    w_in2 = w_in.reshape(D_MODEL, IN_COLS)

    proj, f_proj = _inproj(x2d, norm_g, w_in2, 2048)
    proj_meta, f_meta = _inproj(meta_blk, norm_g, w_in2, META_ROWS, n_cols=(T_HG_GATE + 1) * COL_TILE)

    y_sb = _stick_breaking(proj, proj_meta, _suffix_sum_weights(), batch, seq)
    y_hg = _hgrn2(proj, f_proj, proj_meta, f_meta, hg_lb_logits, hg_norm_g,
                  _hg_sum_weights(), batch, seq, 1024)

    out = _final(y_sb.reshape(batch * seq, SB_WIDTH), y_hg.reshape(batch * seq, HG_WIDTH), proj,
                 x2d, w_sb_out.reshape(SB_WIDTH, D_MODEL), w_hg_out.reshape(HG_WIDTH, D_MODEL),
                 w_out.reshape(D_MODEL, D_MODEL), final_norm_g.reshape(1, D_MODEL), 1024)
    return out.reshape(batch, seq, D_MODEL)
```

```python
import functools

import jax
import jax.numpy as jnp
import numpy as np
from jax import lax
from jax.experimental import pallas as pl
from jax.experimental.pallas import tpu as pltpu

F32 = jnp.float32
BF16 = jnp.bfloat16

D_MODEL = 1024
N_META = 16
SB_HEADS = 8
SB_HEAD_DIM = 64
SB_WIDTH = SB_HEADS * SB_HEAD_DIM
SB_SCALE = SB_HEAD_DIM ** -0.5
HG_HEADS = 4
HG_DIM = 128
HG_WIDTH = HG_HEADS * HG_DIM
HG_SCALE = HG_DIM ** -0.5
IN_COLS = 4 * SB_WIDTH + 4 * HG_WIDTH + 2 * D_MODEL
EPS = 1e-6

LANES = 128
VMEM_LIMIT_BYTES = 56 * 1024 * 1024

COL_TILE = 512
T_SB_Q, T_SB_K, T_SB_V, T_SB_GATE, T_HG_Q, T_HG_F, T_HG_I, T_HG_GATE = range(8)
N_COL_TILES = IN_COLS // COL_TILE

KEY_TILE = 128
META_ROWS = 128
HG_CHUNK = 64
HG_LEVELS = (1, 2, 4, 8, 16, 32)
HG_UNROLL = 2


_NT = (((1,), (1,)), ((), ()))
_TN = (((0,), (0,)), ((), ()))


def _sigmoid(x):
    return 1.0 / (1.0 + jnp.exp(-x))


def _split_bf16(x):
    hi = x.astype(BF16)
    lo = (x - hi.astype(F32)).astype(BF16)
    return hi, lo


INPROJ_COLS = 2 * COL_TILE
assert (T_SB_Q * COL_TILE) % INPROJ_COLS == 0


def _inproj_kernel(x_ref, g_ref, w_ref, o_ref, f_ref, u_ref):
    n = pl.program_id(1)

    @pl.when(n == 0)
    def _():
        x = x_ref[...]
        ms = jnp.mean(x * x, axis=-1, keepdims=True)
        u_ref[...] = (x * lax.rsqrt(ms + EPS) * g_ref[...]).astype(BF16)

    lane = lax.broadcasted_iota(jnp.int32, (1, INPROJ_COLS), 1)
    is_q = jnp.logical_and(n == (T_SB_Q * COL_TILE) // INPROJ_COLS, lane < SB_WIDTH)
    scale = jnp.where(is_q, SB_SCALE, 1.0).astype(F32)
    p = jnp.dot(u_ref[...], (w_ref[...] * scale).astype(BF16), preferred_element_type=F32)
    o_ref[...] = p.astype(BF16)

    f_col = (T_HG_F * COL_TILE) % INPROJ_COLS

    @pl.when(n == (T_HG_F * COL_TILE) // INPROJ_COLS)
    def _():
        f_ref[...] = p[:, f_col:f_col + HG_WIDTH]


def _inproj(x2d, norm_g, w_in, row_tile, n_cols=IN_COLS):
    rows = x2d.shape[0]
    assert rows % row_tile == 0 and n_cols % INPROJ_COLS == 0 and n_cols >= (T_HG_F + 1) * COL_TILE
    return pl.pallas_call(
        _inproj_kernel,
        grid=(rows // row_tile, n_cols // INPROJ_COLS),
        in_specs=[
            pl.BlockSpec((row_tile, D_MODEL), lambda r, n: (r, 0)),
            pl.BlockSpec((1, D_MODEL), lambda r, n: (0, 0)),
            pl.BlockSpec((D_MODEL, INPROJ_COLS), lambda r, n: (0, n)),
        ],
        out_specs=[
            pl.BlockSpec((row_tile, INPROJ_COLS), lambda r, n: (r, n)),
            pl.BlockSpec((row_tile, HG_WIDTH), lambda r, n: (r, 0)),
        ],
        out_shape=[
            jax.ShapeDtypeStruct((rows, n_cols), BF16),
            jax.ShapeDtypeStruct((rows, HG_WIDTH), F32),
        ],
        scratch_shapes=[pltpu.VMEM((row_tile, D_MODEL), BF16)],
        compiler_params=pltpu.CompilerParams(
            dimension_semantics=("parallel", "arbitrary"),
            vmem_limit_bytes=VMEM_LIMIT_BYTES),
        name="inproj",
    )(x2d, norm_g, w_in)


SB_PAIRS = SB_WIDTH // LANES
SB_UNDERFLOW = -104.0
SB_QTILES = 4
SB_EARLY = 48


def _sb_kernel(q_ref, k_ref, v_ref, g_ref, mk_ref, mv_ref, uw_ref, o_ref, acc_ref, carry_ref):
    j = pl.program_id(1)
    lane = lax.broadcasted_iota(jnp.int32, (KEY_TILE, LANES), 1)
    first = lane < SB_HEAD_DIM
    uw = uw_ref[...]
    pair_lanes = [slice(p * LANES, (p + 1) * LANES) for p in range(SB_PAIRS)]
    q_rows = [slice(t * KEY_TILE, (t + 1) * KEY_TILE) for t in range(SB_QTILES)]
    q2 = []
    for qr in q_rows:
        for sl in pair_lanes:
            q = q_ref[qr, sl]
            zero = jnp.zeros_like(q)
            q2.append(jnp.concatenate([jnp.where(first, q, zero), jnp.where(first, zero, q)], axis=0))

    def early(x):
        return jnp.concatenate([x[:SB_EARLY], x[KEY_TILE:KEY_TILE + SB_EARLY]], axis=0)

    def with_early(x, e, combine):
        return jnp.concatenate([combine(x[:SB_EARLY], e[:SB_EARLY]), x[SB_EARLY:KEY_TILE],
                                combine(x[KEY_TILE:KEY_TILE + SB_EARLY], e[SB_EARLY:]),
                                x[KEY_TILE + SB_EARLY:]], axis=0)

    def log_terms(z, valid):
        log_beta = jnp.minimum(z, 0.0) - jnp.log(1.0 + jnp.exp(-jnp.abs(z)))
        log_1mb = log_beta - z
        if valid is not None:
            log_1mb = jnp.where(valid, log_1mb, 0.0)
        hi, lo = _split_bf16(log_1mb)
        return log_beta, jnp.concatenate([hi, lo], axis=1)

    def sweep(kt_ref, vt_ref, jobs, first_pass=False):
        chains = [(t * SB_PAIRS + p, sl) + job
                  for t, job in enumerate(jobs) for p, sl in enumerate(pair_lanes)]
        zs = [lax.dot_general(q2[c], kt_ref[rows, sl], _NT, preferred_element_type=F32)
              for c, sl, rows, _, _ in chains]
        zes = [None if erows is None else
               lax.dot_general(early(q2[c]), kt_ref[erows, sl], _NT, preferred_element_type=F32)
               for c, sl, _, _, erows in chains]
        log_betas, sums = [], []
        for z, ze, (_, _, _, valids, _) in zip(zs, zes, chains):
            terms = [] if ze is None else [log_terms(ze, None)]
            terms += [log_terms(z[:, n * KEY_TILE:(n + 1) * KEY_TILE], valid)
                      for n, valid in enumerate(valids)]
            sums.append(jnp.dot(jnp.concatenate([hl for _, hl in terms], axis=0), uw,
                                preferred_element_type=F32))
            log_betas.append([lb for lb, _ in terms])
        ws, wes = [], []
        for idx, (c, _, _, valids, erows) in enumerate(chains):
            off = 0 if erows is None else 2 * SB_EARLY
            lbs = log_betas[idx] if erows is None else log_betas[idx][1:]
            carry = None if first_pass else carry_ref[c]
            w_n = [None] * len(valids)
            for n in reversed(range(len(valids))):
                cs = sums[idx][off + n * 2 * KEY_TILE:off + (n + 1) * 2 * KEY_TILE]
                log_w = lbs[n] + cs[:, :KEY_TILE]
                w = jnp.exp(log_w if carry is None else log_w + carry)
                if valids[n] is not None:
                    w = jnp.where(valids[n], w, 0.0)
                w_n[n] = w.astype(BF16)
                carry = cs[:, KEY_TILE:] if carry is None else carry + cs[:, KEY_TILE:]
            if erows is None:
                wes.append(None)
            else:
                cs = sums[idx][:off]
                wes.append(jnp.exp(log_betas[idx][0] + cs[:, :KEY_TILE] + early(carry)).astype(BF16))
                carry = with_early(carry, cs[:, KEY_TILE:], jnp.add)
            carry_ref[c] = carry
            ws.append(jnp.concatenate(w_n, axis=1))
        for idx, (c, sl, rows, _, erows) in enumerate(chains):
            pv = jnp.dot(ws[idx], vt_ref[rows, sl], preferred_element_type=F32)
            if erows is not None:
                pv = with_early(pv, jnp.dot(wes[idx], vt_ref[erows, sl], preferred_element_type=F32),
                                jnp.add)
            if first_pass:
                acc_ref[c] = pv
            else:
                acc_ref[c] += pv

    row2 = lax.broadcasted_iota(jnp.int32, (2 * KEY_TILE, KEY_TILE), 0) % KEY_TILE
    col2 = lax.broadcasted_iota(jnp.int32, (2 * KEY_TILE, KEY_TILE), 1)
    causal = col2 < row2

    def key_rows(tile, n_tiles):
        return pl.ds(pl.multiple_of(tile * KEY_TILE, KEY_TILE), n_tiles * KEY_TILE)

    i_first = j * SB_QTILES
    fused = j >= 1
    assert SB_QTILES >= 2

    @pl.when(fused)
    def _():
        sweep(k_ref, v_ref, [(key_rows(i_first + t - 1, 2), [None, causal],
                              key_rows(i_first + t - 2, 1)) for t in range(SB_QTILES)],
              first_pass=True)

    @pl.when(jnp.logical_not(fused))
    def _():
        sweep(k_ref, v_ref, [(key_rows(0, t + 1), [None] * t + [causal], None)
                             for t in range(SB_QTILES)], first_pass=True)

    def live_rows():
        top = jnp.max(carry_ref[...].reshape(-1, 8, LANES), axis=0)
        return (jnp.max(top) > SB_UNDERFLOW).astype(jnp.int32)

    left = [jnp.where(fused, i_first + t - 2, 0) for t in range(SB_QTILES)]
    late = row2 >= SB_EARLY

    def cond(c):
        it, live = c
        return jnp.logical_and(it < left[-1], live > 0)

    def body(c):
        it, _ = c
        catch_up = it < 0
        jobs = []
        for t in range(SB_QTILES):
            has = jnp.logical_or(catch_up, it < left[t])
            tile = jnp.where(catch_up, left[t], jnp.maximum(left[t] - 1 - it, 0))
            valid = jnp.logical_and(has, jnp.logical_or(late, jnp.logical_not(catch_up)))
            jobs.append((key_rows(tile, 1), [valid], None))
        sweep(k_ref, v_ref, jobs)
        return it + 1, live_rows()

    start = jnp.where(fused, -1, 0).astype(jnp.int32)
    _, live = lax.while_loop(cond, body, (start, live_rows()))

    @pl.when(live > 0)
    def _():
        sweep(mk_ref, mv_ref, [(slice(None), [col2 >= META_ROWS - N_META], None)] * SB_QTILES)

    for t, qr in enumerate(q_rows):
        for p, sl in enumerate(pair_lanes):
            acc = acc_ref[t * SB_PAIRS + p]
            out = jnp.where(first, acc[:KEY_TILE], acc[KEY_TILE:])
            gate = g_ref[qr, sl].astype(F32)
            o_ref[qr, sl] = (out * (gate * _sigmoid(gate))).astype(BF16)


def _stick_breaking(proj, proj_meta, uw, batch, seq):
    assert SB_WIDTH == COL_TILE
    proj3 = proj.reshape(batch, seq, IN_COLS)
    q_blk = SB_QTILES * KEY_TILE
    assert seq % q_blk == 0
    return pl.pallas_call(
        _sb_kernel,
        grid=(batch, seq // q_blk),
        in_specs=[
            pl.BlockSpec((None, q_blk, SB_WIDTH), lambda b, i: (b, i, T_SB_Q)),
            pl.BlockSpec((None, seq, SB_WIDTH), lambda b, i: (b, 0, T_SB_K)),
            pl.BlockSpec((None, seq, SB_WIDTH), lambda b, i: (b, 0, T_SB_V)),
            pl.BlockSpec((None, q_blk, SB_WIDTH), lambda b, i: (b, i, T_SB_GATE)),
            pl.BlockSpec((META_ROWS, SB_WIDTH), lambda b, i: (0, T_SB_K)),
            pl.BlockSpec((META_ROWS, SB_WIDTH), lambda b, i: (0, T_SB_V)),
            pl.BlockSpec((2 * KEY_TILE, 2 * KEY_TILE), lambda b, i: (0, 0)),
        ],
        out_specs=pl.BlockSpec((None, q_blk, SB_WIDTH), lambda b, i: (b, i, 0)),
        out_shape=jax.ShapeDtypeStruct((batch, seq, SB_WIDTH), BF16),
        scratch_shapes=[pltpu.VMEM((SB_QTILES * SB_PAIRS, 2 * KEY_TILE, LANES), F32),
                        pltpu.VMEM((SB_QTILES * SB_PAIRS, 2 * KEY_TILE, LANES), F32)],
        compiler_params=pltpu.CompilerParams(
            dimension_semantics=("parallel", "arbitrary"),
            vmem_limit_bytes=VMEM_LIMIT_BYTES),
        name="stick_breaking",
    )(proj3, proj3, proj3, proj3, proj_meta, proj_meta, uw)


def _hg_decay(f_logit, lbv, sum_w, pad_rows):
    f = lbv + (1.0 - lbv) * _sigmoid(f_logit)
    if pad_rows:
        row = lax.broadcasted_iota(jnp.int32, f.shape, 0)
        f = jnp.where(row >= pad_rows, f, 1.0)
    g_hi, g_lo = _split_bf16(jnp.log(f))
    sums = jnp.dot(sum_w, jnp.concatenate([g_hi, g_lo], axis=0), preferred_element_type=F32)
    return sums, 1.0 - f


def _hg_next_state(state_t, b, kk, v):
    b_last = b[HG_CHUNK - 1:HG_CHUNK, :]
    k_dec = (kk * jnp.exp(b_last - b)).astype(BF16)
    return state_t * jnp.exp(b_last) + lax.dot_general(v, k_dec, _TN, preferred_element_type=F32)


def _hg_scores(qf, kk, sums, masks):
    a = jnp.where(masks[0], lax.dot_general(qf.astype(BF16), kk.astype(BF16), _NT,
                                            preferred_element_type=F32), 0.0)
    for lvl in range(len(HG_LEVELS)):
        e = jnp.exp(sums[(1 + lvl) * HG_CHUNK:(2 + lvl) * HG_CHUNK])
        part = lax.dot_general((qf * e).astype(BF16), (kk * e).astype(BF16), _NT,
                               preferred_element_type=F32)
        a = jnp.where(masks[1 + lvl], part, a)
    return a


def _hg_score_masks():
    row = lax.broadcasted_iota(jnp.int32, (HG_CHUNK, HG_CHUNK), 0)
    col = lax.broadcasted_iota(jnp.int32, (HG_CHUNK, HG_CHUNK), 1)
    return [row == col] + [
        ((row // (2 * m)) == (col // (2 * m))) & ((row // m) % 2 == 1) & ((col // m) % 2 == 0)
        for m in HG_LEVELS]


def _hg_kernel(q_ref, f_ref, v_ref, gate_ref, mf_ref, mv_ref, lbl_ref, gn_ref,
               sumw_ref, o_ref, state_ref, *, chunks_per_tile):
    t = pl.program_id(1)
    sum_w = sumw_ref[...]
    lbl = lbl_ref[...]
    ex = jnp.exp(lbl - jnp.max(lbl, axis=0, keepdims=True))
    lb = ex[0:1, :] / jnp.sum(ex, axis=0, keepdims=True)
    gn = gn_ref[...]
    head_lanes = [slice(h * HG_DIM, (h + 1) * HG_DIM) for h in range(HG_HEADS)]

    @pl.when(t == 0)
    def _():
        for h, sl in enumerate(head_lanes):
            b, kk = _hg_decay(mf_ref[:, sl], lb[:, sl], sum_w[:HG_CHUNK], HG_CHUNK - N_META)
            state_ref[h] = _hg_next_state(jnp.zeros((HG_DIM, HG_DIM), F32), b, kk, mv_ref[:, sl])

    masks = _hg_score_masks()

    def body(c, carry):
        base = c * (HG_UNROLL * HG_CHUNK)
        rows = [pl.ds(pl.multiple_of(base + u * HG_CHUNK, HG_CHUNK), HG_CHUNK)
                for u in range(HG_UNROLL)]
        decay = [[_hg_decay(f_ref[r, sl], lb[:, sl], sum_w, 0) for sl in head_lanes] for r in rows]
        states = [state_ref[h] for h in range(HG_HEADS)]
        scores, inter = [], []
        for u, r in enumerate(rows):
            for h, sl in enumerate(head_lanes):
                sums, kk = decay[u][h]
                b = sums[:HG_CHUNK]
                q = q_ref[r, sl].astype(F32)
                qf = q * _sigmoid(q) * HG_SCALE
                scores.append(_hg_scores(qf, kk, sums, masks))
                inter.append(lax.dot_general((qf * jnp.exp(b)).astype(BF16),
                                             states[h].astype(BF16), _NT,
                                             preferred_element_type=F32))
                states[h] = _hg_next_state(states[h], b, kk, v_ref[r, sl])
        for h in range(HG_HEADS):
            state_ref[h] = states[h]
        for u, r in enumerate(rows):
            for h, sl in enumerate(head_lanes):
                i = u * HG_HEADS + h
                o = inter[i] + jnp.dot(scores[i].astype(BF16), v_ref[r, sl],
                                       preferred_element_type=F32)
                o = o * lax.rsqrt(jnp.mean(o * o, axis=-1, keepdims=True) + EPS) * gn[:, sl]
                gate = gate_ref[r, sl].astype(F32)
                o_ref[r, sl] = (o * (gate * _sigmoid(gate))).astype(BF16)
        return carry

    lax.fori_loop(0, chunks_per_tile // HG_UNROLL, body, 0)


def _hgrn2(proj, f_proj, proj_meta, f_meta, lb_logits, hg_norm_g, sum_w, batch, seq, row_tile):
    assert seq % row_tile == 0 and row_tile % HG_CHUNK == 0
    proj3 = proj.reshape(batch, seq, IN_COLS)
    f3 = f_proj.reshape(batch, seq, HG_WIDTH)
    meta_blk = META_ROWS // HG_CHUNK - 1
    const = lambda b, t: (0, 0)
    return pl.pallas_call(
        functools.partial(_hg_kernel, chunks_per_tile=row_tile // HG_CHUNK),
        grid=(batch, seq // row_tile),
        in_specs=[
            pl.BlockSpec((None, row_tile, HG_WIDTH), lambda b, t: (b, t, T_HG_Q)),
            pl.BlockSpec((None, row_tile, HG_WIDTH), lambda b, t: (b, t, 0)),
            pl.BlockSpec((None, row_tile, HG_WIDTH), lambda b, t: (b, t, T_HG_I)),
            pl.BlockSpec((None, row_tile, HG_WIDTH), lambda b, t: (b, t, T_HG_GATE)),
            pl.BlockSpec((HG_CHUNK, HG_WIDTH), lambda b, t: (meta_blk, 0)),
            pl.BlockSpec((HG_CHUNK, HG_WIDTH), lambda b, t: (meta_blk, T_HG_I)),
            pl.BlockSpec(lb_logits.shape, const),
            pl.BlockSpec((1, HG_WIDTH), const),
            pl.BlockSpec(sum_w.shape, const),
        ],
        out_specs=pl.BlockSpec((None, row_tile, HG_WIDTH), lambda b, t: (b, t, 0)),
        out_shape=jax.ShapeDtypeStruct((batch, seq, HG_WIDTH), BF16),
        scratch_shapes=[pltpu.VMEM((HG_HEADS, HG_DIM, HG_DIM), F32)],
        compiler_params=pltpu.CompilerParams(
            dimension_semantics=("parallel", "arbitrary"),
            vmem_limit_bytes=VMEM_LIMIT_BYTES),
        name="hgrn2",
    )(proj3, f3, proj3, proj3, f_meta, proj_meta, lb_logits, hg_norm_g, sum_w)


MERGE_SLAB = 256


def _final_kernel(ysb_ref, yhg_ref, gsb_ref, ghg_ref, x_ref, wsb_ref, whg_ref, wout_ref, fg_ref,
                  o_ref, wsb_s, whg_s, wout_s):
    @pl.when(pl.program_id(0) == 0)
    def _():
        wsb_s[...] = wsb_ref[...].astype(BF16)
        whg_s[...] = whg_ref[...].astype(BF16)
        wout_s[...] = wout_ref[...].astype(BF16)

    n_slab = o_ref.shape[0] // MERGE_SLAB
    slabs = [pl.ds(s * MERGE_SLAB, MERGE_SLAB) for s in range(n_slab)]
    a = [jnp.dot(ysb_ref[sl, :], wsb_s[...], preferred_element_type=F32) for sl in slabs]
    b = [jnp.dot(yhg_ref[sl, :], whg_s[...], preferred_element_type=F32) for sl in slabs]
    merged = [(_sigmoid(gsb_ref[sl, :].astype(F32)) * a[s]
               + _sigmoid(ghg_ref[sl, :].astype(F32)) * b[s]).astype(BF16)
              for s, sl in enumerate(slabs)]
    h = [x_ref[sl, :] + jnp.dot(merged[s], wout_s[...], preferred_element_type=F32)
         for s, sl in enumerate(slabs)]
    for s, sl in enumerate(slabs):
        ms = jnp.mean(h[s] * h[s], axis=-1, keepdims=True)
        o_ref[sl, :] = h[s] * lax.rsqrt(ms + EPS) * fg_ref[...]


def _final(y_sb, y_hg, proj, x2d, w_sb_out, w_hg_out, w_out, final_norm_g, row_tile):
    rows = x2d.shape[0]
    assert rows % row_tile == 0
    gate_blk = (4 * SB_WIDTH + 4 * HG_WIDTH) // D_MODEL
    const = lambda r: (0, 0)
    return pl.pallas_call(
        _final_kernel,
        grid=(rows // row_tile,),
        in_specs=[
            pl.BlockSpec((row_tile, SB_WIDTH), lambda r: (r, 0)),
            pl.BlockSpec((row_tile, HG_WIDTH), lambda r: (r, 0)),
            pl.BlockSpec((row_tile, D_MODEL), lambda r: (r, gate_blk)),
            pl.BlockSpec((row_tile, D_MODEL), lambda r: (r, gate_blk + 1)),
            pl.BlockSpec((row_tile, D_MODEL), lambda r: (r, 0)),
            pl.BlockSpec((SB_WIDTH, D_MODEL), const),
            pl.BlockSpec((HG_WIDTH, D_MODEL), const),
            pl.BlockSpec((D_MODEL, D_MODEL), const),
            pl.BlockSpec((1, D_MODEL), const),
        ],
        out_specs=pl.BlockSpec((row_tile, D_MODEL), lambda r: (r, 0)),
        out_shape=jax.ShapeDtypeStruct((rows, D_MODEL), F32),
        scratch_shapes=[pltpu.VMEM((SB_WIDTH, D_MODEL), BF16),
                        pltpu.VMEM((HG_WIDTH, D_MODEL), BF16),
                        pltpu.VMEM((D_MODEL, D_MODEL), BF16)],
        compiler_params=pltpu.CompilerParams(
            dimension_semantics=("arbitrary",),
            vmem_limit_bytes=VMEM_LIMIT_BYTES),
        name="merge_out",
    )(y_sb, y_hg, proj, proj, x2d, w_sb_out, w_hg_out, w_out, final_norm_g)


def _suffix_sum_weights():
    j = np.arange(2 * KEY_TILE)[:, None] % KEY_TILE
    s = np.arange(2 * KEY_TILE)[None, :]
    return jnp.asarray(np.where(s < KEY_TILE, j > s, True), BF16)


def _hg_sum_weights():
    r = np.arange(HG_CHUNK)[:, None]
    c = np.arange(HG_CHUNK)[None, :]
    groups = [c <= r]
    for m in HG_LEVELS:
        ref = (r // (2 * m)) * 2 * m + m - 1
        groups.append(np.where(r > ref, (c > ref) & (c <= r), (c > r) & (c <= ref)))
    w = np.concatenate(groups, axis=0)
    return jnp.asarray(np.concatenate([w, w], axis=1), BF16)


def kernel(x, meta, norm_g, w_in, w_sb_out, w_hg_out, w_out, hg_norm_g, hg_lb_logits, final_norm_g):
    batch, seq, d = x.shape
    assert d == D_MODEL and meta.shape == (N_META, D_MODEL)
    assert norm_g.shape[0] == 1 and w_in.shape == (1, D_MODEL, IN_COLS)
    assert seq % 1024 == 0

    x2d = x.reshape(batch * seq, D_MODEL)
    meta_blk = jnp.concatenate(
        [jnp.zeros((META_ROWS - N_META, D_MODEL), x.dtype), meta.astype(x.dtype)], axis=0)
    w_in2 = w_in.reshape(D_MODEL, IN_COLS)

    proj, f_proj = _inproj(x2d, norm_g, w_in2, 2048)
    meta_cols = pl.cdiv((T_HG_GATE + 1) * COL_TILE, INPROJ_COLS) * INPROJ_COLS
    proj_meta, f_meta = _inproj(meta_blk, norm_g, w_in2, META_ROWS, n_cols=meta_cols)

    y_sb = _stick_breaking(proj, proj_meta, _suffix_sum_weights(), batch, seq)
    y_hg = _hgrn2(proj, f_proj, proj_meta, f_meta, hg_lb_logits, hg_norm_g,
                  _hg_sum_weights(), batch, seq, 2048)

    out = _final(y_sb.reshape(batch * seq, SB_WIDTH), y_hg.reshape(batch * seq, HG_WIDTH), proj,
                 x2d, w_sb_out.reshape(SB_WIDTH, D_MODEL), w_hg_out.reshape(HG_WIDTH, D_MODEL),
                 w_out.reshape(D_MODEL, D_MODEL), final_norm_g.reshape(1, D_MODEL), 1024)
    return out.reshape(batch, seq, D_MODEL)
```

```python
import functools

import jax
import jax.numpy as jnp
import numpy as np
from jax import lax
from jax.experimental import pallas as pl
from jax.experimental.pallas import tpu as pltpu

F32 = jnp.float32
BF16 = jnp.bfloat16

D_MODEL = 1024
N_META = 16
SB_HEADS = 8
SB_HEAD_DIM = 64
SB_WIDTH = SB_HEADS * SB_HEAD_DIM
SB_SCALE = SB_HEAD_DIM ** -0.5
HG_HEADS = 4
HG_DIM = 128
HG_WIDTH = HG_HEADS * HG_DIM
HG_SCALE = HG_DIM ** -0.5
IN_COLS = 4 * SB_WIDTH + 4 * HG_WIDTH + 2 * D_MODEL
EPS = 1e-6

LANES = 128
VMEM_LIMIT_BYTES = 56 * 1024 * 1024

COL_TILE = 512
T_SB_Q, T_SB_K, T_SB_V, T_SB_GATE, T_HG_Q, T_HG_F, T_HG_I, T_HG_GATE = range(8)
N_COL_TILES = IN_COLS // COL_TILE

KEY_TILE = 128
META_ROWS = 128
HG_CHUNK = 64
HG_LEVELS = (1, 2, 4, 8, 16, 32)
HG_UNROLL = 2


_NT = (((1,), (1,)), ((), ()))
_TN = (((0,), (0,)), ((), ()))


def _sigmoid(x):
    return 1.0 / (1.0 + jnp.exp(-x))


def _split_bf16(x):
    hi = x.astype(BF16)
    lo = (x - hi.astype(F32)).astype(BF16)
    return hi, lo


INPROJ_COLS = 2 * COL_TILE
assert (T_SB_Q * COL_TILE) % INPROJ_COLS == 0


def _inproj_kernel(x_ref, g_ref, w_ref, o_ref, f_ref, u_ref):
    n = pl.program_id(1)

    @pl.when(n == 0)
    def _():
        x = x_ref[...]
        ms = jnp.mean(x * x, axis=-1, keepdims=True)
        u_ref[...] = (x * lax.rsqrt(ms + EPS) * g_ref[...]).astype(BF16)

    lane = lax.broadcasted_iota(jnp.int32, (1, INPROJ_COLS), 1)
    is_q = jnp.logical_and(n == (T_SB_Q * COL_TILE) // INPROJ_COLS, lane < SB_WIDTH)
    scale = jnp.where(is_q, SB_SCALE, 1.0).astype(F32)
    p = jnp.dot(u_ref[...], (w_ref[...] * scale).astype(BF16), preferred_element_type=F32)
    o_ref[...] = p.astype(BF16)

    f_col = (T_HG_F * COL_TILE) % INPROJ_COLS

    @pl.when(n == (T_HG_F * COL_TILE) // INPROJ_COLS)
    def _():
        f_ref[...] = p[:, f_col:f_col + HG_WIDTH]


def _inproj(x2d, norm_g, w_in, row_tile, n_cols=IN_COLS):
    rows = x2d.shape[0]
    assert rows % row_tile == 0 and n_cols % INPROJ_COLS == 0 and n_cols >= (T_HG_F + 1) * COL_TILE
    return pl.pallas_call(
        _inproj_kernel,
        grid=(rows // row_tile, n_cols // INPROJ_COLS),
        in_specs=[
            pl.BlockSpec((row_tile, D_MODEL), lambda r, n: (r, 0)),
            pl.BlockSpec((1, D_MODEL), lambda r, n: (0, 0)),
            pl.BlockSpec((D_MODEL, INPROJ_COLS), lambda r, n: (0, n)),
        ],
        out_specs=[
            pl.BlockSpec((row_tile, INPROJ_COLS), lambda r, n: (r, n)),
            pl.BlockSpec((row_tile, HG_WIDTH), lambda r, n: (r, 0)),
        ],
        out_shape=[
            jax.ShapeDtypeStruct((rows, n_cols), BF16),
            jax.ShapeDtypeStruct((rows, HG_WIDTH), F32),
        ],
        scratch_shapes=[pltpu.VMEM((row_tile, D_MODEL), BF16)],
        compiler_params=pltpu.CompilerParams(
            dimension_semantics=("parallel", "arbitrary"),
            vmem_limit_bytes=VMEM_LIMIT_BYTES),
        name="inproj",
    )(x2d, norm_g, w_in)


SB_PAIRS = SB_WIDTH // LANES
SB_UNDERFLOW = -104.0
SB_QTILES = 4
SB_EARLY = 48


def _sb_kernel(q_ref, k_ref, v_ref, g_ref, mk_ref, mv_ref, uw_ref, o_ref, acc_ref, carry_ref):
    j = pl.program_id(1)
    lane = lax.broadcasted_iota(jnp.int32, (KEY_TILE, LANES), 1)
    first = lane < SB_HEAD_DIM
    uw = uw_ref[...]
    pair_lanes = [slice(p * LANES, (p + 1) * LANES) for p in range(SB_PAIRS)]
    q_rows = [slice(t * KEY_TILE, (t + 1) * KEY_TILE) for t in range(SB_QTILES)]
    q2 = []
    for qr in q_rows:
        for sl in pair_lanes:
            q = q_ref[qr, sl]
            zero = jnp.zeros_like(q)
            q2.append(jnp.concatenate([jnp.where(first, q, zero), jnp.where(first, zero, q)], axis=0))

    def early(x):
        return jnp.concatenate([x[:SB_EARLY], x[KEY_TILE:KEY_TILE + SB_EARLY]], axis=0)

    def with_early(x, e, combine):
        return jnp.concatenate([combine(x[:SB_EARLY], e[:SB_EARLY]), x[SB_EARLY:KEY_TILE],
                                combine(x[KEY_TILE:KEY_TILE + SB_EARLY], e[SB_EARLY:]),
                                x[KEY_TILE + SB_EARLY:]], axis=0)

    def log_terms(z, valid):
        log_beta = jnp.minimum(z, 0.0) - jnp.log(1.0 + jnp.exp(-jnp.abs(z)))
        log_1mb = log_beta - z
        if valid is not None:
            log_1mb = jnp.where(valid, log_1mb, 0.0)
        hi, lo = _split_bf16(log_1mb)
        return log_beta, jnp.concatenate([hi, lo], axis=1)

    def sweep(kt_ref, vt_ref, jobs, first_pass=False):
        chains = [(t * SB_PAIRS + p, sl) + job
                  for t, job in enumerate(jobs) for p, sl in enumerate(pair_lanes)]

        def scores(chain):
            c, sl, rows, _, erows = chain
            z = lax.dot_general(q2[c], kt_ref[rows, sl], _NT, preferred_element_type=F32)
            ze = None if erows is None else lax.dot_general(
                early(q2[c]), kt_ref[erows, sl], _NT, preferred_element_type=F32)
            return z, ze

        def suffix_sums(chain, z, ze):
            valids = chain[3]
            terms = [] if ze is None else [log_terms(ze, None)]
            terms += [log_terms(z[:, n * KEY_TILE:(n + 1) * KEY_TILE], valid)
                      for n, valid in enumerate(valids)]
            sums = jnp.dot(jnp.concatenate([hl for _, hl in terms], axis=0), uw,
                           preferred_element_type=F32)
            return [lb for lb, _ in terms], sums

        def weights(chain, log_betas, sums):
            c, _, _, valids, erows = chain
            off = 0 if erows is None else 2 * SB_EARLY
            lbs = log_betas if erows is None else log_betas[1:]
            carry = None if first_pass else carry_ref[c]
            w_n = [None] * len(valids)
            for n in reversed(range(len(valids))):
                cs = sums[off + n * 2 * KEY_TILE:off + (n + 1) * 2 * KEY_TILE]
                log_w = lbs[n] + cs[:, :KEY_TILE]
                w = jnp.exp(log_w if carry is None else log_w + carry)
                if valids[n] is not None:
                    w = jnp.where(valids[n], w, 0.0)
                w_n[n] = w.astype(BF16)
                carry = cs[:, KEY_TILE:] if carry is None else carry + cs[:, KEY_TILE:]
            we = None
            if erows is not None:
                cs = sums[:off]
                we = jnp.exp(log_betas[0] + cs[:, :KEY_TILE] + early(carry)).astype(BF16)
                carry = with_early(carry, cs[:, KEY_TILE:], jnp.add)
            carry_ref[c] = carry
            return jnp.concatenate(w_n, axis=1), we

        def accumulate(chain, w, we):
            c, sl, rows, _, erows = chain
            pv = jnp.dot(w, vt_ref[rows, sl], preferred_element_type=F32)
            if erows is not None:
                pv = with_early(pv, jnp.dot(we, vt_ref[erows, sl], preferred_element_type=F32),
                                jnp.add)
            if first_pass:
                acc_ref[c] = pv
            else:
                acc_ref[c] += pv

        stages = (scores, suffix_sums, weights, accumulate)
        held = [None] * len(chains)
        for k in range(len(chains) + len(stages) - 1):
            for s, stage in reversed(list(enumerate(stages))):
                i = k - s
                if 0 <= i < len(chains):
                    held[i] = stage(chains[i]) if s == 0 else stage(chains[i], *held[i])

    row2 = lax.broadcasted_iota(jnp.int32, (2 * KEY_TILE, KEY_TILE), 0) % KEY_TILE
    col2 = lax.broadcasted_iota(jnp.int32, (2 * KEY_TILE, KEY_TILE), 1)
    causal = col2 < row2

    def key_rows(tile, n_tiles):
        return pl.ds(pl.multiple_of(tile * KEY_TILE, KEY_TILE), n_tiles * KEY_TILE)

    i_first = j * SB_QTILES
    fused = j >= 1
    assert SB_QTILES >= 2

    @pl.when(fused)
    def _():
        sweep(k_ref, v_ref, [(key_rows(i_first + t - 1, 2), [None, causal],
                              key_rows(i_first + t - 2, 1)) for t in range(SB_QTILES)],
              first_pass=True)

    @pl.when(jnp.logical_not(fused))
    def _():
        sweep(k_ref, v_ref, [(key_rows(0, t + 1), [None] * t + [causal], None)
                             for t in range(SB_QTILES)], first_pass=True)

    def live_rows():
        top = jnp.max(carry_ref[...].reshape(-1, 8, LANES), axis=0)
        return (jnp.max(top) > SB_UNDERFLOW).astype(jnp.int32)

    left = [jnp.where(fused, i_first + t - 2, 0) for t in range(SB_QTILES)]
    late = row2 >= SB_EARLY

    def cond(c):
        it, live = c
        return jnp.logical_and(it < left[-1], live > 0)

    def body(c):
        it, _ = c
        catch_up = it < 0
        jobs = []
        for t in range(SB_QTILES):
            has = jnp.logical_or(catch_up, it < left[t])
            tile = jnp.where(catch_up, left[t], jnp.maximum(left[t] - 1 - it, 0))
            valid = jnp.logical_and(has, jnp.logical_or(late, jnp.logical_not(catch_up)))
            jobs.append((key_rows(tile, 1), [valid], None))
        sweep(k_ref, v_ref, jobs)
        return it + 1, live_rows()

    start = jnp.where(fused, -1, 0).astype(jnp.int32)
    _, live = lax.while_loop(cond, body, (start, live_rows()))

    @pl.when(live > 0)
    def _():
        sweep(mk_ref, mv_ref, [(slice(None), [col2 >= META_ROWS - N_META], None)] * SB_QTILES)

    for t, qr in enumerate(q_rows):
        for p, sl in enumerate(pair_lanes):
            acc = acc_ref[t * SB_PAIRS + p]
            out = jnp.where(first, acc[:KEY_TILE], acc[KEY_TILE:])
            gate = g_ref[qr, sl].astype(F32)
            o_ref[qr, sl] = (out * (gate * _sigmoid(gate))).astype(BF16)


def _stick_breaking(proj, proj_meta, uw, batch, seq):
    assert SB_WIDTH == COL_TILE
    proj3 = proj.reshape(batch, seq, IN_COLS)
    q_blk = SB_QTILES * KEY_TILE
    assert seq % q_blk == 0
    return pl.pallas_call(
        _sb_kernel,
        grid=(batch, seq // q_blk),
        in_specs=[
            pl.BlockSpec((None, q_blk, SB_WIDTH), lambda b, i: (b, i, T_SB_Q)),
            pl.BlockSpec((None, seq, SB_WIDTH), lambda b, i: (b, 0, T_SB_K)),
            pl.BlockSpec((None, seq, SB_WIDTH), lambda b, i: (b, 0, T_SB_V)),
            pl.BlockSpec((None, q_blk, SB_WIDTH), lambda b, i: (b, i, T_SB_GATE)),
            pl.BlockSpec((META_ROWS, SB_WIDTH), lambda b, i: (0, T_SB_K)),
            pl.BlockSpec((META_ROWS, SB_WIDTH), lambda b, i: (0, T_SB_V)),
            pl.BlockSpec((2 * KEY_TILE, 2 * KEY_TILE), lambda b, i: (0, 0)),
        ],
        out_specs=pl.BlockSpec((None, q_blk, SB_WIDTH), lambda b, i: (b, i, 0)),
        out_shape=jax.ShapeDtypeStruct((batch, seq, SB_WIDTH), BF16),
        scratch_shapes=[pltpu.VMEM((SB_QTILES * SB_PAIRS, 2 * KEY_TILE, LANES), F32),
                        pltpu.VMEM((SB_QTILES * SB_PAIRS, 2 * KEY_TILE, LANES), F32)],
        compiler_params=pltpu.CompilerParams(
            dimension_semantics=("parallel", "arbitrary"),
            vmem_limit_bytes=VMEM_LIMIT_BYTES),
        name="stick_breaking",
    )(proj3, proj3, proj3, proj3, proj_meta, proj_meta, uw)


def _hg_decay(f_logit, lbv, sum_w, pad_rows):
    f = lbv + (1.0 - lbv) * _sigmoid(f_logit)
    if pad_rows:
        row = lax.broadcasted_iota(jnp.int32, f.shape, 0)
        f = jnp.where(row >= pad_rows, f, 1.0)
    g_hi, g_lo = _split_bf16(jnp.log(f))
    sums = jnp.dot(sum_w, jnp.concatenate([g_hi, g_lo], axis=0), preferred_element_type=F32)
    return sums, 1.0 - f


def _hg_next_state(state_t, b, kk, v):
    b_last = b[HG_CHUNK - 1:HG_CHUNK, :]
    k_dec = (kk * jnp.exp(b_last - b)).astype(BF16)
    return state_t * jnp.exp(b_last) + lax.dot_general(v, k_dec, _TN, preferred_element_type=F32)


def _hg_scores(qf, kk, sums, masks):
    a = jnp.where(masks[0], lax.dot_general(qf.astype(BF16), kk.astype(BF16), _NT,
                                            preferred_element_type=F32), 0.0)
    for lvl in range(len(HG_LEVELS)):
        e = jnp.exp(sums[(1 + lvl) * HG_CHUNK:(2 + lvl) * HG_CHUNK])
        part = lax.dot_general((qf * e).astype(BF16), (kk * e).astype(BF16), _NT,
                               preferred_element_type=F32)
        a = jnp.where(masks[1 + lvl], part, a)
    return a


def _hg_score_masks():
    row = lax.broadcasted_iota(jnp.int32, (HG_CHUNK, HG_CHUNK), 0)
    col = lax.broadcasted_iota(jnp.int32, (HG_CHUNK, HG_CHUNK), 1)
    return [row == col] + [
        ((row // (2 * m)) == (col // (2 * m))) & ((row // m) % 2 == 1) & ((col // m) % 2 == 0)
        for m in HG_LEVELS]


def _hg_kernel(q_ref, f_ref, v_ref, gate_ref, mf_ref, mv_ref, lbl_ref, gn_ref,
               sumw_ref, o_ref, state_ref, *, chunks_per_tile):
    t = pl.program_id(1)
    sum_w = sumw_ref[...]
    lbl = lbl_ref[...]
    ex = jnp.exp(lbl - jnp.max(lbl, axis=0, keepdims=True))
    lb = ex[0:1, :] / jnp.sum(ex, axis=0, keepdims=True)
    gn = gn_ref[...]
    head_lanes = [slice(h * HG_DIM, (h + 1) * HG_DIM) for h in range(HG_HEADS)]

    @pl.when(t == 0)
    def _():
        for h, sl in enumerate(head_lanes):
            b, kk = _hg_decay(mf_ref[:, sl], lb[:, sl], sum_w[:HG_CHUNK], HG_CHUNK - N_META)
            state_ref[h] = _hg_next_state(jnp.zeros((HG_DIM, HG_DIM), F32), b, kk, mv_ref[:, sl])

    masks = _hg_score_masks()

    def body(c, carry):
        base = c * (HG_UNROLL * HG_CHUNK)
        rows = [pl.ds(pl.multiple_of(base + u * HG_CHUNK, HG_CHUNK), HG_CHUNK)
                for u in range(HG_UNROLL)]
        decay = [[_hg_decay(f_ref[r, sl], lb[:, sl], sum_w, 0) for sl in head_lanes] for r in rows]
        states = [state_ref[h] for h in range(HG_HEADS)]
        scores, inter = [], []
        for u, r in enumerate(rows):
            for h, sl in enumerate(head_lanes):
                sums, kk = decay[u][h]
                b = sums[:HG_CHUNK]
                q = q_ref[r, sl].astype(F32)
                qf = q * _sigmoid(q) * HG_SCALE
                scores.append(_hg_scores(qf, kk, sums, masks))
                inter.append(lax.dot_general((qf * jnp.exp(b)).astype(BF16),
                                             states[h].astype(BF16), _NT,
                                             preferred_element_type=F32))
                states[h] = _hg_next_state(states[h], b, kk, v_ref[r, sl])
        for h in range(HG_HEADS):
            state_ref[h] = states[h]
        for u, r in enumerate(rows):
            for h, sl in enumerate(head_lanes):
                i = u * HG_HEADS + h
                o = inter[i] + jnp.dot(scores[i].astype(BF16), v_ref[r, sl],
                                       preferred_element_type=F32)
                o = o * lax.rsqrt(jnp.mean(o * o, axis=-1, keepdims=True) + EPS) * gn[:, sl]
                gate = gate_ref[r, sl].astype(F32)
                o_ref[r, sl] = (o * (gate * _sigmoid(gate))).astype(BF16)
        return carry

    lax.fori_loop(0, chunks_per_tile // HG_UNROLL, body, 0)


def _hgrn2(proj, f_proj, proj_meta, f_meta, lb_logits, hg_norm_g, sum_w, batch, seq, row_tile):
    assert seq % row_tile == 0 and row_tile % HG_CHUNK == 0
    proj3 = proj.reshape(batch, seq, IN_COLS)
    f3 = f_proj.reshape(batch, seq, HG_WIDTH)
    meta_blk = META_ROWS // HG_CHUNK - 1
    const = lambda b, t: (0, 0)
    return pl.pallas_call(
        functools.partial(_hg_kernel, chunks_per_tile=row_tile // HG_CHUNK),
        grid=(batch, seq // row_tile),
        in_specs=[
            pl.BlockSpec((None, row_tile, HG_WIDTH), lambda b, t: (b, t, T_HG_Q)),
            pl.BlockSpec((None, row_tile, HG_WIDTH), lambda b, t: (b, t, 0)),
            pl.BlockSpec((None, row_tile, HG_WIDTH), lambda b, t: (b, t, T_HG_I)),
            pl.BlockSpec((None, row_tile, HG_WIDTH), lambda b, t: (b, t, T_HG_GATE)),
            pl.BlockSpec((HG_CHUNK, HG_WIDTH), lambda b, t: (meta_blk, 0)),
            pl.BlockSpec((HG_CHUNK, HG_WIDTH), lambda b, t: (meta_blk, T_HG_I)),
            pl.BlockSpec(lb_logits.shape, const),
            pl.BlockSpec((1, HG_WIDTH), const),
            pl.BlockSpec(sum_w.shape, const),
        ],
        out_specs=pl.BlockSpec((None, row_tile, HG_WIDTH), lambda b, t: (b, t, 0)),
        out_shape=jax.ShapeDtypeStruct((batch, seq, HG_WIDTH), BF16),
        scratch_shapes=[pltpu.VMEM((HG_HEADS, HG_DIM, HG_DIM), F32)],
        compiler_params=pltpu.CompilerParams(
            dimension_semantics=("parallel", "arbitrary"),
            vmem_limit_bytes=VMEM_LIMIT_BYTES),
        name="hgrn2",
    )(proj3, f3, proj3, proj3, f_meta, proj_meta, lb_logits, hg_norm_g, sum_w)


MERGE_SLAB = 256


def _final_kernel(ysb_ref, yhg_ref, gsb_ref, ghg_ref, x_ref, wsb_ref, whg_ref, wout_ref, fg_ref,
                  o_ref, wsb_s, whg_s, wout_s):
    @pl.when(pl.program_id(0) == 0)
    def _():
        wsb_s[...] = wsb_ref[...].astype(BF16)
        whg_s[...] = whg_ref[...].astype(BF16)
        wout_s[...] = wout_ref[...].astype(BF16)

    n_slab = o_ref.shape[0] // MERGE_SLAB
    slabs = [pl.ds(s * MERGE_SLAB, MERGE_SLAB) for s in range(n_slab)]
    a = [jnp.dot(ysb_ref[sl, :], wsb_s[...], preferred_element_type=F32) for sl in slabs]
    b = [jnp.dot(yhg_ref[sl, :], whg_s[...], preferred_element_type=F32) for sl in slabs]
    merged = [(_sigmoid(gsb_ref[sl, :].astype(F32)) * a[s]
               + _sigmoid(ghg_ref[sl, :].astype(F32)) * b[s]).astype(BF16)
              for s, sl in enumerate(slabs)]
    h = [x_ref[sl, :] + jnp.dot(merged[s], wout_s[...], preferred_element_type=F32)
         for s, sl in enumerate(slabs)]
    for s, sl in enumerate(slabs):
        ms = jnp.mean(h[s] * h[s], axis=-1, keepdims=True)
        o_ref[sl, :] = h[s] * lax.rsqrt(ms + EPS) * fg_ref[...]


def _final(y_sb, y_hg, proj, x2d, w_sb_out, w_hg_out, w_out, final_norm_g, row_tile):
    rows = x2d.shape[0]
    assert rows % row_tile == 0
    gate_blk = (4 * SB_WIDTH + 4 * HG_WIDTH) // D_MODEL
    const = lambda r: (0, 0)
    return pl.pallas_call(
        _final_kernel,
        grid=(rows // row_tile,),
        in_specs=[
            pl.BlockSpec((row_tile, SB_WIDTH), lambda r: (r, 0)),
            pl.BlockSpec((row_tile, HG_WIDTH), lambda r: (r, 0)),
            pl.BlockSpec((row_tile, D_MODEL), lambda r: (r, gate_blk)),
            pl.BlockSpec((row_tile, D_MODEL), lambda r: (r, gate_blk + 1)),
            pl.BlockSpec((row_tile, D_MODEL), lambda r: (r, 0)),
            pl.BlockSpec((SB_WIDTH, D_MODEL), const),
            pl.BlockSpec((HG_WIDTH, D_MODEL), const),
            pl.BlockSpec((D_MODEL, D_MODEL), const),
            pl.BlockSpec((1, D_MODEL), const),
        ],
        out_specs=pl.BlockSpec((row_tile, D_MODEL), lambda r: (r, 0)),
        out_shape=jax.ShapeDtypeStruct((rows, D_MODEL), F32),
        scratch_shapes=[pltpu.VMEM((SB_WIDTH, D_MODEL), BF16),
                        pltpu.VMEM((HG_WIDTH, D_MODEL), BF16),
                        pltpu.VMEM((D_MODEL, D_MODEL), BF16)],
        compiler_params=pltpu.CompilerParams(
            dimension_semantics=("arbitrary",),
            vmem_limit_bytes=VMEM_LIMIT_BYTES),
        name="merge_out",
    )(y_sb, y_hg, proj, proj, x2d, w_sb_out, w_hg_out, w_out, final_norm_g)


def _suffix_sum_weights():
    j = np.arange(2 * KEY_TILE)[:, None] % KEY_TILE
    s = np.arange(2 * KEY_TILE)[None, :]
    return jnp.asarray(np.where(s < KEY_TILE, j > s, True), BF16)


def _hg_sum_weights():
    r = np.arange(HG_CHUNK)[:, None]
    c = np.arange(HG_CHUNK)[None, :]
    groups = [c <= r]
    for m in HG_LEVELS:
        ref = (r // (2 * m)) * 2 * m + m - 1
        groups.append(np.where(r > ref, (c > ref) & (c <= r), (c > r) & (c <= ref)))
    w = np.concatenate(groups, axis=0)
    return jnp.asarray(np.concatenate([w, w], axis=1), BF16)


def kernel(x, meta, norm_g, w_in, w_sb_out, w_hg_out, w_out, hg_norm_g, hg_lb_logits, final_norm_g):
    batch, seq, d = x.shape
    assert d == D_MODEL and meta.shape == (N_META, D_MODEL)
    assert norm_g.shape[0] == 1 and w_in.shape == (1, D_MODEL, IN_COLS)
    assert seq % 1024 == 0

    x2d = x.reshape(batch * seq, D_MODEL)
    meta_blk = jnp.concatenate(
        [jnp.zeros((META_ROWS - N_META, D_MODEL), x.dtype), meta.astype(x.dtype)], axis=0)
    w_in2 = w_in.reshape(D_MODEL, IN_COLS)

    proj, f_proj = _inproj(x2d, norm_g, w_in2, 2048)
    meta_cols = pl.cdiv((T_HG_GATE + 1) * COL_TILE, INPROJ_COLS) * INPROJ_COLS
    proj_meta, f_meta = _inproj(meta_blk, norm_g, w_in2, META_ROWS, n_cols=meta_cols)

    y_sb = _stick_breaking(proj, proj_meta, _suffix_sum_weights(), batch, seq)
    y_hg = _hgrn2(proj, f_proj, proj_meta, f_meta, hg_lb_logits, hg_norm_g,
                  _hg_sum_weights(), batch, seq, 1024)

    out = _final(y_sb.reshape(batch * seq, SB_WIDTH), y_hg.reshape(batch * seq, HG_WIDTH), proj,
                 x2d, w_sb_out.reshape(SB_WIDTH, D_MODEL), w_hg_out.reshape(HG_WIDTH, D_MODEL),
                 w_out.reshape(D_MODEL, D_MODEL), final_norm_g.reshape(1, D_MODEL), 1024)
    return out.reshape(batch, seq, D_MODEL)
```

```python
import functools

import jax
import jax.numpy as jnp
import numpy as np
from jax import lax
from jax.experimental import pallas as pl
from jax.experimental.pallas import tpu as pltpu

F32 = jnp.float32
BF16 = jnp.bfloat16

D_MODEL = 1024
N_META = 16
SB_HEADS = 8
SB_HEAD_DIM = 64
SB_WIDTH = SB_HEADS * SB_HEAD_DIM
SB_SCALE = SB_HEAD_DIM ** -0.5
HG_HEADS = 4
HG_DIM = 128
HG_WIDTH = HG_HEADS * HG_DIM
HG_SCALE = HG_DIM ** -0.5
IN_COLS = 4 * SB_WIDTH + 4 * HG_WIDTH + 2 * D_MODEL
EPS = 1e-6

LANES = 128
SUBLANES = 8
VMEM_LIMIT_BYTES = 56 * 1024 * 1024

INPROJ_ROWS = 2048
HG_ROWS = 1024
MERGE_ROWS = 1024

COL_TILE = 512
T_SB_Q, T_SB_K, T_SB_V, T_SB_GATE, T_HG_Q, T_HG_F, T_HG_I, T_HG_GATE = range(8)
N_COL_TILES = IN_COLS // COL_TILE

KEY_TILE = 128
META_ROWS = 128
HG_CHUNK = 64
HG_LEVELS = (1, 2, 4, 8, 16, 32)
HG_UNROLL = 2


_NT = (((1,), (1,)), ((), ()))
_TN = (((0,), (0,)), ((), ()))


def _sigmoid(x):
    return 1.0 / (1.0 + jnp.exp(-x))


def _split_bf16(x):
    hi = x.astype(BF16)
    lo = (x - hi.astype(F32)).astype(BF16)
    return hi, lo


INPROJ_COLS = 2 * COL_TILE
assert (T_SB_Q * COL_TILE) % INPROJ_COLS == 0


def _inproj_kernel(x_ref, g_ref, w_ref, o_ref, f_ref, u_ref):
    n = pl.program_id(1)

    @pl.when(n == 0)
    def _():
        x = x_ref[...]
        ms = jnp.mean(x * x, axis=-1, keepdims=True)
        u_ref[...] = (x * lax.rsqrt(ms + EPS) * g_ref[...]).astype(BF16)

    lane = lax.broadcasted_iota(jnp.int32, (1, INPROJ_COLS), 1)
    is_q = jnp.logical_and(n == (T_SB_Q * COL_TILE) // INPROJ_COLS, lane < SB_WIDTH)
    scale = jnp.where(is_q, SB_SCALE, 1.0).astype(F32)
    p = jnp.dot(u_ref[...], (w_ref[...] * scale).astype(BF16), preferred_element_type=F32)
    o_ref[...] = p.astype(BF16)

    f_col = (T_HG_F * COL_TILE) % INPROJ_COLS

    @pl.when(n == (T_HG_F * COL_TILE) // INPROJ_COLS)
    def _():
        f_ref[...] = p[:, f_col:f_col + HG_WIDTH]


def _inproj(x2d, norm_g, w_in, row_tile, n_cols=IN_COLS):
    rows = x2d.shape[0]
    assert rows % row_tile == 0 and n_cols % INPROJ_COLS == 0 and n_cols >= (T_HG_F + 1) * COL_TILE
    return pl.pallas_call(
        _inproj_kernel,
        grid=(rows // row_tile, n_cols // INPROJ_COLS),
        in_specs=[
            pl.BlockSpec((row_tile, D_MODEL), lambda r, n: (r, 0)),
            pl.BlockSpec((1, D_MODEL), lambda r, n: (0, 0)),
            pl.BlockSpec((D_MODEL, INPROJ_COLS), lambda r, n: (0, n)),
        ],
        out_specs=[
            pl.BlockSpec((row_tile, INPROJ_COLS), lambda r, n: (r, n)),
            pl.BlockSpec((row_tile, HG_WIDTH), lambda r, n: (r, 0)),
        ],
        out_shape=[
            jax.ShapeDtypeStruct((rows, n_cols), BF16),
            jax.ShapeDtypeStruct((rows, HG_WIDTH), F32),
        ],
        scratch_shapes=[pltpu.VMEM((row_tile, D_MODEL), BF16)],
        compiler_params=pltpu.CompilerParams(
            dimension_semantics=("parallel", "arbitrary"),
            vmem_limit_bytes=VMEM_LIMIT_BYTES),
        name="inproj",
    )(x2d, norm_g, w_in)


SB_PAIRS = SB_WIDTH // LANES
SB_UNDERFLOW = -104.0
SB_QTILES = 4
SB_EARLY = 48


def _sb_kernel(q_ref, k_ref, v_ref, g_ref, mk_ref, mv_ref, uw_ref, o_ref, acc_ref, carry_ref):
    j = pl.program_id(1)
    lane = lax.broadcasted_iota(jnp.int32, (KEY_TILE, LANES), 1)
    first = lane < SB_HEAD_DIM
    uw = uw_ref[...]
    pair_lanes = [slice(p * LANES, (p + 1) * LANES) for p in range(SB_PAIRS)]
    q_rows = [slice(t * KEY_TILE, (t + 1) * KEY_TILE) for t in range(SB_QTILES)]
    q2 = []
    for qr in q_rows:
        for sl in pair_lanes:
            q = q_ref[qr, sl]
            zero = jnp.zeros_like(q)
            q2.append(jnp.concatenate([jnp.where(first, q, zero), jnp.where(first, zero, q)], axis=0))

    def early(x):
        return jnp.concatenate([x[:SB_EARLY], x[KEY_TILE:KEY_TILE + SB_EARLY]], axis=0)

    def with_early(x, e, combine):
        return jnp.concatenate([combine(x[:SB_EARLY], e[:SB_EARLY]), x[SB_EARLY:KEY_TILE],
                                combine(x[KEY_TILE:KEY_TILE + SB_EARLY], e[SB_EARLY:]),
                                x[KEY_TILE + SB_EARLY:]], axis=0)

    def log_terms(z, valid):
        log_beta = jnp.minimum(z, 0.0) - jnp.log(1.0 + jnp.exp(-jnp.abs(z)))
        log_1mb = log_beta - z
        if valid is not None:
            log_1mb = jnp.where(valid, log_1mb, 0.0)
        hi, lo = _split_bf16(log_1mb)
        return log_beta, jnp.concatenate([hi, lo], axis=1)

    def sweep(kt_ref, vt_ref, jobs, first_pass=False):
        chains = [(t * SB_PAIRS + p, sl) + job
                  for t, job in enumerate(jobs) for p, sl in enumerate(pair_lanes)]

        def scores(chain):
            c, sl, rows, _, erows = chain
            z = lax.dot_general(q2[c], kt_ref[rows, sl], _NT, preferred_element_type=F32)
            ze = None if erows is None else lax.dot_general(
                early(q2[c]), kt_ref[erows, sl], _NT, preferred_element_type=F32)
            return z, ze

        def suffix_sums(chain, z, ze):
            valids = chain[3]
            terms = [] if ze is None else [log_terms(ze, None)]
            terms += [log_terms(z[:, n * KEY_TILE:(n + 1) * KEY_TILE], valid)
                      for n, valid in enumerate(valids)]
            sums = jnp.dot(jnp.concatenate([hl for _, hl in terms], axis=0), uw,
                           preferred_element_type=F32)
            return [lb for lb, _ in terms], sums

        def weights(chain, log_betas, sums):
            c, _, _, valids, erows = chain
            off = 0 if erows is None else 2 * SB_EARLY
            lbs = log_betas if erows is None else log_betas[1:]
            carry = None if first_pass else carry_ref[c]
            w_n = [None] * len(valids)
            for n in reversed(range(len(valids))):
                cs = sums[off + n * 2 * KEY_TILE:off + (n + 1) * 2 * KEY_TILE]
                log_w = lbs[n] + cs[:, :KEY_TILE]
                w = jnp.exp(log_w if carry is None else log_w + carry)
                if valids[n] is not None:
                    w = jnp.where(valids[n], w, 0.0)
                w_n[n] = w.astype(BF16)
                carry = cs[:, KEY_TILE:] if carry is None else carry + cs[:, KEY_TILE:]
            we = None
            if erows is not None:
                cs = sums[:off]
                we = jnp.exp(log_betas[0] + cs[:, :KEY_TILE] + early(carry)).astype(BF16)
                carry = with_early(carry, cs[:, KEY_TILE:], jnp.add)
            carry_ref[c] = carry
            return jnp.concatenate(w_n, axis=1), we

        def accumulate(chain, w, we):
            c, sl, rows, _, erows = chain
            pv = jnp.dot(w, vt_ref[rows, sl], preferred_element_type=F32)
            if erows is not None:
                pv = with_early(pv, jnp.dot(we, vt_ref[erows, sl], preferred_element_type=F32),
                                jnp.add)
            if first_pass:
                acc_ref[c] = pv
            else:
                acc_ref[c] += pv

        stages = (scores, suffix_sums, weights, accumulate)
        held = [None] * len(chains)
        for k in range(len(chains) + len(stages) - 1):
            for s, stage in reversed(list(enumerate(stages))):
                i = k - s
                if 0 <= i < len(chains):
                    held[i] = stage(chains[i]) if s == 0 else stage(chains[i], *held[i])

    row2 = lax.broadcasted_iota(jnp.int32, (2 * KEY_TILE, KEY_TILE), 0) % KEY_TILE
    col2 = lax.broadcasted_iota(jnp.int32, (2 * KEY_TILE, KEY_TILE), 1)
    causal = col2 < row2

    def key_rows(tile, n_tiles):
        return pl.ds(pl.multiple_of(tile * KEY_TILE, KEY_TILE), n_tiles * KEY_TILE)

    i_first = j * SB_QTILES
    fused = j >= 1
    assert SB_QTILES >= 2

    @pl.when(fused)
    def _():
        sweep(k_ref, v_ref, [(key_rows(i_first + t - 1, 2), [None, causal],
                              key_rows(i_first + t - 2, 1)) for t in range(SB_QTILES)],
              first_pass=True)

    @pl.when(jnp.logical_not(fused))
    def _():
        sweep(k_ref, v_ref, [(key_rows(0, t + 1), [None] * t + [causal], None)
                             for t in range(SB_QTILES)], first_pass=True)

    def live_rows():
        top = jnp.max(carry_ref[...].reshape(-1, SUBLANES, LANES), axis=0)
        return (jnp.max(top) > SB_UNDERFLOW).astype(jnp.int32)

    left = [jnp.where(fused, i_first + t - 2, 0) for t in range(SB_QTILES)]
    late = row2 >= SB_EARLY

    def cond(c):
        it, live = c
        return jnp.logical_and(it < left[-1], live > 0)

    def body(c):
        it, _ = c
        catch_up = it < 0
        jobs = []
        for t in range(SB_QTILES):
            has = jnp.logical_or(catch_up, it < left[t])
            tile = jnp.where(catch_up, left[t], jnp.maximum(left[t] - 1 - it, 0))
            valid = jnp.logical_and(has, jnp.logical_or(late, jnp.logical_not(catch_up)))
            jobs.append((key_rows(tile, 1), [valid], None))
        sweep(k_ref, v_ref, jobs)
        return it + 1, live_rows()

    start = jnp.where(fused, -1, 0).astype(jnp.int32)
    _, live = lax.while_loop(cond, body, (start, live_rows()))

    @pl.when(live > 0)
    def _():
        sweep(mk_ref, mv_ref, [(slice(None), [col2 >= META_ROWS - N_META], None)] * SB_QTILES)

    for t, qr in enumerate(q_rows):
        for p, sl in enumerate(pair_lanes):
            acc = acc_ref[t * SB_PAIRS + p]
            out = jnp.where(first, acc[:KEY_TILE], acc[KEY_TILE:])
            gate = g_ref[qr, sl].astype(F32)
            o_ref[qr, sl] = (out * (gate * _sigmoid(gate))).astype(BF16)


def _stick_breaking(proj, proj_meta, uw, batch, seq):
    assert SB_WIDTH == COL_TILE
    proj3 = proj.reshape(batch, seq, IN_COLS)
    q_blk = SB_QTILES * KEY_TILE
    assert seq % q_blk == 0
    return pl.pallas_call(
        _sb_kernel,
        grid=(batch, seq // q_blk),
        in_specs=[
            pl.BlockSpec((None, q_blk, SB_WIDTH), lambda b, i: (b, i, T_SB_Q)),
            pl.BlockSpec((None, seq, SB_WIDTH), lambda b, i: (b, 0, T_SB_K)),
            pl.BlockSpec((None, seq, SB_WIDTH), lambda b, i: (b, 0, T_SB_V)),
            pl.BlockSpec((None, q_blk, SB_WIDTH), lambda b, i: (b, i, T_SB_GATE)),
            pl.BlockSpec((META_ROWS, SB_WIDTH), lambda b, i: (0, T_SB_K)),
            pl.BlockSpec((META_ROWS, SB_WIDTH), lambda b, i: (0, T_SB_V)),
            pl.BlockSpec((2 * KEY_TILE, 2 * KEY_TILE), lambda b, i: (0, 0)),
        ],
        out_specs=pl.BlockSpec((None, q_blk, SB_WIDTH), lambda b, i: (b, i, 0)),
        out_shape=jax.ShapeDtypeStruct((batch, seq, SB_WIDTH), BF16),
        scratch_shapes=[pltpu.VMEM((SB_QTILES * SB_PAIRS, 2 * KEY_TILE, LANES), F32),
                        pltpu.VMEM((SB_QTILES * SB_PAIRS, 2 * KEY_TILE, LANES), F32)],
        compiler_params=pltpu.CompilerParams(
            dimension_semantics=("parallel", "arbitrary"),
            vmem_limit_bytes=VMEM_LIMIT_BYTES),
        name="stick_breaking",
    )(proj3, proj3, proj3, proj3, proj_meta, proj_meta, uw)


def _hg_decay(f_logit, lbv, sum_w, pad_rows):
    f = lbv + (1.0 - lbv) * _sigmoid(f_logit)
    if pad_rows:
        row = lax.broadcasted_iota(jnp.int32, f.shape, 0)
        f = jnp.where(row >= pad_rows, f, 1.0)
    g_hi, g_lo = _split_bf16(jnp.log(f))
    sums = jnp.dot(sum_w, jnp.concatenate([g_hi, g_lo], axis=0), preferred_element_type=F32)
    return sums, 1.0 - f


def _hg_next_state(state_t, b, kk, v):
    b_last = b[HG_CHUNK - 1:HG_CHUNK, :]
    k_dec = (kk * jnp.exp(b_last - b)).astype(BF16)
    return state_t * jnp.exp(b_last) + lax.dot_general(v, k_dec, _TN, preferred_element_type=F32)


def _hg_scores(qf, kk, sums, masks):
    a = jnp.where(masks[0], lax.dot_general(qf.astype(BF16), kk.astype(BF16), _NT,
                                            preferred_element_type=F32), 0.0)
    for lvl in range(len(HG_LEVELS)):
        e = jnp.exp(sums[(1 + lvl) * HG_CHUNK:(2 + lvl) * HG_CHUNK])
        part = lax.dot_general((qf * e).astype(BF16), (kk * e).astype(BF16), _NT,
                               preferred_element_type=F32)
        a = jnp.where(masks[1 + lvl], part, a)
    return a


def _hg_score_masks():
    row = lax.broadcasted_iota(jnp.int32, (HG_CHUNK, HG_CHUNK), 0)
    col = lax.broadcasted_iota(jnp.int32, (HG_CHUNK, HG_CHUNK), 1)
    return [row == col] + [
        ((row // (2 * m)) == (col // (2 * m))) & ((row // m) % 2 == 1) & ((col // m) % 2 == 0)
        for m in HG_LEVELS]


def _hg_kernel(q_ref, f_ref, v_ref, gate_ref, mf_ref, mv_ref, lbl_ref, gn_ref,
               sumw_ref, o_ref, state_ref, *, chunks_per_tile):
    t = pl.program_id(1)
    sum_w = sumw_ref[...]
    lbl = lbl_ref[...]
    ex = jnp.exp(lbl - jnp.max(lbl, axis=0, keepdims=True))
    lb = ex[0:1, :] / jnp.sum(ex, axis=0, keepdims=True)
    gn = gn_ref[...]
    head_lanes = [slice(h * HG_DIM, (h + 1) * HG_DIM) for h in range(HG_HEADS)]

    @pl.when(t == 0)
    def _():
        for h, sl in enumerate(head_lanes):
            b, kk = _hg_decay(mf_ref[:, sl], lb[:, sl], sum_w[:HG_CHUNK], HG_CHUNK - N_META)
            state_ref[h] = _hg_next_state(jnp.zeros((HG_DIM, HG_DIM), F32), b, kk, mv_ref[:, sl])

    masks = _hg_score_masks()

    def body(c, carry):
        base = c * (HG_UNROLL * HG_CHUNK)
        rows = [pl.ds(pl.multiple_of(base + u * HG_CHUNK, HG_CHUNK), HG_CHUNK)
                for u in range(HG_UNROLL)]
        decay = [[_hg_decay(f_ref[r, sl], lb[:, sl], sum_w, 0) for sl in head_lanes] for r in rows]
        states = [state_ref[h] for h in range(HG_HEADS)]
        scores, inter = [], []
        for u, r in enumerate(rows):
            for h, sl in enumerate(head_lanes):
                sums, kk = decay[u][h]
                b = sums[:HG_CHUNK]
                q = q_ref[r, sl].astype(F32)
                qf = q * _sigmoid(q) * HG_SCALE
                scores.append(_hg_scores(qf, kk, sums, masks))
                inter.append(lax.dot_general((qf * jnp.exp(b)).astype(BF16),
                                             states[h].astype(BF16), _NT,
                                             preferred_element_type=F32))
                states[h] = _hg_next_state(states[h], b, kk, v_ref[r, sl])
        for h in range(HG_HEADS):
            state_ref[h] = states[h]
        for u, r in enumerate(rows):
            for h, sl in enumerate(head_lanes):
                i = u * HG_HEADS + h
                o = inter[i] + jnp.dot(scores[i].astype(BF16), v_ref[r, sl],
                                       preferred_element_type=F32)
                o = o * lax.rsqrt(jnp.mean(o * o, axis=-1, keepdims=True) + EPS) * gn[:, sl]
                gate = gate_ref[r, sl].astype(F32)
                o_ref[r, sl] = (o * (gate * _sigmoid(gate))).astype(BF16)
        return carry

    lax.fori_loop(0, chunks_per_tile // HG_UNROLL, body, 0)


def _hgrn2(proj, f_proj, proj_meta, f_meta, lb_logits, hg_norm_g, sum_w, batch, seq, row_tile):
    assert seq % row_tile == 0 and row_tile % HG_CHUNK == 0
    proj3 = proj.reshape(batch, seq, IN_COLS)
    f3 = f_proj.reshape(batch, seq, HG_WIDTH)
    meta_blk = META_ROWS // HG_CHUNK - 1
    const = lambda b, t: (0, 0)
    return pl.pallas_call(
        functools.partial(_hg_kernel, chunks_per_tile=row_tile // HG_CHUNK),
        grid=(batch, seq // row_tile),
        in_specs=[
            pl.BlockSpec((None, row_tile, HG_WIDTH), lambda b, t: (b, t, T_HG_Q)),
            pl.BlockSpec((None, row_tile, HG_WIDTH), lambda b, t: (b, t, 0)),
            pl.BlockSpec((None, row_tile, HG_WIDTH), lambda b, t: (b, t, T_HG_I)),
            pl.BlockSpec((None, row_tile, HG_WIDTH), lambda b, t: (b, t, T_HG_GATE)),
            pl.BlockSpec((HG_CHUNK, HG_WIDTH), lambda b, t: (meta_blk, 0)),
            pl.BlockSpec((HG_CHUNK, HG_WIDTH), lambda b, t: (meta_blk, T_HG_I)),
            pl.BlockSpec(lb_logits.shape, const),
            pl.BlockSpec((1, HG_WIDTH), const),
            pl.BlockSpec(sum_w.shape, const),
        ],
        out_specs=pl.BlockSpec((None, row_tile, HG_WIDTH), lambda b, t: (b, t, 0)),
        out_shape=jax.ShapeDtypeStruct((batch, seq, HG_WIDTH), BF16),
        scratch_shapes=[pltpu.VMEM((HG_HEADS, HG_DIM, HG_DIM), F32)],
        compiler_params=pltpu.CompilerParams(
            dimension_semantics=("parallel", "arbitrary"),
            vmem_limit_bytes=VMEM_LIMIT_BYTES),
        name="hgrn2",
    )(proj3, f3, proj3, proj3, f_meta, proj_meta, lb_logits, hg_norm_g, sum_w)


MERGE_SLAB = 256


def _final_kernel(ysb_ref, yhg_ref, gsb_ref, ghg_ref, x_ref, wsb_ref, whg_ref, wout_ref, fg_ref,
                  o_ref, wsb_s, whg_s, wout_s):
    @pl.when(pl.program_id(0) == 0)
    def _():
        wsb_s[...] = wsb_ref[...].astype(BF16)
        whg_s[...] = whg_ref[...].astype(BF16)
        wout_s[...] = wout_ref[...].astype(BF16)

    n_slab = o_ref.shape[0] // MERGE_SLAB
    slabs = [pl.ds(s * MERGE_SLAB, MERGE_SLAB) for s in range(n_slab)]
    a = [jnp.dot(ysb_ref[sl, :], wsb_s[...], preferred_element_type=F32) for sl in slabs]
    b = [jnp.dot(yhg_ref[sl, :], whg_s[...], preferred_element_type=F32) for sl in slabs]
    merged = [(_sigmoid(gsb_ref[sl, :].astype(F32)) * a[s]
               + _sigmoid(ghg_ref[sl, :].astype(F32)) * b[s]).astype(BF16)
              for s, sl in enumerate(slabs)]
    h = [x_ref[sl, :] + jnp.dot(merged[s], wout_s[...], preferred_element_type=F32)
         for s, sl in enumerate(slabs)]
    for s, sl in enumerate(slabs):
        ms = jnp.mean(h[s] * h[s], axis=-1, keepdims=True)
        o_ref[sl, :] = h[s] * lax.rsqrt(ms + EPS) * fg_ref[...]


def _final(y_sb, y_hg, proj, x2d, w_sb_out, w_hg_out, w_out, final_norm_g, row_tile):
    rows = x2d.shape[0]
    assert rows % row_tile == 0
    gate_blk = (4 * SB_WIDTH + 4 * HG_WIDTH) // D_MODEL
    const = lambda r: (0, 0)
    return pl.pallas_call(
        _final_kernel,
        grid=(rows // row_tile,),
        in_specs=[
            pl.BlockSpec((row_tile, SB_WIDTH), lambda r: (r, 0)),
            pl.BlockSpec((row_tile, HG_WIDTH), lambda r: (r, 0)),
            pl.BlockSpec((row_tile, D_MODEL), lambda r: (r, gate_blk)),
            pl.BlockSpec((row_tile, D_MODEL), lambda r: (r, gate_blk + 1)),
            pl.BlockSpec((row_tile, D_MODEL), lambda r: (r, 0)),
            pl.BlockSpec((SB_WIDTH, D_MODEL), const),
            pl.BlockSpec((HG_WIDTH, D_MODEL), const),
            pl.BlockSpec((D_MODEL, D_MODEL), const),
            pl.BlockSpec((1, D_MODEL), const),
        ],
        out_specs=pl.BlockSpec((row_tile, D_MODEL), lambda r: (r, 0)),
        out_shape=jax.ShapeDtypeStruct((rows, D_MODEL), F32),
        scratch_shapes=[pltpu.VMEM((SB_WIDTH, D_MODEL), BF16),
                        pltpu.VMEM((HG_WIDTH, D_MODEL), BF16),
                        pltpu.VMEM((D_MODEL, D_MODEL), BF16)],
        compiler_params=pltpu.CompilerParams(
            dimension_semantics=("arbitrary",),
            vmem_limit_bytes=VMEM_LIMIT_BYTES),
        name="merge_out",
    )(y_sb, y_hg, proj, proj, x2d, w_sb_out, w_hg_out, w_out, final_norm_g)


def _suffix_sum_weights():
    j = np.arange(2 * KEY_TILE)[:, None] % KEY_TILE
    s = np.arange(2 * KEY_TILE)[None, :]
    return jnp.asarray(np.where(s < KEY_TILE, j > s, True), BF16)


def _hg_sum_weights():
    r = np.arange(HG_CHUNK)[:, None]
    c = np.arange(HG_CHUNK)[None, :]
    groups = [c <= r]
    for m in HG_LEVELS:
        ref = (r // (2 * m)) * 2 * m + m - 1
        groups.append(np.where(r > ref, (c > ref) & (c <= r), (c > r) & (c <= ref)))
    w = np.concatenate(groups, axis=0)
    return jnp.asarray(np.concatenate([w, w], axis=1), BF16)


def kernel(x, meta, norm_g, w_in, w_sb_out, w_hg_out, w_out, hg_norm_g, hg_lb_logits, final_norm_g):
    batch, seq, d = x.shape
    assert d == D_MODEL and meta.shape == (N_META, D_MODEL)
    assert norm_g.shape[0] == 1 and w_in.shape == (1, D_MODEL, IN_COLS)
    assert seq % HG_ROWS == 0 and (batch * seq) % INPROJ_ROWS == 0

    x2d = x.reshape(batch * seq, D_MODEL)
    meta_blk = jnp.concatenate(
        [jnp.zeros((META_ROWS - N_META, D_MODEL), x.dtype), meta.astype(x.dtype)], axis=0)
    w_in2 = w_in.reshape(D_MODEL, IN_COLS)

    proj, f_proj = _inproj(x2d, norm_g, w_in2, INPROJ_ROWS)
    meta_cols = pl.cdiv((T_HG_GATE + 1) * COL_TILE, INPROJ_COLS) * INPROJ_COLS
    proj_meta, f_meta = _inproj(meta_blk, norm_g, w_in2, META_ROWS, n_cols=meta_cols)

    y_sb = _stick_breaking(proj, proj_meta, _suffix_sum_weights(), batch, seq)
    y_hg = _hgrn2(proj, f_proj, proj_meta, f_meta, hg_lb_logits, hg_norm_g,
                  _hg_sum_weights(), batch, seq, HG_ROWS)

    out = _final(y_sb.reshape(batch * seq, SB_WIDTH), y_hg.reshape(batch * seq, HG_WIDTH), proj,
                 x2d, w_sb_out.reshape(SB_WIDTH, D_MODEL), w_hg_out.reshape(HG_WIDTH, D_MODEL),
                 w_out.reshape(D_MODEL, D_MODEL), final_norm_g.reshape(1, D_MODEL), MERGE_ROWS)
    return out.reshape(batch, seq, D_MODEL)
```

```python
import functools

import jax
import jax.numpy as jnp
import numpy as np
from jax import lax
from jax.experimental import pallas as pl
from jax.experimental.pallas import tpu as pltpu

F32 = jnp.float32
BF16 = jnp.bfloat16

D_MODEL = 1024
N_META = 16
SB_HEADS = 8
SB_HEAD_DIM = 64
SB_WIDTH = SB_HEADS * SB_HEAD_DIM
SB_SCALE = SB_HEAD_DIM ** -0.5
HG_HEADS = 4
HG_DIM = 128
HG_WIDTH = HG_HEADS * HG_DIM
HG_SCALE = HG_DIM ** -0.5
IN_COLS = 4 * SB_WIDTH + 4 * HG_WIDTH + 2 * D_MODEL
EPS = 1e-6

LANES = 128
SUBLANES = 8
VMEM_LIMIT_BYTES = 56 * 1024 * 1024

INPROJ_ROWS = 2048
HG_ROWS = 1024
MERGE_ROWS = 1024

COL_TILE = 512
T_SB_Q, T_SB_K, T_SB_V, T_SB_GATE, T_HG_Q, T_HG_F, T_HG_I, T_HG_GATE = range(8)
N_COL_TILES = IN_COLS // COL_TILE

KEY_TILE = 128
META_ROWS = 128
HG_CHUNK = 64
HG_LEVELS = (1, 2, 4, 8, 16, 32)
HG_UNROLL = 2


_NT = (((1,), (1,)), ((), ()))
_TN = (((0,), (0,)), ((), ()))


def _sigmoid(x):
    return 1.0 / (1.0 + jnp.exp(-x))


def _split_bf16(x):
    hi = x.astype(BF16)
    lo = (x - hi.astype(F32)).astype(BF16)
    return hi, lo


INPROJ_COLS = 2 * COL_TILE
INPROJ_SLAB = 512
assert (T_SB_Q * COL_TILE) % INPROJ_COLS == 0


def _inproj_kernel(x_ref, g_ref, w_ref, o_ref, f_ref, u_ref):
    n = pl.program_id(1)
    f_step, f_col = divmod(T_HG_F * COL_TILE, INPROJ_COLS)
    assert f_step != 0

    def weights():
        lane = lax.broadcasted_iota(jnp.int32, (1, INPROJ_COLS), 1)
        is_q = jnp.logical_and(n == (T_SB_Q * COL_TILE) // INPROJ_COLS, lane < SB_WIDTH)
        return (w_ref[...] * jnp.where(is_q, SB_SCALE, 1.0).astype(F32)).astype(BF16)

    @pl.when(n == 0)
    def _():
        wb = weights()
        slab = min(INPROJ_SLAB, x_ref.shape[0])
        for s in range(x_ref.shape[0] // slab):
            rows = pl.ds(s * slab, slab)
            x = x_ref[rows, :]
            ms = jnp.mean(x * x, axis=-1, keepdims=True)
            u = (x * lax.rsqrt(ms + EPS) * g_ref[...]).astype(BF16)
            u_ref[rows, :] = u
            o_ref[rows, :] = jnp.dot(u, wb, preferred_element_type=F32).astype(BF16)

    @pl.when(n != 0)
    def _():
        p = jnp.dot(u_ref[...], weights(), preferred_element_type=F32)
        o_ref[...] = p.astype(BF16)

        @pl.when(n == f_step)
        def _():
            f_ref[...] = p[:, f_col:f_col + HG_WIDTH]


def _inproj(x2d, norm_g, w_in, row_tile, n_cols=IN_COLS):
    rows = x2d.shape[0]
    assert rows % row_tile == 0 and n_cols % INPROJ_COLS == 0 and n_cols >= (T_HG_F + 1) * COL_TILE
    return pl.pallas_call(
        _inproj_kernel,
        grid=(rows // row_tile, n_cols // INPROJ_COLS),
        in_specs=[
            pl.BlockSpec((row_tile, D_MODEL), lambda r, n: (r, 0)),
            pl.BlockSpec((1, D_MODEL), lambda r, n: (0, 0)),
            pl.BlockSpec((D_MODEL, INPROJ_COLS), lambda r, n: (0, n)),
        ],
        out_specs=[
            pl.BlockSpec((row_tile, INPROJ_COLS), lambda r, n: (r, n)),
            pl.BlockSpec((row_tile, HG_WIDTH), lambda r, n: (r, 0)),
        ],
        out_shape=[
            jax.ShapeDtypeStruct((rows, n_cols), BF16),
            jax.ShapeDtypeStruct((rows, HG_WIDTH), F32),
        ],
        scratch_shapes=[pltpu.VMEM((row_tile, D_MODEL), BF16)],
        compiler_params=pltpu.CompilerParams(
            dimension_semantics=("parallel", "arbitrary"),
            vmem_limit_bytes=VMEM_LIMIT_BYTES),
        name="inproj",
    )(x2d, norm_g, w_in)


SB_PAIRS = SB_WIDTH // LANES
SB_UNDERFLOW = -104.0
SB_QTILES = 4
SB_EARLY = 48


def _sb_kernel(q_ref, k_ref, v_ref, g_ref, mk_ref, mv_ref, uw_ref, o_ref, acc_ref, carry_ref):
    j = pl.program_id(1)
    lane = lax.broadcasted_iota(jnp.int32, (KEY_TILE, LANES), 1)
    first = lane < SB_HEAD_DIM
    uw = uw_ref[...]
    pair_lanes = [slice(p * LANES, (p + 1) * LANES) for p in range(SB_PAIRS)]
    q_rows = [slice(t * KEY_TILE, (t + 1) * KEY_TILE) for t in range(SB_QTILES)]
    q2 = []
    for qr in q_rows:
        for sl in pair_lanes:
            q = q_ref[qr, sl]
            zero = jnp.zeros_like(q)
            q2.append(jnp.concatenate([jnp.where(first, q, zero), jnp.where(first, zero, q)], axis=0))

    def early(x):
        return jnp.concatenate([x[:SB_EARLY], x[KEY_TILE:KEY_TILE + SB_EARLY]], axis=0)

    def with_early(x, e, combine):
        return jnp.concatenate([combine(x[:SB_EARLY], e[:SB_EARLY]), x[SB_EARLY:KEY_TILE],
                                combine(x[KEY_TILE:KEY_TILE + SB_EARLY], e[SB_EARLY:]),
                                x[KEY_TILE + SB_EARLY:]], axis=0)

    def log_terms(z, valid):
        log_beta = jnp.minimum(z, 0.0) - jnp.log(1.0 + jnp.exp(-jnp.abs(z)))
        log_1mb = log_beta - z
        if valid is not None:
            log_1mb = jnp.where(valid, log_1mb, 0.0)
        hi, lo = _split_bf16(log_1mb)
        return log_beta, jnp.concatenate([hi, lo], axis=1)

    def sweep(kt_ref, vt_ref, jobs, first_pass=False):
        chains = [(t * SB_PAIRS + p, sl) + job
                  for t, job in enumerate(jobs) for p, sl in enumerate(pair_lanes)]

        def scores(chain):
            c, sl, rows, _, erows = chain
            z = lax.dot_general(q2[c], kt_ref[rows, sl], _NT, preferred_element_type=F32)
            ze = None if erows is None else lax.dot_general(
                early(q2[c]), kt_ref[erows, sl], _NT, preferred_element_type=F32)
            return z, ze

        def suffix_sums(chain, z, ze):
            valids = chain[3]
            terms = [] if ze is None else [log_terms(ze, None)]
            terms += [log_terms(z[:, n * KEY_TILE:(n + 1) * KEY_TILE], valid)
                      for n, valid in enumerate(valids)]
            sums = jnp.dot(jnp.concatenate([hl for _, hl in terms], axis=0), uw,
                           preferred_element_type=F32)
            return [lb for lb, _ in terms], sums

        def weights(chain, log_betas, sums):
            c, _, _, valids, erows = chain
            off = 0 if erows is None else 2 * SB_EARLY
            lbs = log_betas if erows is None else log_betas[1:]
            carry = None if first_pass else carry_ref[c]
            w_n = [None] * len(valids)
            for n in reversed(range(len(valids))):
                cs = sums[off + n * 2 * KEY_TILE:off + (n + 1) * 2 * KEY_TILE]
                log_w = lbs[n] + cs[:, :KEY_TILE]
                w = jnp.exp(log_w if carry is None else log_w + carry)
                if valids[n] is not None:
                    w = jnp.where(valids[n], w, 0.0)
                w_n[n] = w.astype(BF16)
                carry = cs[:, KEY_TILE:] if carry is None else carry + cs[:, KEY_TILE:]
            we = None
            if erows is not None:
                cs = sums[:off]
                we = jnp.exp(log_betas[0] + cs[:, :KEY_TILE] + early(carry)).astype(BF16)
                carry = with_early(carry, cs[:, KEY_TILE:], jnp.add)
            carry_ref[c] = carry
            return jnp.concatenate(w_n, axis=1), we

        def accumulate(chain, w, we):
            c, sl, rows, _, erows = chain
            pv = jnp.dot(w, vt_ref[rows, sl], preferred_element_type=F32)
            if erows is not None:
                pv = with_early(pv, jnp.dot(we, vt_ref[erows, sl], preferred_element_type=F32),
                                jnp.add)
            if first_pass:
                acc_ref[c] = pv
            else:
                acc_ref[c] += pv

        stages = (scores, suffix_sums, weights, accumulate)
        held = [None] * len(chains)
        for k in range(len(chains) + len(stages) - 1):
            for s, stage in reversed(list(enumerate(stages))):
                i = k - s
                if 0 <= i < len(chains):
                    held[i] = stage(chains[i]) if s == 0 else stage(chains[i], *held[i])

    row2 = lax.broadcasted_iota(jnp.int32, (2 * KEY_TILE, KEY_TILE), 0) % KEY_TILE
    col2 = lax.broadcasted_iota(jnp.int32, (2 * KEY_TILE, KEY_TILE), 1)
    causal = col2 < row2

    def key_rows(tile, n_tiles):
        return pl.ds(pl.multiple_of(tile * KEY_TILE, KEY_TILE), n_tiles * KEY_TILE)

    i_first = j * SB_QTILES
    fused = j >= 1
    assert SB_QTILES >= 2

    @pl.when(fused)
    def _():
        sweep(k_ref, v_ref, [(key_rows(i_first + t - 1, 2), [None, causal],
                              key_rows(i_first + t - 2, 1)) for t in range(SB_QTILES)],
              first_pass=True)

    @pl.when(jnp.logical_not(fused))
    def _():
        sweep(k_ref, v_ref, [(key_rows(0, t + 1), [None] * t + [causal], None)
                             for t in range(SB_QTILES)], first_pass=True)

    def live_rows():
        top = jnp.max(carry_ref[...].reshape(-1, SUBLANES, LANES), axis=0)
        return (jnp.max(top) > SB_UNDERFLOW).astype(jnp.int32)

    left = [jnp.where(fused, i_first + t - 2, 0) for t in range(SB_QTILES)]
    late = row2 >= SB_EARLY

    def cond(c):
        it, live = c
        return jnp.logical_and(it < left[-1], live > 0)

    def body(c):
        it, _ = c
        catch_up = it < 0
        jobs = []
        for t in range(SB_QTILES):
            has = jnp.logical_or(catch_up, it < left[t])
            tile = jnp.where(catch_up, left[t], jnp.maximum(left[t] - 1 - it, 0))
            valid = jnp.logical_and(has, jnp.logical_or(late, jnp.logical_not(catch_up)))
            jobs.append((key_rows(tile, 1), [valid], None))
        sweep(k_ref, v_ref, jobs)
        return it + 1, live_rows()

    start = jnp.where(fused, -1, 0).astype(jnp.int32)
    _, live = lax.while_loop(cond, body, (start, live_rows()))

    @pl.when(live > 0)
    def _():
        sweep(mk_ref, mv_ref, [(slice(None), [col2 >= META_ROWS - N_META], None)] * SB_QTILES)

    for t, qr in enumerate(q_rows):
        for p, sl in enumerate(pair_lanes):
            acc = acc_ref[t * SB_PAIRS + p]
            out = jnp.where(first, acc[:KEY_TILE], acc[KEY_TILE:])
            gate = g_ref[qr, sl].astype(F32)
            o_ref[qr, sl] = (out * (gate * _sigmoid(gate))).astype(BF16)


def _stick_breaking(proj, proj_meta, uw, batch, seq):
    assert SB_WIDTH == COL_TILE
    proj3 = proj.reshape(batch, seq, IN_COLS)
    q_blk = SB_QTILES * KEY_TILE
    assert seq % q_blk == 0
    return pl.pallas_call(
        _sb_kernel,
        grid=(batch, seq // q_blk),
        in_specs=[
            pl.BlockSpec((None, q_blk, SB_WIDTH), lambda b, i: (b, i, T_SB_Q)),
            pl.BlockSpec((None, seq, SB_WIDTH), lambda b, i: (b, 0, T_SB_K)),
            pl.BlockSpec((None, seq, SB_WIDTH), lambda b, i: (b, 0, T_SB_V)),
            pl.BlockSpec((None, q_blk, SB_WIDTH), lambda b, i: (b, i, T_SB_GATE)),
            pl.BlockSpec((META_ROWS, SB_WIDTH), lambda b, i: (0, T_SB_K)),
            pl.BlockSpec((META_ROWS, SB_WIDTH), lambda b, i: (0, T_SB_V)),
            pl.BlockSpec((2 * KEY_TILE, 2 * KEY_TILE), lambda b, i: (0, 0)),
        ],
        out_specs=pl.BlockSpec((None, q_blk, SB_WIDTH), lambda b, i: (b, i, 0)),
        out_shape=jax.ShapeDtypeStruct((batch, seq, SB_WIDTH), BF16),
        scratch_shapes=[pltpu.VMEM((SB_QTILES * SB_PAIRS, 2 * KEY_TILE, LANES), F32),
                        pltpu.VMEM((SB_QTILES * SB_PAIRS, 2 * KEY_TILE, LANES), F32)],
        compiler_params=pltpu.CompilerParams(
            dimension_semantics=("parallel", "arbitrary"),
            vmem_limit_bytes=VMEM_LIMIT_BYTES),
        name="stick_breaking",
    )(proj3, proj3, proj3, proj3, proj_meta, proj_meta, uw)


def _hg_decay(f_logit, lbv, sum_w, pad_rows):
    f = lbv + (1.0 - lbv) * _sigmoid(f_logit)
    if pad_rows:
        row = lax.broadcasted_iota(jnp.int32, f.shape, 0)
        f = jnp.where(row >= pad_rows, f, 1.0)
    g_hi, g_lo = _split_bf16(jnp.log(f))
    sums = jnp.dot(sum_w, jnp.concatenate([g_hi, g_lo], axis=0), preferred_element_type=F32)
    return sums, 1.0 - f


def _hg_next_state(state_t, b, kk, v):
    b_last = b[HG_CHUNK - 1:HG_CHUNK, :]
    k_dec = (kk * jnp.exp(b_last - b)).astype(BF16)
    return state_t * jnp.exp(b_last) + lax.dot_general(v, k_dec, _TN, preferred_element_type=F32)


def _hg_scores(qf, kk, sums, masks):
    a = jnp.where(masks[0], lax.dot_general(qf.astype(BF16), kk.astype(BF16), _NT,
                                            preferred_element_type=F32), 0.0)
    for lvl in range(len(HG_LEVELS)):
        e = jnp.exp(sums[(1 + lvl) * HG_CHUNK:(2 + lvl) * HG_CHUNK])
        part = lax.dot_general((qf * e).astype(BF16), (kk * e).astype(BF16), _NT,
                               preferred_element_type=F32)
        a = jnp.where(masks[1 + lvl], part, a)
    return a


def _hg_score_masks():
    row = lax.broadcasted_iota(jnp.int32, (HG_CHUNK, HG_CHUNK), 0)
    col = lax.broadcasted_iota(jnp.int32, (HG_CHUNK, HG_CHUNK), 1)
    return [row == col] + [
        ((row // (2 * m)) == (col // (2 * m))) & ((row // m) % 2 == 1) & ((col // m) % 2 == 0)
        for m in HG_LEVELS]


def _hg_kernel(q_ref, f_ref, v_ref, gate_ref, mf_ref, mv_ref, lbl_ref, gn_ref,
               sumw_ref, o_ref, state_ref, *, chunks_per_tile):
    t = pl.program_id(1)
    sum_w = sumw_ref[...]
    lbl = lbl_ref[...]
    ex = jnp.exp(lbl - jnp.max(lbl, axis=0, keepdims=True))
    lb = ex[0:1, :] / jnp.sum(ex, axis=0, keepdims=True)
    gn = gn_ref[...]
    head_lanes = [slice(h * HG_DIM, (h + 1) * HG_DIM) for h in range(HG_HEADS)]

    @pl.when(t == 0)
    def _():
        for h, sl in enumerate(head_lanes):
            b, kk = _hg_decay(mf_ref[:, sl], lb[:, sl], sum_w[:HG_CHUNK], HG_CHUNK - N_META)
            state_ref[h] = _hg_next_state(jnp.zeros((HG_DIM, HG_DIM), F32), b, kk, mv_ref[:, sl])

    masks = _hg_score_masks()

    def body(c, carry):
        base = c * (HG_UNROLL * HG_CHUNK)
        rows = [pl.ds(pl.multiple_of(base + u * HG_CHUNK, HG_CHUNK), HG_CHUNK)
                for u in range(HG_UNROLL)]
        decay = [[_hg_decay(f_ref[r, sl], lb[:, sl], sum_w, 0) for sl in head_lanes] for r in rows]
        states = [state_ref[h] for h in range(HG_HEADS)]
        scores, inter = [], []
        for u, r in enumerate(rows):
            for h, sl in enumerate(head_lanes):
                sums, kk = decay[u][h]
                b = sums[:HG_CHUNK]
                q = q_ref[r, sl].astype(F32)
                qf = q * _sigmoid(q) * HG_SCALE
                scores.append(_hg_scores(qf, kk, sums, masks))
                inter.append(lax.dot_general((qf * jnp.exp(b)).astype(BF16),
                                             states[h].astype(BF16), _NT,
                                             preferred_element_type=F32))
                states[h] = _hg_next_state(states[h], b, kk, v_ref[r, sl])
        for h in range(HG_HEADS):
            state_ref[h] = states[h]
        for u, r in enumerate(rows):
            for h, sl in enumerate(head_lanes):
                i = u * HG_HEADS + h
                o = inter[i] + jnp.dot(scores[i].astype(BF16), v_ref[r, sl],
                                       preferred_element_type=F32)
                o = o * lax.rsqrt(jnp.mean(o * o, axis=-1, keepdims=True) + EPS) * gn[:, sl]
                gate = gate_ref[r, sl].astype(F32)
                o_ref[r, sl] = (o * (gate * _sigmoid(gate))).astype(BF16)
        return carry

    lax.fori_loop(0, chunks_per_tile // HG_UNROLL, body, 0)


def _hgrn2(proj, f_proj, proj_meta, f_meta, lb_logits, hg_norm_g, sum_w, batch, seq, row_tile):
    assert seq % row_tile == 0 and row_tile % HG_CHUNK == 0
    proj3 = proj.reshape(batch, seq, IN_COLS)
    f3 = f_proj.reshape(batch, seq, HG_WIDTH)
    meta_blk = META_ROWS // HG_CHUNK - 1
    const = lambda b, t: (0, 0)
    return pl.pallas_call(
        functools.partial(_hg_kernel, chunks_per_tile=row_tile // HG_CHUNK),
        grid=(batch, seq // row_tile),
        in_specs=[
            pl.BlockSpec((None, row_tile, HG_WIDTH), lambda b, t: (b, t, T_HG_Q)),
            pl.BlockSpec((None, row_tile, HG_WIDTH), lambda b, t: (b, t, 0)),
            pl.BlockSpec((None, row_tile, HG_WIDTH), lambda b, t: (b, t, T_HG_I)),
            pl.BlockSpec((None, row_tile, HG_WIDTH), lambda b, t: (b, t, T_HG_GATE)),
            pl.BlockSpec((HG_CHUNK, HG_WIDTH), lambda b, t: (meta_blk, 0)),
            pl.BlockSpec((HG_CHUNK, HG_WIDTH), lambda b, t: (meta_blk, T_HG_I)),
            pl.BlockSpec(lb_logits.shape, const),
            pl.BlockSpec((1, HG_WIDTH), const),
            pl.BlockSpec(sum_w.shape, const),
        ],
        out_specs=pl.BlockSpec((None, row_tile, HG_WIDTH), lambda b, t: (b, t, 0)),
        out_shape=jax.ShapeDtypeStruct((batch, seq, HG_WIDTH), BF16),
        scratch_shapes=[pltpu.VMEM((HG_HEADS, HG_DIM, HG_DIM), F32)],
        compiler_params=pltpu.CompilerParams(
            dimension_semantics=("parallel", "arbitrary"),
            vmem_limit_bytes=VMEM_LIMIT_BYTES),
        name="hgrn2",
    )(proj3, f3, proj3, proj3, f_meta, proj_meta, lb_logits, hg_norm_g, sum_w)


MERGE_SLAB = 256


def _final_kernel(ysb_ref, yhg_ref, gsb_ref, ghg_ref, x_ref, wsb_ref, whg_ref, wout_ref, fg_ref,
                  o_ref, wsb_s, whg_s, wout_s):
    @pl.when(pl.program_id(0) == 0)
    def _():
        wsb_s[...] = wsb_ref[...].astype(BF16)
        whg_s[...] = whg_ref[...].astype(BF16)
        wout_s[...] = wout_ref[...].astype(BF16)

    n_slab = o_ref.shape[0] // MERGE_SLAB
    slabs = [pl.ds(s * MERGE_SLAB, MERGE_SLAB) for s in range(n_slab)]
    a = [jnp.dot(ysb_ref[sl, :], wsb_s[...], preferred_element_type=F32) for sl in slabs]
    b = [jnp.dot(yhg_ref[sl, :], whg_s[...], preferred_element_type=F32) for sl in slabs]
    merged = [(_sigmoid(gsb_ref[sl, :].astype(F32)) * a[s]
               + _sigmoid(ghg_ref[sl, :].astype(F32)) * b[s]).astype(BF16)
              for s, sl in enumerate(slabs)]
    h = [x_ref[sl, :] + jnp.dot(merged[s], wout_s[...], preferred_element_type=F32)
         for s, sl in enumerate(slabs)]
    for s, sl in enumerate(slabs):
        ms = jnp.mean(h[s] * h[s], axis=-1, keepdims=True)
        o_ref[sl, :] = h[s] * lax.rsqrt(ms + EPS) * fg_ref[...]


def _final(y_sb, y_hg, proj, x2d, w_sb_out, w_hg_out, w_out, final_norm_g, row_tile):
    rows = x2d.shape[0]
    assert rows % row_tile == 0
    gate_blk = (4 * SB_WIDTH + 4 * HG_WIDTH) // D_MODEL
    const = lambda r: (0, 0)
    return pl.pallas_call(
        _final_kernel,
        grid=(rows // row_tile,),
        in_specs=[
            pl.BlockSpec((row_tile, SB_WIDTH), lambda r: (r, 0)),
            pl.BlockSpec((row_tile, HG_WIDTH), lambda r: (r, 0)),
            pl.BlockSpec((row_tile, D_MODEL), lambda r: (r, gate_blk)),
            pl.BlockSpec((row_tile, D_MODEL), lambda r: (r, gate_blk + 1)),
            pl.BlockSpec((row_tile, D_MODEL), lambda r: (r, 0)),
            pl.BlockSpec((SB_WIDTH, D_MODEL), const),
            pl.BlockSpec((HG_WIDTH, D_MODEL), const),
            pl.BlockSpec((D_MODEL, D_MODEL), const),
            pl.BlockSpec((1, D_MODEL), const),
        ],
        out_specs=pl.BlockSpec((row_tile, D_MODEL), lambda r: (r, 0)),
        out_shape=jax.ShapeDtypeStruct((rows, D_MODEL), F32),
        scratch_shapes=[pltpu.VMEM((SB_WIDTH, D_MODEL), BF16),
                        pltpu.VMEM((HG_WIDTH, D_MODEL), BF16),
                        pltpu.VMEM((D_MODEL, D_MODEL), BF16)],
        compiler_params=pltpu.CompilerParams(
            dimension_semantics=("arbitrary",),
            vmem_limit_bytes=VMEM_LIMIT_BYTES),
        name="merge_out",
    )(y_sb, y_hg, proj, proj, x2d, w_sb_out, w_hg_out, w_out, final_norm_g)


def _suffix_sum_weights():
    j = np.arange(2 * KEY_TILE)[:, None] % KEY_TILE
    s = np.arange(2 * KEY_TILE)[None, :]
    return jnp.asarray(np.where(s < KEY_TILE, j > s, True), BF16)


def _hg_sum_weights():
    r = np.arange(HG_CHUNK)[:, None]
    c = np.arange(HG_CHUNK)[None, :]
    groups = [c <= r]
    for m in HG_LEVELS:
        ref = (r // (2 * m)) * 2 * m + m - 1
        groups.append(np.where(r > ref, (c > ref) & (c <= r), (c > r) & (c <= ref)))
    w = np.concatenate(groups, axis=0)
    return jnp.asarray(np.concatenate([w, w], axis=1), BF16)


def kernel(x, meta, norm_g, w_in, w_sb_out, w_hg_out, w_out, hg_norm_g, hg_lb_logits, final_norm_g):
    batch, seq, d = x.shape
    assert d == D_MODEL and meta.shape == (N_META, D_MODEL)
    assert norm_g.shape[0] == 1 and w_in.shape == (1, D_MODEL, IN_COLS)
    assert seq % HG_ROWS == 0 and (batch * seq) % INPROJ_ROWS == 0

    x2d = x.reshape(batch * seq, D_MODEL)
    meta_blk = jnp.concatenate(
        [jnp.zeros((META_ROWS - N_META, D_MODEL), x.dtype), meta.astype(x.dtype)], axis=0)
    w_in2 = w_in.reshape(D_MODEL, IN_COLS)

    proj, f_proj = _inproj(x2d, norm_g, w_in2, INPROJ_ROWS)
    meta_cols = pl.cdiv((T_HG_GATE + 1) * COL_TILE, INPROJ_COLS) * INPROJ_COLS
    proj_meta, f_meta = _inproj(meta_blk, norm_g, w_in2, META_ROWS, n_cols=meta_cols)

    y_sb = _stick_breaking(proj, proj_meta, _suffix_sum_weights(), batch, seq)
    y_hg = _hgrn2(proj, f_proj, proj_meta, f_meta, hg_lb_logits, hg_norm_g,
                  _hg_sum_weights(), batch, seq, HG_ROWS)

    out = _final(y_sb.reshape(batch * seq, SB_WIDTH), y_hg.reshape(batch * seq, HG_WIDTH), proj,
                 x2d, w_sb_out.reshape(SB_WIDTH, D_MODEL), w_hg_out.reshape(HG_WIDTH, D_MODEL),
                 w_out.reshape(D_MODEL, D_MODEL), final_norm_g.reshape(1, D_MODEL), MERGE_ROWS)
    return out.reshape(batch, seq, D_MODEL)
```

```python
import functools

import jax
import jax.numpy as jnp
import numpy as np
from jax import lax
from jax.experimental import pallas as pl
from jax.experimental.pallas import tpu as pltpu

F32 = jnp.float32
BF16 = jnp.bfloat16

D_MODEL = 1024
N_META = 16
SB_HEADS = 8
SB_HEAD_DIM = 64
SB_WIDTH = SB_HEADS * SB_HEAD_DIM
SB_SCALE = SB_HEAD_DIM ** -0.5
HG_HEADS = 4
HG_DIM = 128
HG_WIDTH = HG_HEADS * HG_DIM
HG_SCALE = HG_DIM ** -0.5
IN_COLS = 4 * SB_WIDTH + 4 * HG_WIDTH + 2 * D_MODEL
EPS = 1e-6

LANES = 128
SUBLANES = 8
VMEM_LIMIT_BYTES = 56 * 1024 * 1024

INPROJ_ROWS = 2048
HG_ROWS = 1024
MERGE_ROWS = 1024

COL_TILE = 512
T_SB_Q, T_SB_K, T_SB_V, T_SB_GATE, T_HG_Q, T_HG_F, T_HG_I, T_HG_GATE = range(8)
N_COL_TILES = IN_COLS // COL_TILE

KEY_TILE = 128
META_ROWS = 128
HG_CHUNK = 64
HG_LEVELS = (1, 2, 4, 8, 16, 32)
HG_UNROLL = 16


_NT = (((1,), (1,)), ((), ()))
_TN = (((0,), (0,)), ((), ()))


def _sigmoid(x):
    return 1.0 / (1.0 + jnp.exp(-x))


def _split_bf16(x):
    hi = x.astype(BF16)
    lo = (x - hi.astype(F32)).astype(BF16)
    return hi, lo


INPROJ_COLS = 2 * COL_TILE
INPROJ_SLAB = 512
assert (T_SB_Q * COL_TILE) % INPROJ_COLS == 0


def _inproj_kernel(x_ref, g_ref, w_ref, o_ref, f_ref, u_ref):
    n = pl.program_id(1)
    f_step, f_col = divmod(T_HG_F * COL_TILE, INPROJ_COLS)
    assert f_step != 0

    def weights():
        lane = lax.broadcasted_iota(jnp.int32, (1, INPROJ_COLS), 1)
        is_q = jnp.logical_and(n == (T_SB_Q * COL_TILE) // INPROJ_COLS, lane < SB_WIDTH)
        return (w_ref[...] * jnp.where(is_q, SB_SCALE, 1.0).astype(F32)).astype(BF16)

    @pl.when(n == 0)
    def _():
        wb = weights()
        slab = min(INPROJ_SLAB, x_ref.shape[0])
        for s in range(x_ref.shape[0] // slab):
            rows = pl.ds(s * slab, slab)
            x = x_ref[rows, :]
            ms = jnp.mean(x * x, axis=-1, keepdims=True)
            u = (x * lax.rsqrt(ms + EPS) * g_ref[...]).astype(BF16)
            u_ref[rows, :] = u
            o_ref[rows, :] = jnp.dot(u, wb, preferred_element_type=F32).astype(BF16)

    @pl.when(n != 0)
    def _():
        p = jnp.dot(u_ref[...], weights(), preferred_element_type=F32)
        o_ref[...] = p.astype(BF16)

        @pl.when(n == f_step)
        def _():
            f_ref[...] = p[:, f_col:f_col + HG_WIDTH]


def _inproj(x2d, norm_g, w_in, row_tile, n_cols=IN_COLS):
    rows = x2d.shape[0]
    assert rows % row_tile == 0 and n_cols % INPROJ_COLS == 0 and n_cols >= (T_HG_F + 1) * COL_TILE
    return pl.pallas_call(
        _inproj_kernel,
        grid=(rows // row_tile, n_cols // INPROJ_COLS),
        in_specs=[
            pl.BlockSpec((row_tile, D_MODEL), lambda r, n: (r, 0)),
            pl.BlockSpec((1, D_MODEL), lambda r, n: (0, 0)),
            pl.BlockSpec((D_MODEL, INPROJ_COLS), lambda r, n: (0, n)),
        ],
        out_specs=[
            pl.BlockSpec((row_tile, INPROJ_COLS), lambda r, n: (r, n)),
            pl.BlockSpec((row_tile, HG_WIDTH), lambda r, n: (r, 0)),
        ],
        out_shape=[
            jax.ShapeDtypeStruct((rows, n_cols), BF16),
            jax.ShapeDtypeStruct((rows, HG_WIDTH), F32),
        ],
        scratch_shapes=[pltpu.VMEM((row_tile, D_MODEL), BF16)],
        compiler_params=pltpu.CompilerParams(
            dimension_semantics=("parallel", "arbitrary"),
            vmem_limit_bytes=VMEM_LIMIT_BYTES),
        name="inproj",
    )(x2d, norm_g, w_in)


SB_PAIRS = SB_WIDTH // LANES
SB_UNDERFLOW = -104.0
SB_QTILES = 4
SB_EARLY = 48


def _sb_kernel(q_ref, k_ref, v_ref, g_ref, mk_ref, mv_ref, uw_ref, o_ref, acc_ref, carry_ref):
    j = pl.program_id(1)
    lane = lax.broadcasted_iota(jnp.int32, (KEY_TILE, LANES), 1)
    first = lane < SB_HEAD_DIM
    uw = uw_ref[...]
    pair_lanes = [slice(p * LANES, (p + 1) * LANES) for p in range(SB_PAIRS)]
    q_rows = [slice(t * KEY_TILE, (t + 1) * KEY_TILE) for t in range(SB_QTILES)]
    q2 = []
    for qr in q_rows:
        for sl in pair_lanes:
            q = q_ref[qr, sl]
            zero = jnp.zeros_like(q)
            q2.append(jnp.concatenate([jnp.where(first, q, zero), jnp.where(first, zero, q)], axis=0))

    def early(x):
        return jnp.concatenate([x[:SB_EARLY], x[KEY_TILE:KEY_TILE + SB_EARLY]], axis=0)

    def with_early(x, e, combine):
        return jnp.concatenate([combine(x[:SB_EARLY], e[:SB_EARLY]), x[SB_EARLY:KEY_TILE],
                                combine(x[KEY_TILE:KEY_TILE + SB_EARLY], e[SB_EARLY:]),
                                x[KEY_TILE + SB_EARLY:]], axis=0)

    def log_terms(z, valid):
        log_beta = jnp.minimum(z, 0.0) - jnp.log(1.0 + jnp.exp(-jnp.abs(z)))
        log_1mb = log_beta - z
        if valid is not None:
            log_1mb = jnp.where(valid, log_1mb, 0.0)
        hi, lo = _split_bf16(log_1mb)
        return log_beta, jnp.concatenate([hi, lo], axis=1)

    def sweep(kt_ref, vt_ref, jobs, first_pass=False):
        chains = [(t * SB_PAIRS + p, sl) + job
                  for t, job in enumerate(jobs) for p, sl in enumerate(pair_lanes)]

        def scores(chain):
            c, sl, rows, _, erows = chain
            z = lax.dot_general(q2[c], kt_ref[rows, sl], _NT, preferred_element_type=F32)
            ze = None if erows is None else lax.dot_general(
                early(q2[c]), kt_ref[erows, sl], _NT, preferred_element_type=F32)
            return z, ze

        def suffix_sums(chain, z, ze):
            valids = chain[3]
            terms = [] if ze is None else [log_terms(ze, None)]
            terms += [log_terms(z[:, n * KEY_TILE:(n + 1) * KEY_TILE], valid)
                      for n, valid in enumerate(valids)]
            sums = jnp.dot(jnp.concatenate([hl for _, hl in terms], axis=0), uw,
                           preferred_element_type=F32)
            return [lb for lb, _ in terms], sums

        def weights(chain, log_betas, sums):
            c, _, _, valids, erows = chain
            off = 0 if erows is None else 2 * SB_EARLY
            lbs = log_betas if erows is None else log_betas[1:]
            carry = None if first_pass else carry_ref[c]
            w_n = [None] * len(valids)
            for n in reversed(range(len(valids))):
                cs = sums[off + n * 2 * KEY_TILE:off + (n + 1) * 2 * KEY_TILE]
                log_w = lbs[n] + cs[:, :KEY_TILE]
                w = jnp.exp(log_w if carry is None else log_w + carry)
                if valids[n] is not None:
                    w = jnp.where(valids[n], w, 0.0)
                w_n[n] = w.astype(BF16)
                carry = cs[:, KEY_TILE:] if carry is None else carry + cs[:, KEY_TILE:]
            we = None
            if erows is not None:
                cs = sums[:off]
                we = jnp.exp(log_betas[0] + cs[:, :KEY_TILE] + early(carry)).astype(BF16)
                carry = with_early(carry, cs[:, KEY_TILE:], jnp.add)
            carry_ref[c] = carry
            return jnp.concatenate(w_n, axis=1), we

        def accumulate(chain, w, we):
            c, sl, rows, _, erows = chain
            pv = jnp.dot(w, vt_ref[rows, sl], preferred_element_type=F32)
            if erows is not None:
                pv = with_early(pv, jnp.dot(we, vt_ref[erows, sl], preferred_element_type=F32),
                                jnp.add)
            if first_pass:
                acc_ref[c] = pv
            else:
                acc_ref[c] += pv

        stages = (scores, suffix_sums, weights, accumulate)
        held = [None] * len(chains)
        for k in range(len(chains) + len(stages) - 1):
            for s, stage in reversed(list(enumerate(stages))):
                i = k - s
                if 0 <= i < len(chains):
                    held[i] = stage(chains[i]) if s == 0 else stage(chains[i], *held[i])

    row2 = lax.broadcasted_iota(jnp.int32, (2 * KEY_TILE, KEY_TILE), 0) % KEY_TILE
    col2 = lax.broadcasted_iota(jnp.int32, (2 * KEY_TILE, KEY_TILE), 1)
    causal = col2 < row2

    def key_rows(tile, n_tiles):
        return pl.ds(pl.multiple_of(tile * KEY_TILE, KEY_TILE), n_tiles * KEY_TILE)

    i_first = j * SB_QTILES
    fused = j >= 1
    assert SB_QTILES >= 2

    @pl.when(fused)
    def _():
        sweep(k_ref, v_ref, [(key_rows(i_first + t - 1, 2), [None, causal],
                              key_rows(i_first + t - 2, 1)) for t in range(SB_QTILES)],
              first_pass=True)

    @pl.when(jnp.logical_not(fused))
    def _():
        sweep(k_ref, v_ref, [(key_rows(0, t + 1), [None] * t + [causal], None)
                             for t in range(SB_QTILES)], first_pass=True)

    def live_rows():
        top = jnp.max(carry_ref[...].reshape(-1, SUBLANES, LANES), axis=0)
        return (jnp.max(top) > SB_UNDERFLOW).astype(jnp.int32)

    left = [jnp.where(fused, i_first + t - 2, 0) for t in range(SB_QTILES)]
    late = row2 >= SB_EARLY

    def cond(c):
        it, live = c
        return jnp.logical_and(it < left[-1], live > 0)

    def body(c):
        it, _ = c
        catch_up = it < 0
        jobs = []
        for t in range(SB_QTILES):
            has = jnp.logical_or(catch_up, it < left[t])
            tile = jnp.where(catch_up, left[t], jnp.maximum(left[t] - 1 - it, 0))
            valid = jnp.logical_and(has, jnp.logical_or(late, jnp.logical_not(catch_up)))
            jobs.append((key_rows(tile, 1), [valid], None))
        sweep(k_ref, v_ref, jobs)
        return it + 1, live_rows()

    start = jnp.where(fused, -1, 0).astype(jnp.int32)
    _, live = lax.while_loop(cond, body, (start, live_rows()))

    @pl.when(live > 0)
    def _():
        sweep(mk_ref, mv_ref, [(slice(None), [col2 >= META_ROWS - N_META], None)] * SB_QTILES)

    for t, qr in enumerate(q_rows):
        for p, sl in enumerate(pair_lanes):
            acc = acc_ref[t * SB_PAIRS + p]
            out = jnp.where(first, acc[:KEY_TILE], acc[KEY_TILE:])
            gate = g_ref[qr, sl].astype(F32)
            o_ref[qr, sl] = (out * (gate * _sigmoid(gate))).astype(BF16)


def _stick_breaking(proj, proj_meta, uw, batch, seq):
    assert SB_WIDTH == COL_TILE
    proj3 = proj.reshape(batch, seq, IN_COLS)
    q_blk = SB_QTILES * KEY_TILE
    assert seq % q_blk == 0
    return pl.pallas_call(
        _sb_kernel,
        grid=(batch, seq // q_blk),
        in_specs=[
            pl.BlockSpec((None, q_blk, SB_WIDTH), lambda b, i: (b, i, T_SB_Q)),
            pl.BlockSpec((None, seq, SB_WIDTH), lambda b, i: (b, 0, T_SB_K)),
            pl.BlockSpec((None, seq, SB_WIDTH), lambda b, i: (b, 0, T_SB_V)),
            pl.BlockSpec((None, q_blk, SB_WIDTH), lambda b, i: (b, i, T_SB_GATE)),
            pl.BlockSpec((META_ROWS, SB_WIDTH), lambda b, i: (0, T_SB_K)),
            pl.BlockSpec((META_ROWS, SB_WIDTH), lambda b, i: (0, T_SB_V)),
            pl.BlockSpec((2 * KEY_TILE, 2 * KEY_TILE), lambda b, i: (0, 0)),
        ],
        out_specs=pl.BlockSpec((None, q_blk, SB_WIDTH), lambda b, i: (b, i, 0)),
        out_shape=jax.ShapeDtypeStruct((batch, seq, SB_WIDTH), BF16),
        scratch_shapes=[pltpu.VMEM((SB_QTILES * SB_PAIRS, 2 * KEY_TILE, LANES), F32),
                        pltpu.VMEM((SB_QTILES * SB_PAIRS, 2 * KEY_TILE, LANES), F32)],
        compiler_params=pltpu.CompilerParams(
            dimension_semantics=("parallel", "arbitrary"),
            vmem_limit_bytes=VMEM_LIMIT_BYTES),
        name="stick_breaking",
    )(proj3, proj3, proj3, proj3, proj_meta, proj_meta, uw)


def _hg_decay(f_logit, lbv, sum_w, pad_rows):
    f = lbv + (1.0 - lbv) * _sigmoid(f_logit)
    if pad_rows:
        row = lax.broadcasted_iota(jnp.int32, f.shape, 0)
        f = jnp.where(row >= pad_rows, f, 1.0)
    g_hi, g_lo = _split_bf16(jnp.log(f))
    sums = jnp.dot(sum_w, jnp.concatenate([g_hi, g_lo], axis=0), preferred_element_type=F32)
    return sums, 1.0 - f


def _hg_next_state(state_t, b, kk, v):
    b_last = b[HG_CHUNK - 1:HG_CHUNK, :]
    k_dec = (kk * jnp.exp(b_last - b)).astype(BF16)
    return state_t * jnp.exp(b_last) + lax.dot_general(v, k_dec, _TN, preferred_element_type=F32)


def _hg_scores(qf, kk, sums, masks):
    a = jnp.where(masks[0], lax.dot_general(qf.astype(BF16), kk.astype(BF16), _NT,
                                            preferred_element_type=F32), 0.0)
    for lvl in range(len(HG_LEVELS)):
        e = jnp.exp(sums[(1 + lvl) * HG_CHUNK:(2 + lvl) * HG_CHUNK])
        part = lax.dot_general((qf * e).astype(BF16), (kk * e).astype(BF16), _NT,
                               preferred_element_type=F32)
        a = jnp.where(masks[1 + lvl], part, a)
    return a


def _hg_score_masks():
    row = lax.broadcasted_iota(jnp.int32, (HG_CHUNK, HG_CHUNK), 0)
    col = lax.broadcasted_iota(jnp.int32, (HG_CHUNK, HG_CHUNK), 1)
    return [row == col] + [
        ((row // (2 * m)) == (col // (2 * m))) & ((row // m) % 2 == 1) & ((col // m) % 2 == 0)
        for m in HG_LEVELS]


def _hg_kernel(q_ref, f_ref, v_ref, gate_ref, mf_ref, mv_ref, lbl_ref, gn_ref,
               sumw_ref, o_ref, state_ref, *, chunks_per_tile):
    t = pl.program_id(1)
    sum_w = sumw_ref[...]
    lbl = lbl_ref[...]
    ex = jnp.exp(lbl - jnp.max(lbl, axis=0, keepdims=True))
    lb = ex[0:1, :] / jnp.sum(ex, axis=0, keepdims=True)
    gn = gn_ref[...]
    head_lanes = [slice(h * HG_DIM, (h + 1) * HG_DIM) for h in range(HG_HEADS)]

    @pl.when(t == 0)
    def _():
        for h, sl in enumerate(head_lanes):
            b, kk = _hg_decay(mf_ref[:, sl], lb[:, sl], sum_w[:HG_CHUNK], HG_CHUNK - N_META)
            state_ref[h] = _hg_next_state(jnp.zeros((HG_DIM, HG_DIM), F32), b, kk, mv_ref[:, sl])

    masks = _hg_score_masks()

    def body(c, carry):
        base = c * (HG_UNROLL * HG_CHUNK)
        rows = [pl.ds(pl.multiple_of(base + u * HG_CHUNK, HG_CHUNK), HG_CHUNK)
                for u in range(HG_UNROLL)]
        states = [state_ref[h] for h in range(HG_HEADS)]

        def decay(r):
            return [_hg_decay(f_ref[r, sl], lb[:, sl], sum_w, 0) for sl in head_lanes]

        def mix(r, dec):
            out = []
            for h, sl in enumerate(head_lanes):
                sums, kk = dec[h]
                b = sums[:HG_CHUNK]
                q = q_ref[r, sl].astype(F32)
                qf = q * _sigmoid(q) * HG_SCALE
                scores = _hg_scores(qf, kk, sums, masks)
                inter = lax.dot_general((qf * jnp.exp(b)).astype(BF16), states[h].astype(BF16),
                                        _NT, preferred_element_type=F32)
                states[h] = _hg_next_state(states[h], b, kk, v_ref[r, sl])
                out.append((scores, inter))
            return out

        def emit(r, mixed):
            for h, sl in enumerate(head_lanes):
                scores, inter = mixed[h]
                o = inter + jnp.dot(scores.astype(BF16), v_ref[r, sl], preferred_element_type=F32)
                o = o * lax.rsqrt(jnp.mean(o * o, axis=-1, keepdims=True) + EPS) * gn[:, sl]
                gate = gate_ref[r, sl].astype(F32)
                o_ref[r, sl] = (o * (gate * _sigmoid(gate))).astype(BF16)

        stages = (decay, mix, emit)
        held = [None] * len(rows)
        for k in range(len(rows) + len(stages) - 1):
            for s in reversed(range(len(stages))):
                i = k - s
                if 0 <= i < len(rows):
                    held[i] = stages[s](rows[i]) if s == 0 else stages[s](rows[i], held[i])
        for h in range(HG_HEADS):
            state_ref[h] = states[h]
        return carry

    lax.fori_loop(0, chunks_per_tile // HG_UNROLL, body, 0)


def _hgrn2(proj, f_proj, proj_meta, f_meta, lb_logits, hg_norm_g, sum_w, batch, seq, row_tile):
    assert seq % row_tile == 0 and row_tile % HG_CHUNK == 0
    proj3 = proj.reshape(batch, seq, IN_COLS)
    f3 = f_proj.reshape(batch, seq, HG_WIDTH)
    meta_blk = META_ROWS // HG_CHUNK - 1
    const = lambda b, t: (0, 0)
    return pl.pallas_call(
        functools.partial(_hg_kernel, chunks_per_tile=row_tile // HG_CHUNK),
        grid=(batch, seq // row_tile),
        in_specs=[
            pl.BlockSpec((None, row_tile, HG_WIDTH), lambda b, t: (b, t, T_HG_Q)),
            pl.BlockSpec((None, row_tile, HG_WIDTH), lambda b, t: (b, t, 0)),
            pl.BlockSpec((None, row_tile, HG_WIDTH), lambda b, t: (b, t, T_HG_I)),
            pl.BlockSpec((None, row_tile, HG_WIDTH), lambda b, t: (b, t, T_HG_GATE)),
            pl.BlockSpec((HG_CHUNK, HG_WIDTH), lambda b, t: (meta_blk, 0)),
            pl.BlockSpec((HG_CHUNK, HG_WIDTH), lambda b, t: (meta_blk, T_HG_I)),
            pl.BlockSpec(lb_logits.shape, const),
            pl.BlockSpec((1, HG_WIDTH), const),
            pl.BlockSpec(sum_w.shape, const),
        ],
        out_specs=pl.BlockSpec((None, row_tile, HG_WIDTH), lambda b, t: (b, t, 0)),
        out_shape=jax.ShapeDtypeStruct((batch, seq, HG_WIDTH), BF16),
        scratch_shapes=[pltpu.VMEM((HG_HEADS, HG_DIM, HG_DIM), F32)],
        compiler_params=pltpu.CompilerParams(
            dimension_semantics=("parallel", "arbitrary"),
            vmem_limit_bytes=VMEM_LIMIT_BYTES),
        name="hgrn2",
    )(proj3, f3, proj3, proj3, f_meta, proj_meta, lb_logits, hg_norm_g, sum_w)


MERGE_SLAB = 256


def _final_kernel(ysb_ref, yhg_ref, gsb_ref, ghg_ref, x_ref, wsb_ref, whg_ref, wout_ref, fg_ref,
                  o_ref, wsb_s, whg_s, wout_s):
    @pl.when(pl.program_id(0) == 0)
    def _():
        wsb_s[...] = wsb_ref[...].astype(BF16)
        whg_s[...] = whg_ref[...].astype(BF16)
        wout_s[...] = wout_ref[...].astype(BF16)

    n_slab = o_ref.shape[0] // MERGE_SLAB
    slabs = [pl.ds(s * MERGE_SLAB, MERGE_SLAB) for s in range(n_slab)]
    a = [jnp.dot(ysb_ref[sl, :], wsb_s[...], preferred_element_type=F32) for sl in slabs]
    b = [jnp.dot(yhg_ref[sl, :], whg_s[...], preferred_element_type=F32) for sl in slabs]
    merged = [(_sigmoid(gsb_ref[sl, :].astype(F32)) * a[s]
               + _sigmoid(ghg_ref[sl, :].astype(F32)) * b[s]).astype(BF16)
              for s, sl in enumerate(slabs)]
    h = [x_ref[sl, :] + jnp.dot(merged[s], wout_s[...], preferred_element_type=F32)
         for s, sl in enumerate(slabs)]
    for s, sl in enumerate(slabs):
        ms = jnp.mean(h[s] * h[s], axis=-1, keepdims=True)
        o_ref[sl, :] = h[s] * lax.rsqrt(ms + EPS) * fg_ref[...]


def _final(y_sb, y_hg, proj, x2d, w_sb_out, w_hg_out, w_out, final_norm_g, row_tile):
    rows = x2d.shape[0]
    assert rows % row_tile == 0
    gate_blk = (4 * SB_WIDTH + 4 * HG_WIDTH) // D_MODEL
    const = lambda r: (0, 0)
    return pl.pallas_call(
        _final_kernel,
        grid=(rows // row_tile,),
        in_specs=[
            pl.BlockSpec((row_tile, SB_WIDTH), lambda r: (r, 0)),
            pl.BlockSpec((row_tile, HG_WIDTH), lambda r: (r, 0)),
            pl.BlockSpec((row_tile, D_MODEL), lambda r: (r, gate_blk)),
            pl.BlockSpec((row_tile, D_MODEL), lambda r: (r, gate_blk + 1)),
            pl.BlockSpec((row_tile, D_MODEL), lambda r: (r, 0)),
            pl.BlockSpec((SB_WIDTH, D_MODEL), const),
            pl.BlockSpec((HG_WIDTH, D_MODEL), const),
            pl.BlockSpec((D_MODEL, D_MODEL), const),
            pl.BlockSpec((1, D_MODEL), const),
        ],
        out_specs=pl.BlockSpec((row_tile, D_MODEL), lambda r: (r, 0)),
        out_shape=jax.ShapeDtypeStruct((rows, D_MODEL), F32),
        scratch_shapes=[pltpu.VMEM((SB_WIDTH, D_MODEL), BF16),
                        pltpu.VMEM((HG_WIDTH, D_MODEL), BF16),
                        pltpu.VMEM((D_MODEL, D_MODEL), BF16)],
        compiler_params=pltpu.CompilerParams(
            dimension_semantics=("arbitrary",),
            vmem_limit_bytes=VMEM_LIMIT_BYTES),
        name="merge_out",
    )(y_sb, y_hg, proj, proj, x2d, w_sb_out, w_hg_out, w_out, final_norm_g)


def _suffix_sum_weights():
    j = np.arange(2 * KEY_TILE)[:, None] % KEY_TILE
    s = np.arange(2 * KEY_TILE)[None, :]
    return jnp.asarray(np.where(s < KEY_TILE, j > s, True), BF16)


def _hg_sum_weights():
    r = np.arange(HG_CHUNK)[:, None]
    c = np.arange(HG_CHUNK)[None, :]
    groups = [c <= r]
    for m in HG_LEVELS:
        ref = (r // (2 * m)) * 2 * m + m - 1
        groups.append(np.where(r > ref, (c > ref) & (c <= r), (c > r) & (c <= ref)))
    w = np.concatenate(groups, axis=0)
    return jnp.asarray(np.concatenate([w, w], axis=1), BF16)


def kernel(x, meta, norm_g, w_in, w_sb_out, w_hg_out, w_out, hg_norm_g, hg_lb_logits, final_norm_g):
    batch, seq, d = x.shape
    assert d == D_MODEL and meta.shape == (N_META, D_MODEL)
    assert norm_g.shape[0] == 1 and w_in.shape == (1, D_MODEL, IN_COLS)
    assert seq % HG_ROWS == 0 and (batch * seq) % INPROJ_ROWS == 0

    x2d = x.reshape(batch * seq, D_MODEL)
    meta_blk = jnp.concatenate(
        [jnp.zeros((META_ROWS - N_META, D_MODEL), x.dtype), meta.astype(x.dtype)], axis=0)
    w_in2 = w_in.reshape(D_MODEL, IN_COLS)

    proj, f_proj = _inproj(x2d, norm_g, w_in2, INPROJ_ROWS)
    meta_cols = pl.cdiv((T_HG_GATE + 1) * COL_TILE, INPROJ_COLS) * INPROJ_COLS
    proj_meta, f_meta = _inproj(meta_blk, norm_g, w_in2, META_ROWS, n_cols=meta_cols)

    y_sb = _stick_breaking(proj, proj_meta, _suffix_sum_weights(), batch, seq)
    y_hg = _hgrn2(proj, f_proj, proj_meta, f_meta, hg_lb_logits, hg_norm_g,
                  _hg_sum_weights(), batch, seq, HG_ROWS)

    out = _final(y_sb.reshape(batch * seq, SB_WIDTH), y_hg.reshape(batch * seq, HG_WIDTH), proj,
                 x2d, w_sb_out.reshape(SB_WIDTH, D_MODEL), w_hg_out.reshape(HG_WIDTH, D_MODEL),
                 w_out.reshape(D_MODEL, D_MODEL), final_norm_g.reshape(1, D_MODEL), MERGE_ROWS)
    return out.reshape(batch, seq, D_MODEL)
```

```python
import functools

import jax
import jax.numpy as jnp
import numpy as np
from jax import lax
from jax.experimental import pallas as pl
from jax.experimental.pallas import tpu as pltpu

F32 = jnp.float32
BF16 = jnp.bfloat16

D_MODEL = 1024
N_META = 16
SB_HEADS = 8
SB_HEAD_DIM = 64
SB_WIDTH = SB_HEADS * SB_HEAD_DIM
SB_SCALE = SB_HEAD_DIM ** -0.5
HG_HEADS = 4
HG_DIM = 128
HG_WIDTH = HG_HEADS * HG_DIM
HG_SCALE = HG_DIM ** -0.5
IN_COLS = 4 * SB_WIDTH + 4 * HG_WIDTH + 2 * D_MODEL
EPS = 1e-6

LANES = 128
SUBLANES = 8
VMEM_LIMIT_BYTES = 56 * 1024 * 1024

INPROJ_ROWS = 2048
HG_ROWS = 2048
MERGE_ROWS = 1024

COL_TILE = 512
T_SB_Q, T_SB_K, T_SB_V, T_SB_GATE, T_HG_Q, T_HG_F, T_HG_I, T_HG_GATE = range(8)
N_COL_TILES = IN_COLS // COL_TILE

KEY_TILE = 128
META_ROWS = 128
HG_CHUNK = 64
HG_LEVELS = (1, 2, 4, 8, 16, 32)
HG_UNROLL = 32


_NT = (((1,), (1,)), ((), ()))
_TN = (((0,), (0,)), ((), ()))


def _sigmoid(x):
    return 1.0 / (1.0 + jnp.exp(-x))


def _split_bf16(x):
    hi = x.astype(BF16)
    lo = (x - hi.astype(F32)).astype(BF16)
    return hi, lo


INPROJ_COLS = 2 * COL_TILE
INPROJ_SLAB = 512
assert (T_SB_Q * COL_TILE) % INPROJ_COLS == 0


def _inproj_kernel(x_ref, g_ref, w_ref, o_ref, f_ref, u_ref):
    n = pl.program_id(1)
    f_step, f_col = divmod(T_HG_F * COL_TILE, INPROJ_COLS)
    assert f_step != 0

    def weights():
        lane = lax.broadcasted_iota(jnp.int32, (1, INPROJ_COLS), 1)
        is_q = jnp.logical_and(n == (T_SB_Q * COL_TILE) // INPROJ_COLS, lane < SB_WIDTH)
        return (w_ref[...] * jnp.where(is_q, SB_SCALE, 1.0).astype(F32)).astype(BF16)

    @pl.when(n == 0)
    def _():
        wb = weights()
        slab = min(INPROJ_SLAB, x_ref.shape[0])
        for s in range(x_ref.shape[0] // slab):
            rows = pl.ds(s * slab, slab)
            x = x_ref[rows, :]
            ms = jnp.mean(x * x, axis=-1, keepdims=True)
            u = (x * lax.rsqrt(ms + EPS) * g_ref[...]).astype(BF16)
            u_ref[rows, :] = u
            o_ref[rows, :] = jnp.dot(u, wb, preferred_element_type=F32).astype(BF16)

    @pl.when(n != 0)
    def _():
        p = jnp.dot(u_ref[...], weights(), preferred_element_type=F32)
        o_ref[...] = p.astype(BF16)

        @pl.when(n == f_step)
        def _():
            f_ref[...] = p[:, f_col:f_col + HG_WIDTH]


def _inproj(x2d, norm_g, w_in, row_tile, n_cols=IN_COLS):
    rows = x2d.shape[0]
    assert rows % row_tile == 0 and n_cols % INPROJ_COLS == 0 and n_cols >= (T_HG_F + 1) * COL_TILE
    return pl.pallas_call(
        _inproj_kernel,
        grid=(rows // row_tile, n_cols // INPROJ_COLS),
        in_specs=[
            pl.BlockSpec((row_tile, D_MODEL), lambda r, n: (r, 0)),
            pl.BlockSpec((1, D_MODEL), lambda r, n: (0, 0)),
            pl.BlockSpec((D_MODEL, INPROJ_COLS), lambda r, n: (0, n)),
        ],
        out_specs=[
            pl.BlockSpec((row_tile, INPROJ_COLS), lambda r, n: (r, n)),
            pl.BlockSpec((row_tile, HG_WIDTH), lambda r, n: (r, 0)),
        ],
        out_shape=[
            jax.ShapeDtypeStruct((rows, n_cols), BF16),
            jax.ShapeDtypeStruct((rows, HG_WIDTH), F32),
        ],
        scratch_shapes=[pltpu.VMEM((row_tile, D_MODEL), BF16)],
        compiler_params=pltpu.CompilerParams(
            dimension_semantics=("parallel", "arbitrary"),
            vmem_limit_bytes=VMEM_LIMIT_BYTES),
        name="inproj",
    )(x2d, norm_g, w_in)


SB_PAIRS = SB_WIDTH // LANES
SB_UNDERFLOW = -104.0
SB_QTILES = 4
SB_EARLY = 48


def _sb_kernel(q_ref, k_ref, v_ref, g_ref, mk_ref, mv_ref, uw_ref, o_ref, acc_ref, carry_ref):
    j = pl.program_id(1)
    lane = lax.broadcasted_iota(jnp.int32, (KEY_TILE, LANES), 1)
    first = lane < SB_HEAD_DIM
    uw = uw_ref[...]
    pair_lanes = [slice(p * LANES, (p + 1) * LANES) for p in range(SB_PAIRS)]
    q_rows = [slice(t * KEY_TILE, (t + 1) * KEY_TILE) for t in range(SB_QTILES)]
    q2 = []
    for qr in q_rows:
        for sl in pair_lanes:
            q = q_ref[qr, sl]
            zero = jnp.zeros_like(q)
            q2.append(jnp.concatenate([jnp.where(first, q, zero), jnp.where(first, zero, q)], axis=0))

    def early(x):
        return jnp.concatenate([x[:SB_EARLY], x[KEY_TILE:KEY_TILE + SB_EARLY]], axis=0)

    def with_early(x, e, combine):
        return jnp.concatenate([combine(x[:SB_EARLY], e[:SB_EARLY]), x[SB_EARLY:KEY_TILE],
                                combine(x[KEY_TILE:KEY_TILE + SB_EARLY], e[SB_EARLY:]),
                                x[KEY_TILE + SB_EARLY:]], axis=0)

    def log_terms(z, valid):
        log_beta = jnp.minimum(z, 0.0) - jnp.log(1.0 + jnp.exp(-jnp.abs(z)))
        log_1mb = log_beta - z
        if valid is not None:
            log_1mb = jnp.where(valid, log_1mb, 0.0)
        hi, lo = _split_bf16(log_1mb)
        return log_beta, jnp.concatenate([hi, lo], axis=1)

    def sweep(kt_ref, vt_ref, jobs, first_pass=False):
        chains = [(t * SB_PAIRS + p, sl) + job
                  for t, job in enumerate(jobs) for p, sl in enumerate(pair_lanes)]

        def scores(chain):
            c, sl, rows, _, erows = chain
            z = lax.dot_general(q2[c], kt_ref[rows, sl], _NT, preferred_element_type=F32)
            ze = None if erows is None else lax.dot_general(
                early(q2[c]), kt_ref[erows, sl], _NT, preferred_element_type=F32)
            return z, ze

        def suffix_sums(chain, z, ze):
            valids = chain[3]
            terms = [] if ze is None else [log_terms(ze, None)]
            terms += [log_terms(z[:, n * KEY_TILE:(n + 1) * KEY_TILE], valid)
                      for n, valid in enumerate(valids)]
            sums = jnp.dot(jnp.concatenate([hl for _, hl in terms], axis=0), uw,
                           preferred_element_type=F32)
            return [lb for lb, _ in terms], sums

        def weights(chain, log_betas, sums):
            c, _, _, valids, erows = chain
            off = 0 if erows is None else 2 * SB_EARLY
            lbs = log_betas if erows is None else log_betas[1:]
            carry = None if first_pass else carry_ref[c]
            w_n = [None] * len(valids)
            for n in reversed(range(len(valids))):
                cs = sums[off + n * 2 * KEY_TILE:off + (n + 1) * 2 * KEY_TILE]
                log_w = lbs[n] + cs[:, :KEY_TILE]
                w = jnp.exp(log_w if carry is None else log_w + carry)
                if valids[n] is not None:
                    w = jnp.where(valids[n], w, 0.0)
                w_n[n] = w.astype(BF16)
                carry = cs[:, KEY_TILE:] if carry is None else carry + cs[:, KEY_TILE:]
            we = None
            if erows is not None:
                cs = sums[:off]
                we = jnp.exp(log_betas[0] + cs[:, :KEY_TILE] + early(carry)).astype(BF16)
                carry = with_early(carry, cs[:, KEY_TILE:], jnp.add)
            carry_ref[c] = carry
            return jnp.concatenate(w_n, axis=1), we

        def accumulate(chain, w, we):
            c, sl, rows, _, erows = chain
            pv = jnp.dot(w, vt_ref[rows, sl], preferred_element_type=F32)
            if erows is not None:
                pv = with_early(pv, jnp.dot(we, vt_ref[erows, sl], preferred_element_type=F32),
                                jnp.add)
            if first_pass:
                acc_ref[c] = pv
            else:
                acc_ref[c] += pv

        stages = (scores, suffix_sums, weights, accumulate)
        held = [None] * len(chains)
        for k in range(len(chains) + len(stages) - 1):
            for s, stage in reversed(list(enumerate(stages))):
                i = k - s
                if 0 <= i < len(chains):
                    held[i] = stage(chains[i]) if s == 0 else stage(chains[i], *held[i])

    row2 = lax.broadcasted_iota(jnp.int32, (2 * KEY_TILE, KEY_TILE), 0) % KEY_TILE
    col2 = lax.broadcasted_iota(jnp.int32, (2 * KEY_TILE, KEY_TILE), 1)
    causal = col2 < row2

    def key_rows(tile, n_tiles):
        return pl.ds(pl.multiple_of(tile * KEY_TILE, KEY_TILE), n_tiles * KEY_TILE)

    i_first = j * SB_QTILES
    fused = j >= 1
    assert SB_QTILES >= 2

    @pl.when(fused)
    def _():
        sweep(k_ref, v_ref, [(key_rows(i_first + t - 1, 2), [None, causal],
                              key_rows(i_first + t - 2, 1)) for t in range(SB_QTILES)],
              first_pass=True)

    @pl.when(jnp.logical_not(fused))
    def _():
        sweep(k_ref, v_ref, [(key_rows(0, t + 1), [None] * t + [causal], None)
                             for t in range(SB_QTILES)], first_pass=True)

    def live_rows():
        top = jnp.max(carry_ref[...].reshape(-1, SUBLANES, LANES), axis=0)
        return (jnp.max(top) > SB_UNDERFLOW).astype(jnp.int32)

    left = [jnp.where(fused, i_first + t - 2, 0) for t in range(SB_QTILES)]
    late = row2 >= SB_EARLY

    def cond(c):
        it, live = c
        return jnp.logical_and(it < left[-1], live > 0)

    def body(c):
        it, _ = c
        catch_up = it < 0
        jobs = []
        for t in range(SB_QTILES):
            has = jnp.logical_or(catch_up, it < left[t])
            tile = jnp.where(catch_up, left[t], jnp.maximum(left[t] - 1 - it, 0))
            valid = jnp.logical_and(has, jnp.logical_or(late, jnp.logical_not(catch_up)))
            jobs.append((key_rows(tile, 1), [valid], None))
        sweep(k_ref, v_ref, jobs)
        return it + 1, live_rows()

    start = jnp.where(fused, -1, 0).astype(jnp.int32)
    _, live = lax.while_loop(cond, body, (start, live_rows()))

    @pl.when(live > 0)
    def _():
        sweep(mk_ref, mv_ref, [(slice(None), [col2 >= META_ROWS - N_META], None)] * SB_QTILES)

    for t, qr in enumerate(q_rows):
        for p, sl in enumerate(pair_lanes):
            acc = acc_ref[t * SB_PAIRS + p]
            out = jnp.where(first, acc[:KEY_TILE], acc[KEY_TILE:])
            gate = g_ref[qr, sl].astype(F32)
            o_ref[qr, sl] = (out * (gate * _sigmoid(gate))).astype(BF16)


def _stick_breaking(proj, proj_meta, uw, batch, seq):
    assert SB_WIDTH == COL_TILE
    proj3 = proj.reshape(batch, seq, IN_COLS)
    q_blk = SB_QTILES * KEY_TILE
    assert seq % q_blk == 0
    return pl.pallas_call(
        _sb_kernel,
        grid=(batch, seq // q_blk),
        in_specs=[
            pl.BlockSpec((None, q_blk, SB_WIDTH), lambda b, i: (b, i, T_SB_Q)),
            pl.BlockSpec((None, seq, SB_WIDTH), lambda b, i: (b, 0, T_SB_K)),
            pl.BlockSpec((None, seq, SB_WIDTH), lambda b, i: (b, 0, T_SB_V)),
            pl.BlockSpec((None, q_blk, SB_WIDTH), lambda b, i: (b, i, T_SB_GATE)),
            pl.BlockSpec((META_ROWS, SB_WIDTH), lambda b, i: (0, T_SB_K)),
            pl.BlockSpec((META_ROWS, SB_WIDTH), lambda b, i: (0, T_SB_V)),
            pl.BlockSpec((2 * KEY_TILE, 2 * KEY_TILE), lambda b, i: (0, 0)),
        ],
        out_specs=pl.BlockSpec((None, q_blk, SB_WIDTH), lambda b, i: (b, i, 0)),
        out_shape=jax.ShapeDtypeStruct((batch, seq, SB_WIDTH), BF16),
        scratch_shapes=[pltpu.VMEM((SB_QTILES * SB_PAIRS, 2 * KEY_TILE, LANES), F32),
                        pltpu.VMEM((SB_QTILES * SB_PAIRS, 2 * KEY_TILE, LANES), F32)],
        compiler_params=pltpu.CompilerParams(
            dimension_semantics=("parallel", "arbitrary"),
            vmem_limit_bytes=VMEM_LIMIT_BYTES),
        name="stick_breaking",
    )(proj3, proj3, proj3, proj3, proj_meta, proj_meta, uw)


def _hg_decay(f_logit, lbv, sum_w, pad_rows):
    f = lbv + (1.0 - lbv) * _sigmoid(f_logit)
    if pad_rows:
        row = lax.broadcasted_iota(jnp.int32, f.shape, 0)
        f = jnp.where(row >= pad_rows, f, 1.0)
    g_hi, g_lo = _split_bf16(jnp.log(f))
    sums = jnp.dot(sum_w, jnp.concatenate([g_hi, g_lo], axis=0), preferred_element_type=F32)
    return sums, 1.0 - f


def _hg_next_state(state_t, b, kk, v):
    b_last = b[HG_CHUNK - 1:HG_CHUNK, :]
    k_dec = (kk * jnp.exp(b_last - b)).astype(BF16)
    return state_t * jnp.exp(b_last) + lax.dot_general(v, k_dec, _TN, preferred_element_type=F32)


def _hg_scores(qf, kk, sums, masks):
    a = jnp.where(masks[0], lax.dot_general(qf.astype(BF16), kk.astype(BF16), _NT,
                                            preferred_element_type=F32), 0.0)
    for lvl in range(len(HG_LEVELS)):
        e = jnp.exp(sums[(1 + lvl) * HG_CHUNK:(2 + lvl) * HG_CHUNK])
        part = lax.dot_general((qf * e).astype(BF16), (kk * e).astype(BF16), _NT,
                               preferred_element_type=F32)
        a = jnp.where(masks[1 + lvl], part, a)
    return a


def _hg_score_masks():
    row = lax.broadcasted_iota(jnp.int32, (HG_CHUNK, HG_CHUNK), 0)
    col = lax.broadcasted_iota(jnp.int32, (HG_CHUNK, HG_CHUNK), 1)
    return [row == col] + [
        ((row // (2 * m)) == (col // (2 * m))) & ((row // m) % 2 == 1) & ((col // m) % 2 == 0)
        for m in HG_LEVELS]


def _hg_kernel(q_ref, f_ref, v_ref, gate_ref, mf_ref, mv_ref, lbl_ref, gn_ref,
               sumw_ref, o_ref, state_ref, *, chunks_per_tile):
    t = pl.program_id(1)
    sum_w = sumw_ref[...]
    lbl = lbl_ref[...]
    ex = jnp.exp(lbl - jnp.max(lbl, axis=0, keepdims=True))
    lb = ex[0:1, :] / jnp.sum(ex, axis=0, keepdims=True)
    gn = gn_ref[...]
    head_lanes = [slice(h * HG_DIM, (h + 1) * HG_DIM) for h in range(HG_HEADS)]

    @pl.when(t == 0)
    def _():
        for h, sl in enumerate(head_lanes):
            b, kk = _hg_decay(mf_ref[:, sl], lb[:, sl], sum_w[:HG_CHUNK], HG_CHUNK - N_META)
            state_ref[h] = _hg_next_state(jnp.zeros((HG_DIM, HG_DIM), F32), b, kk, mv_ref[:, sl])

    masks = _hg_score_masks()

    def body(c, carry):
        base = c * (HG_UNROLL * HG_CHUNK)
        rows = [pl.ds(pl.multiple_of(base + u * HG_CHUNK, HG_CHUNK), HG_CHUNK)
                for u in range(HG_UNROLL)]
        states = [state_ref[h] for h in range(HG_HEADS)]

        def decay(r):
            return [_hg_decay(f_ref[r, sl], lb[:, sl], sum_w, 0) for sl in head_lanes]

        def mix(r, dec):
            out = []
            for h, sl in enumerate(head_lanes):
                sums, kk = dec[h]
                b = sums[:HG_CHUNK]
                q = q_ref[r, sl].astype(F32)
                qf = q * _sigmoid(q) * HG_SCALE
                scores = _hg_scores(qf, kk, sums, masks)
                inter = lax.dot_general((qf * jnp.exp(b)).astype(BF16), states[h].astype(BF16),
                                        _NT, preferred_element_type=F32)
                states[h] = _hg_next_state(states[h], b, kk, v_ref[r, sl])
                out.append((scores, inter))
            return out

        def emit(r, mixed):
            for h, sl in enumerate(head_lanes):
                scores, inter = mixed[h]
                o = inter + jnp.dot(scores.astype(BF16), v_ref[r, sl], preferred_element_type=F32)
                o = o * lax.rsqrt(jnp.mean(o * o, axis=-1, keepdims=True) + EPS) * gn[:, sl]
                gate = gate_ref[r, sl].astype(F32)
                o_ref[r, sl] = (o * (gate * _sigmoid(gate))).astype(BF16)

        stages = (decay, mix, emit)
        held = [None] * len(rows)
        for k in range(len(rows) + len(stages) - 1):
            for s in reversed(range(len(stages))):
                i = k - s
                if 0 <= i < len(rows):
                    held[i] = stages[s](rows[i]) if s == 0 else stages[s](rows[i], held[i])
        for h in range(HG_HEADS):
            state_ref[h] = states[h]
        return carry

    lax.fori_loop(0, chunks_per_tile // HG_UNROLL, body, 0)


def _hgrn2(proj, f_proj, proj_meta, f_meta, lb_logits, hg_norm_g, sum_w, batch, seq, row_tile):
    assert seq % row_tile == 0 and row_tile % HG_CHUNK == 0
    proj3 = proj.reshape(batch, seq, IN_COLS)
    f3 = f_proj.reshape(batch, seq, HG_WIDTH)
    meta_blk = META_ROWS // HG_CHUNK - 1
    const = lambda b, t: (0, 0)
    return pl.pallas_call(
        functools.partial(_hg_kernel, chunks_per_tile=row_tile // HG_CHUNK),
        grid=(batch, seq // row_tile),
        in_specs=[
            pl.BlockSpec((None, row_tile, HG_WIDTH), lambda b, t: (b, t, T_HG_Q)),
            pl.BlockSpec((None, row_tile, HG_WIDTH), lambda b, t: (b, t, 0)),
            pl.BlockSpec((None, row_tile, HG_WIDTH), lambda b, t: (b, t, T_HG_I)),
            pl.BlockSpec((None, row_tile, HG_WIDTH), lambda b, t: (b, t, T_HG_GATE)),
            pl.BlockSpec((HG_CHUNK, HG_WIDTH), lambda b, t: (meta_blk, 0)),
            pl.BlockSpec((HG_CHUNK, HG_WIDTH), lambda b, t: (meta_blk, T_HG_I)),
            pl.BlockSpec(lb_logits.shape, const),
            pl.BlockSpec((1, HG_WIDTH), const),
            pl.BlockSpec(sum_w.shape, const),
        ],
        out_specs=pl.BlockSpec((None, row_tile, HG_WIDTH), lambda b, t: (b, t, 0)),
        out_shape=jax.ShapeDtypeStruct((batch, seq, HG_WIDTH), BF16),
        scratch_shapes=[pltpu.VMEM((HG_HEADS, HG_DIM, HG_DIM), F32)],
        compiler_params=pltpu.CompilerParams(
            dimension_semantics=("parallel", "arbitrary"),
            vmem_limit_bytes=VMEM_LIMIT_BYTES),
        name="hgrn2",
    )(proj3, f3, proj3, proj3, f_meta, proj_meta, lb_logits, hg_norm_g, sum_w)


MERGE_SLAB = 256


def _final_kernel(ysb_ref, yhg_ref, gsb_ref, ghg_ref, x_ref, wsb_ref, whg_ref, wout_ref, fg_ref,
                  o_ref, wsb_s, whg_s, wout_s):
    @pl.when(pl.program_id(0) == 0)
    def _():
        wsb_s[...] = wsb_ref[...].astype(BF16)
        whg_s[...] = whg_ref[...].astype(BF16)
        wout_s[...] = wout_ref[...].astype(BF16)

    n_slab = o_ref.shape[0] // MERGE_SLAB
    slabs = [pl.ds(s * MERGE_SLAB, MERGE_SLAB) for s in range(n_slab)]
    a = [jnp.dot(ysb_ref[sl, :], wsb_s[...], preferred_element_type=F32) for sl in slabs]
    b = [jnp.dot(yhg_ref[sl, :], whg_s[...], preferred_element_type=F32) for sl in slabs]
    merged = [(_sigmoid(gsb_ref[sl, :].astype(F32)) * a[s]
               + _sigmoid(ghg_ref[sl, :].astype(F32)) * b[s]).astype(BF16)
              for s, sl in enumerate(slabs)]
    h = [x_ref[sl, :] + jnp.dot(merged[s], wout_s[...], preferred_element_type=F32)
         for s, sl in enumerate(slabs)]
    for s, sl in enumerate(slabs):
        ms = jnp.mean(h[s] * h[s], axis=-1, keepdims=True)
        o_ref[sl, :] = h[s] * lax.rsqrt(ms + EPS) * fg_ref[...]


def _final(y_sb, y_hg, proj, x2d, w_sb_out, w_hg_out, w_out, final_norm_g, row_tile):
    rows = x2d.shape[0]
    assert rows % row_tile == 0
    gate_blk = (4 * SB_WIDTH + 4 * HG_WIDTH) // D_MODEL
    const = lambda r: (0, 0)
    return pl.pallas_call(
        _final_kernel,
        grid=(rows // row_tile,),
        in_specs=[
            pl.BlockSpec((row_tile, SB_WIDTH), lambda r: (r, 0)),
            pl.BlockSpec((row_tile, HG_WIDTH), lambda r: (r, 0)),
            pl.BlockSpec((row_tile, D_MODEL), lambda r: (r, gate_blk)),
            pl.BlockSpec((row_tile, D_MODEL), lambda r: (r, gate_blk + 1)),
            pl.BlockSpec((row_tile, D_MODEL), lambda r: (r, 0)),
            pl.BlockSpec((SB_WIDTH, D_MODEL), const),
            pl.BlockSpec((HG_WIDTH, D_MODEL), const),
            pl.BlockSpec((D_MODEL, D_MODEL), const),
            pl.BlockSpec((1, D_MODEL), const),
        ],
        out_specs=pl.BlockSpec((row_tile, D_MODEL), lambda r: (r, 0)),
        out_shape=jax.ShapeDtypeStruct((rows, D_MODEL), F32),
        scratch_shapes=[pltpu.VMEM((SB_WIDTH, D_MODEL), BF16),
                        pltpu.VMEM((HG_WIDTH, D_MODEL), BF16),
                        pltpu.VMEM((D_MODEL, D_MODEL), BF16)],
        compiler_params=pltpu.CompilerParams(
            dimension_semantics=("arbitrary",),
            vmem_limit_bytes=VMEM_LIMIT_BYTES),
        name="merge_out",
    )(y_sb, y_hg, proj, proj, x2d, w_sb_out, w_hg_out, w_out, final_norm_g)


def _suffix_sum_weights():
    j = np.arange(2 * KEY_TILE)[:, None] % KEY_TILE
    s = np.arange(2 * KEY_TILE)[None, :]
    return jnp.asarray(np.where(s < KEY_TILE, j > s, True), BF16)


def _hg_sum_weights():
    r = np.arange(HG_CHUNK)[:, None]
    c = np.arange(HG_CHUNK)[None, :]
    groups = [c <= r]
    for m in HG_LEVELS:
        ref = (r // (2 * m)) * 2 * m + m - 1
        groups.append(np.where(r > ref, (c > ref) & (c <= r), (c > r) & (c <= ref)))
    w = np.concatenate(groups, axis=0)
    return jnp.asarray(np.concatenate([w, w], axis=1), BF16)


def kernel(x, meta, norm_g, w_in, w_sb_out, w_hg_out, w_out, hg_norm_g, hg_lb_logits, final_norm_g):
    batch, seq, d = x.shape
    assert d == D_MODEL and meta.shape == (N_META, D_MODEL)
    assert norm_g.shape[0] == 1 and w_in.shape == (1, D_MODEL, IN_COLS)
    assert seq % HG_ROWS == 0 and (batch * seq) % INPROJ_ROWS == 0

    x2d = x.reshape(batch * seq, D_MODEL)
    meta_blk = jnp.concatenate(
        [jnp.zeros((META_ROWS - N_META, D_MODEL), x.dtype), meta.astype(x.dtype)], axis=0)
    w_in2 = w_in.reshape(D_MODEL, IN_COLS)

    proj, f_proj = _inproj(x2d, norm_g, w_in2, INPROJ_ROWS)
    meta_cols = pl.cdiv((T_HG_GATE + 1) * COL_TILE, INPROJ_COLS) * INPROJ_COLS
    proj_meta, f_meta = _inproj(meta_blk, norm_g, w_in2, META_ROWS, n_cols=meta_cols)

    y_sb = _stick_breaking(proj, proj_meta, _suffix_sum_weights(), batch, seq)
    y_hg = _hgrn2(proj, f_proj, proj_meta, f_meta, hg_lb_logits, hg_norm_g,
                  _hg_sum_weights(), batch, seq, HG_ROWS)

    out = _final(y_sb.reshape(batch * seq, SB_WIDTH), y_hg.reshape(batch * seq, HG_WIDTH), proj,
                 x2d, w_sb_out.reshape(SB_WIDTH, D_MODEL), w_hg_out.reshape(HG_WIDTH, D_MODEL),
                 w_out.reshape(D_MODEL, D_MODEL), final_norm_g.reshape(1, D_MODEL), MERGE_ROWS)
    return out.reshape(batch, seq, D_MODEL)
```

```python
import functools

import jax
import jax.numpy as jnp
import numpy as np
from jax import lax
from jax.experimental import pallas as pl
from jax.experimental.pallas import tpu as pltpu

F32 = jnp.float32
BF16 = jnp.bfloat16

D_MODEL = 1024
N_META = 16
SB_HEADS = 8
SB_HEAD_DIM = 64
SB_WIDTH = SB_HEADS * SB_HEAD_DIM
SB_SCALE = SB_HEAD_DIM ** -0.5
HG_HEADS = 4
HG_DIM = 128
HG_WIDTH = HG_HEADS * HG_DIM
HG_SCALE = HG_DIM ** -0.5
IN_COLS = 4 * SB_WIDTH + 4 * HG_WIDTH + 2 * D_MODEL
EPS = 1e-6

LANES = 128
SUBLANES = 8
VMEM_LIMIT_BYTES = 56 * 1024 * 1024

INPROJ_ROWS = 2048
HG_ROWS = 2048
MERGE_ROWS = 1024

COL_TILE = 512
T_SB_Q, T_SB_K, T_SB_V, T_SB_GATE, T_HG_Q, T_HG_F, T_HG_I, T_HG_GATE = range(8)
N_COL_TILES = IN_COLS // COL_TILE

KEY_TILE = 128
META_ROWS = 128
HG_CHUNK = 64
HG_LEVELS = (1, 2, 4, 8, 16, 32)
HG_UNROLL = 32


_NT = (((1,), (1,)), ((), ()))
_TN = (((0,), (0,)), ((), ()))


def _sigmoid(x):
    return 1.0 / (1.0 + jnp.exp(-x))


def _split_bf16(x):
    hi = x.astype(BF16)
    lo = (x - hi.astype(F32)).astype(BF16)
    return hi, lo


INPROJ_COLS = 2 * COL_TILE
INPROJ_SLAB = 512
assert (T_SB_Q * COL_TILE) % INPROJ_COLS == 0


def _inproj_kernel(x_ref, g_ref, w_ref, o_ref, f_ref, u_ref):
    n = pl.program_id(1)
    f_step, f_col = divmod(T_HG_F * COL_TILE, INPROJ_COLS)
    assert f_step != 0

    def weights():
        lane = lax.broadcasted_iota(jnp.int32, (1, INPROJ_COLS), 1)
        is_q = jnp.logical_and(n == (T_SB_Q * COL_TILE) // INPROJ_COLS, lane < SB_WIDTH)
        return (w_ref[...] * jnp.where(is_q, SB_SCALE, 1.0).astype(F32)).astype(BF16)

    @pl.when(n == 0)
    def _():
        wb = weights()
        slab = min(INPROJ_SLAB, x_ref.shape[0])
        for s in range(x_ref.shape[0] // slab):
            rows = pl.ds(s * slab, slab)
            x = x_ref[rows, :]
            ms = jnp.mean(x * x, axis=-1, keepdims=True)
            u = (x * lax.rsqrt(ms + EPS) * g_ref[...]).astype(BF16)
            u_ref[rows, :] = u
            o_ref[rows, :] = jnp.dot(u, wb, preferred_element_type=F32).astype(BF16)

    @pl.when(n != 0)
    def _():
        p = jnp.dot(u_ref[...], weights(), preferred_element_type=F32)
        o_ref[...] = p.astype(BF16)

        @pl.when(n == f_step)
        def _():
            f_ref[...] = p[:, f_col:f_col + HG_WIDTH]


def _inproj(x2d, norm_g, w_in, row_tile, n_cols=IN_COLS):
    rows = x2d.shape[0]
    assert rows % row_tile == 0 and n_cols % INPROJ_COLS == 0 and n_cols >= (T_HG_F + 1) * COL_TILE
    return pl.pallas_call(
        _inproj_kernel,
        grid=(rows // row_tile, n_cols // INPROJ_COLS),
        in_specs=[
            pl.BlockSpec((row_tile, D_MODEL), lambda r, n: (r, 0)),
            pl.BlockSpec((1, D_MODEL), lambda r, n: (0, 0)),
            pl.BlockSpec((D_MODEL, INPROJ_COLS), lambda r, n: (0, n)),
        ],
        out_specs=[
            pl.BlockSpec((row_tile, INPROJ_COLS), lambda r, n: (r, n)),
            pl.BlockSpec((row_tile, HG_WIDTH), lambda r, n: (r, 0)),
        ],
        out_shape=[
            jax.ShapeDtypeStruct((rows, n_cols), BF16),
            jax.ShapeDtypeStruct((rows, HG_WIDTH), F32),
        ],
        scratch_shapes=[pltpu.VMEM((row_tile, D_MODEL), BF16)],
        compiler_params=pltpu.CompilerParams(
            dimension_semantics=("parallel", "arbitrary"),
            vmem_limit_bytes=VMEM_LIMIT_BYTES),
        name="inproj",
    )(x2d, norm_g, w_in)


SB_PAIRS = SB_WIDTH // LANES
SB_UNDERFLOW = -104.0
SB_QTILES = 8
SB_EARLY = 48


def _sb_kernel(q_ref, k_ref, v_ref, g_ref, mk_ref, mv_ref, uw_ref, o_ref, acc_ref, carry_ref):
    j = pl.program_id(1)
    lane = lax.broadcasted_iota(jnp.int32, (KEY_TILE, LANES), 1)
    first = lane < SB_HEAD_DIM
    uw = uw_ref[...]
    pair_lanes = [slice(p * LANES, (p + 1) * LANES) for p in range(SB_PAIRS)]
    q_rows = [slice(t * KEY_TILE, (t + 1) * KEY_TILE) for t in range(SB_QTILES)]
    q2 = []
    for qr in q_rows:
        for sl in pair_lanes:
            q = q_ref[qr, sl]
            zero = jnp.zeros_like(q)
            q2.append(jnp.concatenate([jnp.where(first, q, zero), jnp.where(first, zero, q)], axis=0))

    def early(x):
        return jnp.concatenate([x[:SB_EARLY], x[KEY_TILE:KEY_TILE + SB_EARLY]], axis=0)

    def with_early(x, e, combine):
        return jnp.concatenate([combine(x[:SB_EARLY], e[:SB_EARLY]), x[SB_EARLY:KEY_TILE],
                                combine(x[KEY_TILE:KEY_TILE + SB_EARLY], e[SB_EARLY:]),
                                x[KEY_TILE + SB_EARLY:]], axis=0)

    def log_terms(z, valid):
        log_beta = jnp.minimum(z, 0.0) - jnp.log(1.0 + jnp.exp(-jnp.abs(z)))
        log_1mb = log_beta - z
        if valid is not None:
            log_1mb = jnp.where(valid, log_1mb, 0.0)
        hi, lo = _split_bf16(log_1mb)
        return log_beta, jnp.concatenate([hi, lo], axis=1)

    def sweep(kt_ref, vt_ref, jobs, first_pass=False):
        chains = [(t * SB_PAIRS + p, sl) + job
                  for t, job in enumerate(jobs) for p, sl in enumerate(pair_lanes)]

        def scores(chain):
            c, sl, rows, _, erows = chain
            z = lax.dot_general(q2[c], kt_ref[rows, sl], _NT, preferred_element_type=F32)
            ze = None if erows is None else lax.dot_general(
                early(q2[c]), kt_ref[erows, sl], _NT, preferred_element_type=F32)
            return z, ze

        def suffix_sums(chain, z, ze):
            valids = chain[3]
            terms = [] if ze is None else [log_terms(ze, None)]
            terms += [log_terms(z[:, n * KEY_TILE:(n + 1) * KEY_TILE], valid)
                      for n, valid in enumerate(valids)]
            sums = jnp.dot(jnp.concatenate([hl for _, hl in terms], axis=0), uw,
                           preferred_element_type=F32)
            return [lb for lb, _ in terms], sums

        def weights(chain, log_betas, sums):
            c, _, _, valids, erows = chain
            off = 0 if erows is None else 2 * SB_EARLY
            lbs = log_betas if erows is None else log_betas[1:]
            carry = None if first_pass else carry_ref[c]
            w_n = [None] * len(valids)
            for n in reversed(range(len(valids))):
                cs = sums[off + n * 2 * KEY_TILE:off + (n + 1) * 2 * KEY_TILE]
                log_w = lbs[n] + cs[:, :KEY_TILE]
                w = jnp.exp(log_w if carry is None else log_w + carry)
                if valids[n] is not None:
                    w = jnp.where(valids[n], w, 0.0)
                w_n[n] = w.astype(BF16)
                carry = cs[:, KEY_TILE:] if carry is None else carry + cs[:, KEY_TILE:]
            we = None
            if erows is not None:
                cs = sums[:off]
                we = jnp.exp(log_betas[0] + cs[:, :KEY_TILE] + early(carry)).astype(BF16)
                carry = with_early(carry, cs[:, KEY_TILE:], jnp.add)
            carry_ref[c] = carry
            return jnp.concatenate(w_n, axis=1), we

        def accumulate(chain, w, we):
            c, sl, rows, _, erows = chain
            pv = jnp.dot(w, vt_ref[rows, sl], preferred_element_type=F32)
            if erows is not None:
                pv = with_early(pv, jnp.dot(we, vt_ref[erows, sl], preferred_element_type=F32),
                                jnp.add)
            if first_pass:
                acc_ref[c] = pv
            else:
                acc_ref[c] += pv

        stages = (scores, suffix_sums, weights, accumulate)
        held = [None] * len(chains)
        for k in range(len(chains) + len(stages) - 1):
            for s, stage in reversed(list(enumerate(stages))):
                i = k - s
                if 0 <= i < len(chains):
                    held[i] = stage(chains[i]) if s == 0 else stage(chains[i], *held[i])

    row2 = lax.broadcasted_iota(jnp.int32, (2 * KEY_TILE, KEY_TILE), 0) % KEY_TILE
    col2 = lax.broadcasted_iota(jnp.int32, (2 * KEY_TILE, KEY_TILE), 1)
    causal = col2 < row2

    def key_rows(tile, n_tiles):
        return pl.ds(pl.multiple_of(tile * KEY_TILE, KEY_TILE), n_tiles * KEY_TILE)

    i_first = j * SB_QTILES
    fused = j >= 1
    assert SB_QTILES >= 3

    def standard(i):
        return key_rows(i - 1, 2), [None, causal], key_rows(i - 2, 1)

    @pl.when(fused)
    def _():
        sweep(k_ref, v_ref, [standard(i_first + t) for t in range(SB_QTILES)], first_pass=True)

    @pl.when(jnp.logical_not(fused))
    def _():
        sweep(k_ref, v_ref, [(key_rows(0, 1), [causal], None), (key_rows(0, 2), [None, causal], None)]
              + [standard(t) for t in range(2, SB_QTILES)], first_pass=True)

    left = [jnp.maximum(i_first + t - 2, 0) for t in range(SB_QTILES)]
    is_standard = [fused if t < 2 else True for t in range(SB_QTILES)]
    late = row2 >= SB_EARLY

    def has_keys(t, it):
        return jnp.where(it < 0, is_standard[t], it < left[t])

    def live_flags(it):
        floor = jnp.full((SUBLANES, LANES), -jnp.inf, F32)
        pending, everyone = floor, floor
        for t in range(SB_QTILES):
            top = jnp.max(carry_ref[t * SB_PAIRS:(t + 1) * SB_PAIRS].reshape(-1, SUBLANES, LANES),
                          axis=0)
            everyone = jnp.maximum(everyone, top)
            pending = jnp.maximum(pending, jnp.where(has_keys(t, it), top, floor))
        return ((jnp.max(pending) > SB_UNDERFLOW).astype(jnp.int32),
                (jnp.max(everyone) > SB_UNDERFLOW).astype(jnp.int32))

    def cond(c):
        return c[1] > 0

    def body(c):
        it = c[0]
        catch_up = it < 0
        jobs = []
        for t in range(SB_QTILES):
            tile = jnp.where(catch_up, left[t], jnp.maximum(left[t] - 1 - it, 0))
            valid = jnp.logical_and(has_keys(t, it),
                                    jnp.logical_or(late, jnp.logical_not(catch_up)))
            jobs.append((key_rows(tile, 1), [valid], None))
        sweep(k_ref, v_ref, jobs)
        return (it + 1,) + live_flags(it + 1)

    _, _, live = lax.while_loop(cond, body, (jnp.int32(-1),) + live_flags(jnp.int32(-1)))

    @pl.when(live > 0)
    def _():
        sweep(mk_ref, mv_ref, [(slice(None), [col2 >= META_ROWS - N_META], None)] * SB_QTILES)

    for t, qr in enumerate(q_rows):
        for p, sl in enumerate(pair_lanes):
            acc = acc_ref[t * SB_PAIRS + p]
            out = jnp.where(first, acc[:KEY_TILE], acc[KEY_TILE:])
            gate = g_ref[qr, sl].astype(F32)
            o_ref[qr, sl] = (out * (gate * _sigmoid(gate))).astype(BF16)


def _stick_breaking(proj, proj_meta, uw, batch, seq):
    assert SB_WIDTH == COL_TILE
    proj3 = proj.reshape(batch, seq, IN_COLS)
    q_blk = SB_QTILES * KEY_TILE
    assert seq % q_blk == 0
    return pl.pallas_call(
        _sb_kernel,
        grid=(batch, seq // q_blk),
        in_specs=[
            pl.BlockSpec((None, q_blk, SB_WIDTH), lambda b, i: (b, i, T_SB_Q)),
            pl.BlockSpec((None, seq, SB_WIDTH), lambda b, i: (b, 0, T_SB_K)),
            pl.BlockSpec((None, seq, SB_WIDTH), lambda b, i: (b, 0, T_SB_V)),
            pl.BlockSpec((None, q_blk, SB_WIDTH), lambda b, i: (b, i, T_SB_GATE)),
            pl.BlockSpec((META_ROWS, SB_WIDTH), lambda b, i: (0, T_SB_K)),
            pl.BlockSpec((META_ROWS, SB_WIDTH), lambda b, i: (0, T_SB_V)),
            pl.BlockSpec((2 * KEY_TILE, 2 * KEY_TILE), lambda b, i: (0, 0)),
        ],
        out_specs=pl.BlockSpec((None, q_blk, SB_WIDTH), lambda b, i: (b, i, 0)),
        out_shape=jax.ShapeDtypeStruct((batch, seq, SB_WIDTH), BF16),
        scratch_shapes=[pltpu.VMEM((SB_QTILES * SB_PAIRS, 2 * KEY_TILE, LANES), F32),
                        pltpu.VMEM((SB_QTILES * SB_PAIRS, 2 * KEY_TILE, LANES), F32)],
        compiler_params=pltpu.CompilerParams(
            dimension_semantics=("parallel", "arbitrary"),
            vmem_limit_bytes=VMEM_LIMIT_BYTES),
        name="stick_breaking",
    )(proj3, proj3, proj3, proj3, proj_meta, proj_meta, uw)


def _hg_decay(f_logit, lbv, sum_w, pad_rows):
    f = lbv + (1.0 - lbv) * _sigmoid(f_logit)
    if pad_rows:
        row = lax.broadcasted_iota(jnp.int32, f.shape, 0)
        f = jnp.where(row >= pad_rows, f, 1.0)
    g_hi, g_lo = _split_bf16(jnp.log(f))
    sums = jnp.dot(sum_w, jnp.concatenate([g_hi, g_lo], axis=0), preferred_element_type=F32)
    return sums, 1.0 - f


def _hg_next_state(state_t, b, kk, v):
    b_last = b[HG_CHUNK - 1:HG_CHUNK, :]
    k_dec = (kk * jnp.exp(b_last - b)).astype(BF16)
    return state_t * jnp.exp(b_last) + lax.dot_general(v, k_dec, _TN, preferred_element_type=F32)


def _hg_scores(qf, kk, sums, masks):
    a = jnp.where(masks[0], lax.dot_general(qf.astype(BF16), kk.astype(BF16), _NT,
                                            preferred_element_type=F32), 0.0)
    for lvl in range(len(HG_LEVELS)):
        e = jnp.exp(sums[(1 + lvl) * HG_CHUNK:(2 + lvl) * HG_CHUNK])
        part = lax.dot_general((qf * e).astype(BF16), (kk * e).astype(BF16), _NT,
                               preferred_element_type=F32)
        a = jnp.where(masks[1 + lvl], part, a)
    return a


def _hg_score_masks():
    row = lax.broadcasted_iota(jnp.int32, (HG_CHUNK, HG_CHUNK), 0)
    col = lax.broadcasted_iota(jnp.int32, (HG_CHUNK, HG_CHUNK), 1)
    return [row == col] + [
        ((row // (2 * m)) == (col // (2 * m))) & ((row // m) % 2 == 1) & ((col // m) % 2 == 0)
        for m in HG_LEVELS]


def _hg_kernel(q_ref, f_ref, v_ref, gate_ref, mf_ref, mv_ref, lbl_ref, gn_ref,
               sumw_ref, o_ref, state_ref, *, chunks_per_tile):
    t = pl.program_id(1)
    sum_w = sumw_ref[...]
    lbl = lbl_ref[...]
    ex = jnp.exp(lbl - jnp.max(lbl, axis=0, keepdims=True))
    lb = ex[0:1, :] / jnp.sum(ex, axis=0, keepdims=True)
    gn = gn_ref[...]
    head_lanes = [slice(h * HG_DIM, (h + 1) * HG_DIM) for h in range(HG_HEADS)]

    @pl.when(t == 0)
    def _():
        for h, sl in enumerate(head_lanes):
            b, kk = _hg_decay(mf_ref[:, sl], lb[:, sl], sum_w[:HG_CHUNK], HG_CHUNK - N_META)
            state_ref[h] = _hg_next_state(jnp.zeros((HG_DIM, HG_DIM), F32), b, kk, mv_ref[:, sl])

    masks = _hg_score_masks()

    def body(c, carry):
        base = c * (HG_UNROLL * HG_CHUNK)
        rows = [pl.ds(pl.multiple_of(base + u * HG_CHUNK, HG_CHUNK), HG_CHUNK)
                for u in range(HG_UNROLL)]
        states = [state_ref[h] for h in range(HG_HEADS)]

        def decay(r):
            return [_hg_decay(f_ref[r, sl], lb[:, sl], sum_w, 0) for sl in head_lanes]

        def mix(r, dec):
            out = []
            for h, sl in enumerate(head_lanes):
                sums, kk = dec[h]
                b = sums[:HG_CHUNK]
                q = q_ref[r, sl].astype(F32)
                qf = q * _sigmoid(q) * HG_SCALE
                scores = _hg_scores(qf, kk, sums, masks)
                inter = lax.dot_general((qf * jnp.exp(b)).astype(BF16), states[h].astype(BF16),
                                        _NT, preferred_element_type=F32)
                states[h] = _hg_next_state(states[h], b, kk, v_ref[r, sl])
                out.append((scores, inter))
            return out

        def emit(r, mixed):
            for h, sl in enumerate(head_lanes):
                scores, inter = mixed[h]
                o = inter + jnp.dot(scores.astype(BF16), v_ref[r, sl], preferred_element_type=F32)
                o = o * lax.rsqrt(jnp.mean(o * o, axis=-1, keepdims=True) + EPS) * gn[:, sl]
                gate = gate_ref[r, sl].astype(F32)
                o_ref[r, sl] = (o * (gate * _sigmoid(gate))).astype(BF16)

        stages = (decay, mix, emit)
        held = [None] * len(rows)
        for k in range(len(rows) + len(stages) - 1):
            for s in reversed(range(len(stages))):
                i = k - s
                if 0 <= i < len(rows):
                    held[i] = stages[s](rows[i]) if s == 0 else stages[s](rows[i], held[i])
        for h in range(HG_HEADS):
            state_ref[h] = states[h]
        return carry

    lax.fori_loop(0, chunks_per_tile // HG_UNROLL, body, 0)


def _hgrn2(proj, f_proj, proj_meta, f_meta, lb_logits, hg_norm_g, sum_w, batch, seq, row_tile):
    assert seq % row_tile == 0 and row_tile % HG_CHUNK == 0
    proj3 = proj.reshape(batch, seq, IN_COLS)
    f3 = f_proj.reshape(batch, seq, HG_WIDTH)
    meta_blk = META_ROWS // HG_CHUNK - 1
    const = lambda b, t: (0, 0)
    return pl.pallas_call(
        functools.partial(_hg_kernel, chunks_per_tile=row_tile // HG_CHUNK),
        grid=(batch, seq // row_tile),
        in_specs=[
            pl.BlockSpec((None, row_tile, HG_WIDTH), lambda b, t: (b, t, T_HG_Q)),
            pl.BlockSpec((None, row_tile, HG_WIDTH), lambda b, t: (b, t, 0)),
            pl.BlockSpec((None, row_tile, HG_WIDTH), lambda b, t: (b, t, T_HG_I)),
            pl.BlockSpec((None, row_tile, HG_WIDTH), lambda b, t: (b, t, T_HG_GATE)),
            pl.BlockSpec((HG_CHUNK, HG_WIDTH), lambda b, t: (meta_blk, 0)),
            pl.BlockSpec((HG_CHUNK, HG_WIDTH), lambda b, t: (meta_blk, T_HG_I)),
            pl.BlockSpec(lb_logits.shape, const),
            pl.BlockSpec((1, HG_WIDTH), const),
            pl.BlockSpec(sum_w.shape, const),
        ],
        out_specs=pl.BlockSpec((None, row_tile, HG_WIDTH), lambda b, t: (b, t, 0)),
        out_shape=jax.ShapeDtypeStruct((batch, seq, HG_WIDTH), BF16),
        scratch_shapes=[pltpu.VMEM((HG_HEADS, HG_DIM, HG_DIM), F32)],
        compiler_params=pltpu.CompilerParams(
            dimension_semantics=("parallel", "arbitrary"),
            vmem_limit_bytes=VMEM_LIMIT_BYTES),
        name="hgrn2",
    )(proj3, f3, proj3, proj3, f_meta, proj_meta, lb_logits, hg_norm_g, sum_w)


MERGE_SLAB = 256


def _final_kernel(ysb_ref, yhg_ref, gsb_ref, ghg_ref, x_ref, wsb_ref, whg_ref, wout_ref, fg_ref,
                  o_ref, wsb_s, whg_s, wout_s):
    @pl.when(pl.program_id(0) == 0)
    def _():
        wsb_s[...] = wsb_ref[...].astype(BF16)
        whg_s[...] = whg_ref[...].astype(BF16)
        wout_s[...] = wout_ref[...].astype(BF16)

    n_slab = o_ref.shape[0] // MERGE_SLAB
    slabs = [pl.ds(s * MERGE_SLAB, MERGE_SLAB) for s in range(n_slab)]
    a = [jnp.dot(ysb_ref[sl, :], wsb_s[...], preferred_element_type=F32) for sl in slabs]
    b = [jnp.dot(yhg_ref[sl, :], whg_s[...], preferred_element_type=F32) for sl in slabs]
    merged = [(_sigmoid(gsb_ref[sl, :].astype(F32)) * a[s]
               + _sigmoid(ghg_ref[sl, :].astype(F32)) * b[s]).astype(BF16)
              for s, sl in enumerate(slabs)]
    h = [x_ref[sl, :] + jnp.dot(merged[s], wout_s[...], preferred_element_type=F32)
         for s, sl in enumerate(slabs)]
    for s, sl in enumerate(slabs):
        ms = jnp.mean(h[s] * h[s], axis=-1, keepdims=True)
        o_ref[sl, :] = h[s] * lax.rsqrt(ms + EPS) * fg_ref[...]


def _final(y_sb, y_hg, proj, x2d, w_sb_out, w_hg_out, w_out, final_norm_g, row_tile):
    rows = x2d.shape[0]
    assert rows % row_tile == 0
    gate_blk = (4 * SB_WIDTH + 4 * HG_WIDTH) // D_MODEL
    const = lambda r: (0, 0)
    return pl.pallas_call(
        _final_kernel,
        grid=(rows // row_tile,),
        in_specs=[
            pl.BlockSpec((row_tile, SB_WIDTH), lambda r: (r, 0)),
            pl.BlockSpec((row_tile, HG_WIDTH), lambda r: (r, 0)),
            pl.BlockSpec((row_tile, D_MODEL), lambda r: (r, gate_blk)),
            pl.BlockSpec((row_tile, D_MODEL), lambda r: (r, gate_blk + 1)),
            pl.BlockSpec((row_tile, D_MODEL), lambda r: (r, 0)),
            pl.BlockSpec((SB_WIDTH, D_MODEL), const),
            pl.BlockSpec((HG_WIDTH, D_MODEL), const),
            pl.BlockSpec((D_MODEL, D_MODEL), const),
            pl.BlockSpec((1, D_MODEL), const),
        ],
        out_specs=pl.BlockSpec((row_tile, D_MODEL), lambda r: (r, 0)),
        out_shape=jax.ShapeDtypeStruct((rows, D_MODEL), F32),
        scratch_shapes=[pltpu.VMEM((SB_WIDTH, D_MODEL), BF16),
                        pltpu.VMEM((HG_WIDTH, D_MODEL), BF16),
                        pltpu.VMEM((D_MODEL, D_MODEL), BF16)],
        compiler_params=pltpu.CompilerParams(
            dimension_semantics=("arbitrary",),
            vmem_limit_bytes=VMEM_LIMIT_BYTES),
        name="merge_out",
    )(y_sb, y_hg, proj, proj, x2d, w_sb_out, w_hg_out, w_out, final_norm_g)


def _suffix_sum_weights():
    j = np.arange(2 * KEY_TILE)[:, None] % KEY_TILE
    s = np.arange(2 * KEY_TILE)[None, :]
    return jnp.asarray(np.where(s < KEY_TILE, j > s, True), BF16)


def _hg_sum_weights():
    r = np.arange(HG_CHUNK)[:, None]
    c = np.arange(HG_CHUNK)[None, :]
    groups = [c <= r]
    for m in HG_LEVELS:
        ref = (r // (2 * m)) * 2 * m + m - 1
        groups.append(np.where(r > ref, (c > ref) & (c <= r), (c > r) & (c <= ref)))
    w = np.concatenate(groups, axis=0)
    return jnp.asarray(np.concatenate([w, w], axis=1), BF16)


def kernel(x, meta, norm_g, w_in, w_sb_out, w_hg_out, w_out, hg_norm_g, hg_lb_logits, final_norm_g):
    batch, seq, d = x.shape
    assert d == D_MODEL and meta.shape == (N_META, D_MODEL)
    assert norm_g.shape[0] == 1 and w_in.shape == (1, D_MODEL, IN_COLS)
    assert seq % HG_ROWS == 0 and (batch * seq) % INPROJ_ROWS == 0

    x2d = x.reshape(batch * seq, D_MODEL)
    meta_blk = jnp.concatenate(
        [jnp.zeros((META_ROWS - N_META, D_MODEL), x.dtype), meta.astype(x.dtype)], axis=0)
    w_in2 = w_in.reshape(D_MODEL, IN_COLS)

    proj, f_proj = _inproj(x2d, norm_g, w_in2, INPROJ_ROWS)
    meta_cols = pl.cdiv((T_HG_GATE + 1) * COL_TILE, INPROJ_COLS) * INPROJ_COLS
    proj_meta, f_meta = _inproj(meta_blk, norm_g, w_in2, META_ROWS, n_cols=meta_cols)

    y_sb = _stick_breaking(proj, proj_meta, _suffix_sum_weights(), batch, seq)
    y_hg = _hgrn2(proj, f_proj, proj_meta, f_meta, hg_lb_logits, hg_norm_g,
                  _hg_sum_weights(), batch, seq, HG_ROWS)

    out = _final(y_sb.reshape(batch * seq, SB_WIDTH), y_hg.reshape(batch * seq, HG_WIDTH), proj,
                 x2d, w_sb_out.reshape(SB_WIDTH, D_MODEL), w_hg_out.reshape(HG_WIDTH, D_MODEL),
                 w_out.reshape(D_MODEL, D_MODEL), final_norm_g.reshape(1, D_MODEL), MERGE_ROWS)
    return out.reshape(batch, seq, D_MODEL)
```

```python
import functools

import jax
import jax.numpy as jnp
import numpy as np
from jax import lax
from jax.experimental import pallas as pl
from jax.experimental.pallas import tpu as pltpu

F32 = jnp.float32
BF16 = jnp.bfloat16

D_MODEL = 1024
N_META = 16
SB_HEADS = 8
SB_HEAD_DIM = 64
SB_WIDTH = SB_HEADS * SB_HEAD_DIM
SB_SCALE = SB_HEAD_DIM ** -0.5
HG_HEADS = 4
HG_DIM = 128
HG_WIDTH = HG_HEADS * HG_DIM
HG_SCALE = HG_DIM ** -0.5
IN_COLS = 4 * SB_WIDTH + 4 * HG_WIDTH + 2 * D_MODEL
EPS = 1e-6

LANES = 128
SUBLANES = 8
VMEM_LIMIT_BYTES = 56 * 1024 * 1024

INPROJ_ROWS = 2048
HG_ROWS = 2048
MERGE_ROWS = 1024

COL_TILE = 512
T_SB_Q, T_SB_K, T_SB_V, T_SB_GATE, T_HG_Q, T_HG_F, T_HG_I, T_HG_GATE = range(8)
N_COL_TILES = IN_COLS // COL_TILE

KEY_TILE = 128
META_ROWS = 128
HG_CHUNK = 64
HG_LEVELS = (1, 2, 4, 8, 16, 32)
HG_UNROLL = 32


_NT = (((1,), (1,)), ((), ()))
_TN = (((0,), (0,)), ((), ()))


def _sigmoid(x):
    return 1.0 / (1.0 + jnp.exp(-x))


def _split_bf16(x):
    hi = x.astype(BF16)
    lo = (x - hi.astype(F32)).astype(BF16)
    return hi, lo


INPROJ_COLS = 2 * COL_TILE
INPROJ_SLAB = 512
assert (T_SB_Q * COL_TILE) % INPROJ_COLS == 0


def _inproj_kernel(x_ref, m_ref, g_ref, w_ref, o_ref, f_ref, om_ref, fm_ref, u_ref, um_ref):
    r = pl.program_id(0)
    n = pl.program_id(1)
    f_step, f_col = divmod(T_HG_F * COL_TILE, INPROJ_COLS)
    assert f_step != 0

    def weights():
        lane = lax.broadcasted_iota(jnp.int32, (1, INPROJ_COLS), 1)
        is_q = jnp.logical_and(n == (T_SB_Q * COL_TILE) // INPROJ_COLS, lane < SB_WIDTH)
        return (w_ref[...] * jnp.where(is_q, SB_SCALE, 1.0).astype(F32)).astype(BF16)

    def normalise(x):
        ms = jnp.mean(x * x, axis=-1, keepdims=True)
        return (x * lax.rsqrt(ms + EPS) * g_ref[...]).astype(BF16)

    @pl.when(n == 0)
    def _():
        wb = weights()
        slab = min(INPROJ_SLAB, x_ref.shape[0])
        for s in range(x_ref.shape[0] // slab):
            rows = pl.ds(s * slab, slab)
            u = normalise(x_ref[rows, :])
            u_ref[rows, :] = u
            o_ref[rows, :] = jnp.dot(u, wb, preferred_element_type=F32).astype(BF16)

    @pl.when(n != 0)
    def _():
        p = jnp.dot(u_ref[...], weights(), preferred_element_type=F32)
        o_ref[...] = p.astype(BF16)

        @pl.when(n == f_step)
        def _():
            f_ref[...] = p[:, f_col:f_col + HG_WIDTH]

    @pl.when(r == 0)
    def _():
        @pl.when(n == 0)
        def _():
            um_ref[...] = normalise(m_ref[...])

        pm = jnp.dot(um_ref[...], weights(), preferred_element_type=F32)
        om_ref[...] = pm.astype(BF16)

        @pl.when(n == f_step)
        def _():
            fm_ref[...] = pm[:, f_col:f_col + HG_WIDTH]


def _inproj(x2d, meta_blk, norm_g, w_in, row_tile):
    rows = x2d.shape[0]
    n_steps = IN_COLS // INPROJ_COLS
    assert rows % row_tile == 0 and meta_blk.shape == (META_ROWS, D_MODEL)
    return pl.pallas_call(
        _inproj_kernel,
        grid=(rows // row_tile, n_steps),
        in_specs=[
            pl.BlockSpec((row_tile, D_MODEL), lambda r, n: (r, 0)),
            pl.BlockSpec((META_ROWS, D_MODEL), lambda r, n: (0, 0)),
            pl.BlockSpec((1, D_MODEL), lambda r, n: (0, 0)),
            pl.BlockSpec((D_MODEL, INPROJ_COLS), lambda r, n: (0, n)),
        ],
        out_specs=[
            pl.BlockSpec((row_tile, INPROJ_COLS), lambda r, n: (r, n)),
            pl.BlockSpec((row_tile, HG_WIDTH), lambda r, n: (r, 0)),
            pl.BlockSpec((META_ROWS, INPROJ_COLS), lambda r, n: (0, jnp.where(r == 0, n, n_steps - 1))),
            pl.BlockSpec((META_ROWS, HG_WIDTH), lambda r, n: (0, 0)),
        ],
        out_shape=[
            jax.ShapeDtypeStruct((rows, IN_COLS), BF16),
            jax.ShapeDtypeStruct((rows, HG_WIDTH), F32),
            jax.ShapeDtypeStruct((META_ROWS, IN_COLS), BF16),
            jax.ShapeDtypeStruct((META_ROWS, HG_WIDTH), F32),
        ],
        scratch_shapes=[pltpu.VMEM((row_tile, D_MODEL), BF16),
                        pltpu.VMEM((META_ROWS, D_MODEL), BF16)],
        compiler_params=pltpu.CompilerParams(
            dimension_semantics=("arbitrary", "arbitrary"),
            vmem_limit_bytes=VMEM_LIMIT_BYTES),
        name="inproj",
    )(x2d, meta_blk, norm_g, w_in)


SB_PAIRS = SB_WIDTH // LANES
SB_UNDERFLOW = -104.0
SB_QTILES = 8
SB_EARLY = 48


def _sb_kernel(q_ref, k_ref, v_ref, g_ref, mk_ref, mv_ref, uw_ref, o_ref, acc_ref, carry_ref):
    j = pl.program_id(1)
    lane = lax.broadcasted_iota(jnp.int32, (KEY_TILE, LANES), 1)
    first = lane < SB_HEAD_DIM
    uw = uw_ref[...]
    pair_lanes = [slice(p * LANES, (p + 1) * LANES) for p in range(SB_PAIRS)]
    q_rows = [slice(t * KEY_TILE, (t + 1) * KEY_TILE) for t in range(SB_QTILES)]
    q2 = []
    for qr in q_rows:
        for sl in pair_lanes:
            q = q_ref[qr, sl]
            zero = jnp.zeros_like(q)
            q2.append(jnp.concatenate([jnp.where(first, q, zero), jnp.where(first, zero, q)], axis=0))

    def early(x):
        return jnp.concatenate([x[:SB_EARLY], x[KEY_TILE:KEY_TILE + SB_EARLY]], axis=0)

    def with_early(x, e, combine):
        return jnp.concatenate([combine(x[:SB_EARLY], e[:SB_EARLY]), x[SB_EARLY:KEY_TILE],
                                combine(x[KEY_TILE:KEY_TILE + SB_EARLY], e[SB_EARLY:]),
                                x[KEY_TILE + SB_EARLY:]], axis=0)

    def log_terms(z, valid):
        log_beta = jnp.minimum(z, 0.0) - jnp.log(1.0 + jnp.exp(-jnp.abs(z)))
        log_1mb = log_beta - z
        if valid is not None:
            log_1mb = jnp.where(valid, log_1mb, 0.0)
        hi, lo = _split_bf16(log_1mb)
        return log_beta, jnp.concatenate([hi, lo], axis=1)

    def sweep(kt_ref, vt_ref, jobs, first_pass=False):
        chains = [(t * SB_PAIRS + p, sl) + job
                  for t, job in enumerate(jobs) for p, sl in enumerate(pair_lanes)]

        def scores(chain):
            c, sl, rows, _, erows = chain
            z = lax.dot_general(q2[c], kt_ref[rows, sl], _NT, preferred_element_type=F32)
            ze = None if erows is None else lax.dot_general(
                early(q2[c]), kt_ref[erows, sl], _NT, preferred_element_type=F32)
            return z, ze

        def suffix_sums(chain, z, ze):
            valids = chain[3]
            terms = [] if ze is None else [log_terms(ze, None)]
            terms += [log_terms(z[:, n * KEY_TILE:(n + 1) * KEY_TILE], valid)
                      for n, valid in enumerate(valids)]
            sums = jnp.dot(jnp.concatenate([hl for _, hl in terms], axis=0), uw,
                           preferred_element_type=F32)
            return [lb for lb, _ in terms], sums

        def weights(chain, log_betas, sums):
            c, _, _, valids, erows = chain
            off = 0 if erows is None else 2 * SB_EARLY
            lbs = log_betas if erows is None else log_betas[1:]
            carry = None if first_pass else carry_ref[c]
            w_n = [None] * len(valids)
            for n in reversed(range(len(valids))):
                cs = sums[off + n * 2 * KEY_TILE:off + (n + 1) * 2 * KEY_TILE]
                log_w = lbs[n] + cs[:, :KEY_TILE]
                w = jnp.exp(log_w if carry is None else log_w + carry)
                if valids[n] is not None:
                    w = jnp.where(valids[n], w, 0.0)
                w_n[n] = w.astype(BF16)
                carry = cs[:, KEY_TILE:] if carry is None else carry + cs[:, KEY_TILE:]
            we = None
            if erows is not None:
                cs = sums[:off]
                we = jnp.exp(log_betas[0] + cs[:, :KEY_TILE] + early(carry)).astype(BF16)
                carry = with_early(carry, cs[:, KEY_TILE:], jnp.add)
            carry_ref[c] = carry
            return jnp.concatenate(w_n, axis=1), we

        def accumulate(chain, w, we):
            c, sl, rows, _, erows = chain
            pv = jnp.dot(w, vt_ref[rows, sl], preferred_element_type=F32)
            if erows is not None:
                pv = with_early(pv, jnp.dot(we, vt_ref[erows, sl], preferred_element_type=F32),
                                jnp.add)
            if first_pass:
                acc_ref[c] = pv
            else:
                acc_ref[c] += pv

        stages = (scores, suffix_sums, weights, accumulate)
        held = [None] * len(chains)
        for k in range(len(chains) + len(stages) - 1):
            for s, stage in reversed(list(enumerate(stages))):
                i = k - s
                if 0 <= i < len(chains):
                    held[i] = stage(chains[i]) if s == 0 else stage(chains[i], *held[i])

    row2 = lax.broadcasted_iota(jnp.int32, (2 * KEY_TILE, KEY_TILE), 0) % KEY_TILE
    col2 = lax.broadcasted_iota(jnp.int32, (2 * KEY_TILE, KEY_TILE), 1)
    causal = col2 < row2

    def key_rows(tile, n_tiles):
        return pl.ds(pl.multiple_of(tile * KEY_TILE, KEY_TILE), n_tiles * KEY_TILE)

    i_first = j * SB_QTILES
    fused = j >= 1
    assert SB_QTILES >= 3

    def standard(i):
        return key_rows(i - 1, 2), [None, causal], key_rows(i - 2, 1)

    @pl.when(fused)
    def _():
        sweep(k_ref, v_ref, [standard(i_first + t) for t in range(SB_QTILES)], first_pass=True)

    @pl.when(jnp.logical_not(fused))
    def _():
        sweep(k_ref, v_ref, [(key_rows(0, 1), [causal], None), (key_rows(0, 2), [None, causal], None)]
              + [standard(t) for t in range(2, SB_QTILES)], first_pass=True)

    left = [jnp.maximum(i_first + t - 2, 0) for t in range(SB_QTILES)]
    is_standard = [fused if t < 2 else True for t in range(SB_QTILES)]
    late = row2 >= SB_EARLY

    def has_keys(t, it):
        return jnp.where(it < 0, is_standard[t], it < left[t])

    def live_flags(it):
        floor = jnp.full((SUBLANES, LANES), -jnp.inf, F32)
        pending, everyone = floor, floor
        for t in range(SB_QTILES):
            top = jnp.max(carry_ref[t * SB_PAIRS:(t + 1) * SB_PAIRS].reshape(-1, SUBLANES, LANES),
                          axis=0)
            everyone = jnp.maximum(everyone, top)
            pending = jnp.maximum(pending, jnp.where(has_keys(t, it), top, floor))
        return ((jnp.max(pending) > SB_UNDERFLOW).astype(jnp.int32),
                (jnp.max(everyone) > SB_UNDERFLOW).astype(jnp.int32))

    def cond(c):
        return c[1] > 0

    def body(c):
        it = c[0]
        catch_up = it < 0
        jobs = []
        for t in range(SB_QTILES):
            tile = jnp.where(catch_up, left[t], jnp.maximum(left[t] - 1 - it, 0))
            valid = jnp.logical_and(has_keys(t, it),
                                    jnp.logical_or(late, jnp.logical_not(catch_up)))
            jobs.append((key_rows(tile, 1), [valid], None))
        sweep(k_ref, v_ref, jobs)
        return (it + 1,) + live_flags(it + 1)

    _, _, live = lax.while_loop(cond, body, (jnp.int32(-1),) + live_flags(jnp.int32(-1)))

    @pl.when(live > 0)
    def _():
        sweep(mk_ref, mv_ref, [(slice(None), [col2 >= META_ROWS - N_META], None)] * SB_QTILES)

    for t, qr in enumerate(q_rows):
        for p, sl in enumerate(pair_lanes):
            acc = acc_ref[t * SB_PAIRS + p]
            out = jnp.where(first, acc[:KEY_TILE], acc[KEY_TILE:])
            gate = g_ref[qr, sl].astype(F32)
            o_ref[qr, sl] = (out * (gate * _sigmoid(gate))).astype(BF16)


def _stick_breaking(proj, proj_meta, uw, batch, seq):
    assert SB_WIDTH == COL_TILE
    proj3 = proj.reshape(batch, seq, IN_COLS)
    q_blk = SB_QTILES * KEY_TILE
    assert seq % q_blk == 0
    return pl.pallas_call(
        _sb_kernel,
        grid=(batch, seq // q_blk),
        in_specs=[
            pl.BlockSpec((None, q_blk, SB_WIDTH), lambda b, i: (b, i, T_SB_Q)),
            pl.BlockSpec((None, seq, SB_WIDTH), lambda b, i: (b, 0, T_SB_K)),
            pl.BlockSpec((None, seq, SB_WIDTH), lambda b, i: (b, 0, T_SB_V)),
            pl.BlockSpec((None, q_blk, SB_WIDTH), lambda b, i: (b, i, T_SB_GATE)),
            pl.BlockSpec((META_ROWS, SB_WIDTH), lambda b, i: (0, T_SB_K)),
            pl.BlockSpec((META_ROWS, SB_WIDTH), lambda b, i: (0, T_SB_V)),
            pl.BlockSpec((2 * KEY_TILE, 2 * KEY_TILE), lambda b, i: (0, 0)),
        ],
        out_specs=pl.BlockSpec((None, q_blk, SB_WIDTH), lambda b, i: (b, i, 0)),
        out_shape=jax.ShapeDtypeStruct((batch, seq, SB_WIDTH), BF16),
        scratch_shapes=[pltpu.VMEM((SB_QTILES * SB_PAIRS, 2 * KEY_TILE, LANES), F32),
                        pltpu.VMEM((SB_QTILES * SB_PAIRS, 2 * KEY_TILE, LANES), F32)],
        compiler_params=pltpu.CompilerParams(
            dimension_semantics=("parallel", "arbitrary"),
            vmem_limit_bytes=VMEM_LIMIT_BYTES),
        name="stick_breaking",
    )(proj3, proj3, proj3, proj3, proj_meta, proj_meta, uw)


def _hg_decay(f_logit, lbv, sum_w, pad_rows):
    f = lbv + (1.0 - lbv) * _sigmoid(f_logit)
    if pad_rows:
        row = lax.broadcasted_iota(jnp.int32, f.shape, 0)
        f = jnp.where(row >= pad_rows, f, 1.0)
    g_hi, g_lo = _split_bf16(jnp.log(f))
    sums = jnp.dot(sum_w, jnp.concatenate([g_hi, g_lo], axis=0), preferred_element_type=F32)
    return sums, 1.0 - f


def _hg_next_state(state_t, b, kk, v):
    b_last = b[HG_CHUNK - 1:HG_CHUNK, :]
    k_dec = (kk * jnp.exp(b_last - b)).astype(BF16)
    return state_t * jnp.exp(b_last) + lax.dot_general(v, k_dec, _TN, preferred_element_type=F32)


def _hg_scores(qf, kk, sums, masks):
    a = jnp.where(masks[0], lax.dot_general(qf.astype(BF16), kk.astype(BF16), _NT,
                                            preferred_element_type=F32), 0.0)
    for lvl in range(len(HG_LEVELS)):
        e = jnp.exp(sums[(1 + lvl) * HG_CHUNK:(2 + lvl) * HG_CHUNK])
        part = lax.dot_general((qf * e).astype(BF16), (kk * e).astype(BF16), _NT,
                               preferred_element_type=F32)
        a = jnp.where(masks[1 + lvl], part, a)
    return a


def _hg_score_masks():
    row = lax.broadcasted_iota(jnp.int32, (HG_CHUNK, HG_CHUNK), 0)
    col = lax.broadcasted_iota(jnp.int32, (HG_CHUNK, HG_CHUNK), 1)
    return [row == col] + [
        ((row // (2 * m)) == (col // (2 * m))) & ((row // m) % 2 == 1) & ((col // m) % 2 == 0)
        for m in HG_LEVELS]


def _hg_kernel(q_ref, f_ref, v_ref, gate_ref, mf_ref, mv_ref, lbl_ref, gn_ref,
               sumw_ref, o_ref, state_ref, *, chunks_per_tile):
    t = pl.program_id(1)
    sum_w = sumw_ref[...]
    lbl = lbl_ref[...]
    ex = jnp.exp(lbl - jnp.max(lbl, axis=0, keepdims=True))
    lb = ex[0:1, :] / jnp.sum(ex, axis=0, keepdims=True)
    gn = gn_ref[...]
    head_lanes = [slice(h * HG_DIM, (h + 1) * HG_DIM) for h in range(HG_HEADS)]

    @pl.when(t == 0)
    def _():
        for h, sl in enumerate(head_lanes):
            b, kk = _hg_decay(mf_ref[:, sl], lb[:, sl], sum_w[:HG_CHUNK], HG_CHUNK - N_META)
            state_ref[h] = _hg_next_state(jnp.zeros((HG_DIM, HG_DIM), F32), b, kk, mv_ref[:, sl])

    masks = _hg_score_masks()

    def body(c, carry):
        base = c * (HG_UNROLL * HG_CHUNK)
        rows = [pl.ds(pl.multiple_of(base + u * HG_CHUNK, HG_CHUNK), HG_CHUNK)
                for u in range(HG_UNROLL)]
        states = [state_ref[h] for h in range(HG_HEADS)]

        def decay(r):
            return [_hg_decay(f_ref[r, sl], lb[:, sl], sum_w, 0) for sl in head_lanes]

        def mix(r, dec):
            out = []
            for h, sl in enumerate(head_lanes):
                sums, kk = dec[h]
                b = sums[:HG_CHUNK]
                q = q_ref[r, sl].astype(F32)
                qf = q * _sigmoid(q) * HG_SCALE
                scores = _hg_scores(qf, kk, sums, masks)
                inter = lax.dot_general((qf * jnp.exp(b)).astype(BF16), states[h].astype(BF16),
                                        _NT, preferred_element_type=F32)
                states[h] = _hg_next_state(states[h], b, kk, v_ref[r, sl])
                out.append((scores, inter))
            return out

        def emit(r, mixed):
            for h, sl in enumerate(head_lanes):
                scores, inter = mixed[h]
                o = inter + jnp.dot(scores.astype(BF16), v_ref[r, sl], preferred_element_type=F32)
                o = o * lax.rsqrt(jnp.mean(o * o, axis=-1, keepdims=True) + EPS) * gn[:, sl]
                gate = gate_ref[r, sl].astype(F32)
                o_ref[r, sl] = (o * (gate * _sigmoid(gate))).astype(BF16)

        stages = (decay, mix, emit)
        held = [None] * len(rows)
        for k in range(len(rows) + len(stages) - 1):
            for s in reversed(range(len(stages))):
                i = k - s
                if 0 <= i < len(rows):
                    held[i] = stages[s](rows[i]) if s == 0 else stages[s](rows[i], held[i])
        for h in range(HG_HEADS):
            state_ref[h] = states[h]
        return carry

    lax.fori_loop(0, chunks_per_tile // HG_UNROLL, body, 0)


def _hgrn2(proj, f_proj, proj_meta, f_meta, lb_logits, hg_norm_g, sum_w, batch, seq, row_tile):
    assert seq % row_tile == 0 and row_tile % HG_CHUNK == 0
    proj3 = proj.reshape(batch, seq, IN_COLS)
    f3 = f_proj.reshape(batch, seq, HG_WIDTH)
    meta_blk = META_ROWS // HG_CHUNK - 1
    const = lambda b, t: (0, 0)
    return pl.pallas_call(
        functools.partial(_hg_kernel, chunks_per_tile=row_tile // HG_CHUNK),
        grid=(batch, seq // row_tile),
        in_specs=[
            pl.BlockSpec((None, row_tile, HG_WIDTH), lambda b, t: (b, t, T_HG_Q)),
            pl.BlockSpec((None, row_tile, HG_WIDTH), lambda b, t: (b, t, 0)),
            pl.BlockSpec((None, row_tile, HG_WIDTH), lambda b, t: (b, t, T_HG_I)),
            pl.BlockSpec((None, row_tile, HG_WIDTH), lambda b, t: (b, t, T_HG_GATE)),
            pl.BlockSpec((HG_CHUNK, HG_WIDTH), lambda b, t: (meta_blk, 0)),
            pl.BlockSpec((HG_CHUNK, HG_WIDTH), lambda b, t: (meta_blk, T_HG_I)),
            pl.BlockSpec(lb_logits.shape, const),
            pl.BlockSpec((1, HG_WIDTH), const),
            pl.BlockSpec(sum_w.shape, const),
        ],
        out_specs=pl.BlockSpec((None, row_tile, HG_WIDTH), lambda b, t: (b, t, 0)),
        out_shape=jax.ShapeDtypeStruct((batch, seq, HG_WIDTH), BF16),
        scratch_shapes=[pltpu.VMEM((HG_HEADS, HG_DIM, HG_DIM), F32)],
        compiler_params=pltpu.CompilerParams(
            dimension_semantics=("parallel", "arbitrary"),
            vmem_limit_bytes=VMEM_LIMIT_BYTES),
        name="hgrn2",
    )(proj3, f3, proj3, proj3, f_meta, proj_meta, lb_logits, hg_norm_g, sum_w)


MERGE_SLAB = 256


def _final_kernel(ysb_ref, yhg_ref, gsb_ref, ghg_ref, x_ref, wsb_ref, whg_ref, wout_ref, fg_ref,
                  o_ref, wsb_s, whg_s, wout_s):
    @pl.when(pl.program_id(0) == 0)
    def _():
        wsb_s[...] = wsb_ref[...].astype(BF16)
        whg_s[...] = whg_ref[...].astype(BF16)
        wout_s[...] = wout_ref[...].astype(BF16)

    n_slab = o_ref.shape[0] // MERGE_SLAB
    slabs = [pl.ds(s * MERGE_SLAB, MERGE_SLAB) for s in range(n_slab)]
    a = [jnp.dot(ysb_ref[sl, :], wsb_s[...], preferred_element_type=F32) for sl in slabs]
    b = [jnp.dot(yhg_ref[sl, :], whg_s[...], preferred_element_type=F32) for sl in slabs]
    merged = [(_sigmoid(gsb_ref[sl, :].astype(F32)) * a[s]
               + _sigmoid(ghg_ref[sl, :].astype(F32)) * b[s]).astype(BF16)
              for s, sl in enumerate(slabs)]
    h = [x_ref[sl, :] + jnp.dot(merged[s], wout_s[...], preferred_element_type=F32)
         for s, sl in enumerate(slabs)]
    for s, sl in enumerate(slabs):
        ms = jnp.mean(h[s] * h[s], axis=-1, keepdims=True)
        o_ref[sl, :] = h[s] * lax.rsqrt(ms + EPS) * fg_ref[...]


def _final(y_sb, y_hg, proj, x2d, w_sb_out, w_hg_out, w_out, final_norm_g, row_tile):
    rows = x2d.shape[0]
    assert rows % row_tile == 0
    gate_blk = (4 * SB_WIDTH + 4 * HG_WIDTH) // D_MODEL
    const = lambda r: (0, 0)
    return pl.pallas_call(
        _final_kernel,
        grid=(rows // row_tile,),
        in_specs=[
            pl.BlockSpec((row_tile, SB_WIDTH), lambda r: (r, 0)),
            pl.BlockSpec((row_tile, HG_WIDTH), lambda r: (r, 0)),
            pl.BlockSpec((row_tile, D_MODEL), lambda r: (r, gate_blk)),
            pl.BlockSpec((row_tile, D_MODEL), lambda r: (r, gate_blk + 1)),
            pl.BlockSpec((row_tile, D_MODEL), lambda r: (r, 0)),
            pl.BlockSpec((SB_WIDTH, D_MODEL), const),
            pl.BlockSpec((HG_WIDTH, D_MODEL), const),
            pl.BlockSpec((D_MODEL, D_MODEL), const),
            pl.BlockSpec((1, D_MODEL), const),
        ],
        out_specs=pl.BlockSpec((row_tile, D_MODEL), lambda r: (r, 0)),
        out_shape=jax.ShapeDtypeStruct((rows, D_MODEL), F32),
        scratch_shapes=[pltpu.VMEM((SB_WIDTH, D_MODEL), BF16),
                        pltpu.VMEM((HG_WIDTH, D_MODEL), BF16),
                        pltpu.VMEM((D_MODEL, D_MODEL), BF16)],
        compiler_params=pltpu.CompilerParams(
            dimension_semantics=("arbitrary",),
            vmem_limit_bytes=VMEM_LIMIT_BYTES),
        name="merge_out",
    )(y_sb, y_hg, proj, proj, x2d, w_sb_out, w_hg_out, w_out, final_norm_g)


def _suffix_sum_weights():
    j = np.arange(2 * KEY_TILE)[:, None] % KEY_TILE
    s = np.arange(2 * KEY_TILE)[None, :]
    return jnp.asarray(np.where(s < KEY_TILE, j > s, True), BF16)


def _hg_sum_weights():
    r = np.arange(HG_CHUNK)[:, None]
    c = np.arange(HG_CHUNK)[None, :]
    groups = [c <= r]
    for m in HG_LEVELS:
        ref = (r // (2 * m)) * 2 * m + m - 1
        groups.append(np.where(r > ref, (c > ref) & (c <= r), (c > r) & (c <= ref)))
    w = np.concatenate(groups, axis=0)
    return jnp.asarray(np.concatenate([w, w], axis=1), BF16)


def kernel(x, meta, norm_g, w_in, w_sb_out, w_hg_out, w_out, hg_norm_g, hg_lb_logits, final_norm_g):
    batch, seq, d = x.shape
    assert d == D_MODEL and meta.shape == (N_META, D_MODEL)
    assert norm_g.shape[0] == 1 and w_in.shape == (1, D_MODEL, IN_COLS)
    assert seq % HG_ROWS == 0 and (batch * seq) % INPROJ_ROWS == 0

    x2d = x.reshape(batch * seq, D_MODEL)
    meta_blk = jnp.concatenate(
        [jnp.zeros((META_ROWS - N_META, D_MODEL), x.dtype), meta.astype(x.dtype)], axis=0)
    w_in2 = w_in.reshape(D_MODEL, IN_COLS)

    proj, f_proj, proj_meta, f_meta = _inproj(x2d, meta_blk, norm_g, w_in2, INPROJ_ROWS)

    y_sb = _stick_breaking(proj, proj_meta, _suffix_sum_weights(), batch, seq)
    y_hg = _hgrn2(proj, f_proj, proj_meta, f_meta, hg_lb_logits, hg_norm_g,
                  _hg_sum_weights(), batch, seq, HG_ROWS)

    out = _final(y_sb.reshape(batch * seq, SB_WIDTH), y_hg.reshape(batch * seq, HG_WIDTH), proj,
                 x2d, w_sb_out.reshape(SB_WIDTH, D_MODEL), w_hg_out.reshape(HG_WIDTH, D_MODEL),
                 w_out.reshape(D_MODEL, D_MODEL), final_norm_g.reshape(1, D_MODEL), MERGE_ROWS)
    return out.reshape(batch, seq, D_MODEL)
```

```python
import functools

import jax
import jax.numpy as jnp
import numpy as np
from jax import lax
from jax.experimental import pallas as pl
from jax.experimental.pallas import tpu as pltpu

F32 = jnp.float32
BF16 = jnp.bfloat16

D_MODEL = 1024
N_META = 16
SB_HEADS = 8
SB_HEAD_DIM = 64
SB_WIDTH = SB_HEADS * SB_HEAD_DIM
SB_SCALE = SB_HEAD_DIM ** -0.5
HG_HEADS = 4
HG_DIM = 128
HG_WIDTH = HG_HEADS * HG_DIM
HG_SCALE = HG_DIM ** -0.5
IN_COLS = 4 * SB_WIDTH + 4 * HG_WIDTH + 2 * D_MODEL
EPS = 1e-6

LANES = 128
SUBLANES = 8
VMEM_LIMIT_BYTES = 56 * 1024 * 1024

INPROJ_ROWS = 2048
HG_ROWS = 2048
MERGE_ROWS = 1024

COL_TILE = 512
T_SB_Q, T_SB_K, T_SB_V, T_SB_GATE, T_HG_Q, T_HG_F, T_HG_I, T_HG_GATE = range(8)

KEY_TILE = 128
META_ROWS = 128
HG_CHUNK = 64
HG_LEVELS = (1, 2, 4, 8, 16, 32)
HG_UNROLL = 32


_NT = (((1,), (1,)), ((), ()))
_TN = (((0,), (0,)), ((), ()))


def _sigmoid(x):
    return 1.0 / (1.0 + jnp.exp(-x))


def _split_bf16(x):
    hi = x.astype(BF16)
    lo = (x - hi.astype(F32)).astype(BF16)
    return hi, lo


INPROJ_COLS = 2 * COL_TILE
INPROJ_SLAB = 512
assert (T_SB_Q * COL_TILE) % INPROJ_COLS == 0


def _inproj_kernel(x_ref, m_ref, g_ref, w_ref, o_ref, f_ref, om_ref, fm_ref, u_ref, um_ref):
    r = pl.program_id(0)
    n = pl.program_id(1)
    f_step, f_col = divmod(T_HG_F * COL_TILE, INPROJ_COLS)
    assert f_step != 0

    def weights():
        lane = lax.broadcasted_iota(jnp.int32, (1, INPROJ_COLS), 1)
        is_q = jnp.logical_and(n == (T_SB_Q * COL_TILE) // INPROJ_COLS, lane < SB_WIDTH)
        return (w_ref[...] * jnp.where(is_q, SB_SCALE, 1.0).astype(F32)).astype(BF16)

    def normalise(x):
        ms = jnp.mean(x * x, axis=-1, keepdims=True)
        return (x * lax.rsqrt(ms + EPS) * g_ref[...]).astype(BF16)

    @pl.when(n == 0)
    def _():
        wb = weights()
        slab = min(INPROJ_SLAB, x_ref.shape[0])
        for s in range(x_ref.shape[0] // slab):
            rows = pl.ds(s * slab, slab)
            u = normalise(x_ref[rows, :])
            u_ref[rows, :] = u
            o_ref[rows, :] = jnp.dot(u, wb, preferred_element_type=F32).astype(BF16)

    @pl.when(n != 0)
    def _():
        p = jnp.dot(u_ref[...], weights(), preferred_element_type=F32)
        o_ref[...] = p.astype(BF16)

        @pl.when(n == f_step)
        def _():
            f_ref[...] = p[:, f_col:f_col + HG_WIDTH]

    @pl.when(r == 0)
    def _():
        @pl.when(n == 0)
        def _():
            um_ref[...] = jnp.zeros_like(um_ref)
            um_ref[META_ROWS - N_META:, :] = normalise(m_ref[...])

        pm = jnp.dot(um_ref[...], weights(), preferred_element_type=F32)
        om_ref[...] = pm.astype(BF16)

        @pl.when(n == f_step)
        def _():
            fm_ref[...] = pm[:, f_col:f_col + HG_WIDTH]


def _inproj(x2d, meta, norm_g, w_in, row_tile):
    rows = x2d.shape[0]
    n_steps = IN_COLS // INPROJ_COLS
    assert rows % row_tile == 0 and meta.shape == (N_META, D_MODEL)
    return pl.pallas_call(
        _inproj_kernel,
        grid=(rows // row_tile, n_steps),
        in_specs=[
            pl.BlockSpec((row_tile, D_MODEL), lambda r, n: (r, 0)),
            pl.BlockSpec((N_META, D_MODEL), lambda r, n: (0, 0)),
            pl.BlockSpec((1, D_MODEL), lambda r, n: (0, 0)),
            pl.BlockSpec((D_MODEL, INPROJ_COLS), lambda r, n: (0, n)),
        ],
        out_specs=[
            pl.BlockSpec((row_tile, INPROJ_COLS), lambda r, n: (r, n)),
            pl.BlockSpec((row_tile, HG_WIDTH), lambda r, n: (r, 0)),
            pl.BlockSpec((META_ROWS, INPROJ_COLS), lambda r, n: (0, jnp.where(r == 0, n, n_steps - 1))),
            pl.BlockSpec((META_ROWS, HG_WIDTH), lambda r, n: (0, 0)),
        ],
        out_shape=[
            jax.ShapeDtypeStruct((rows, IN_COLS), BF16),
            jax.ShapeDtypeStruct((rows, HG_WIDTH), F32),
            jax.ShapeDtypeStruct((META_ROWS, IN_COLS), BF16),
            jax.ShapeDtypeStruct((META_ROWS, HG_WIDTH), F32),
        ],
        scratch_shapes=[pltpu.VMEM((row_tile, D_MODEL), BF16),
                        pltpu.VMEM((META_ROWS, D_MODEL), BF16)],
        compiler_params=pltpu.CompilerParams(
            dimension_semantics=("arbitrary", "arbitrary"),
            vmem_limit_bytes=VMEM_LIMIT_BYTES),
        name="inproj",
    )(x2d, meta, norm_g, w_in)


SB_PAIRS = SB_WIDTH // LANES
SB_UNDERFLOW = -104.0
SB_QTILES = 8
SB_EARLY = 48


def _sb_kernel(q_ref, k_ref, v_ref, g_ref, mk_ref, mv_ref, uw_ref, o_ref, acc_ref, carry_ref):
    j = pl.program_id(1)
    lane = lax.broadcasted_iota(jnp.int32, (KEY_TILE, LANES), 1)
    first = lane < SB_HEAD_DIM
    uw = uw_ref[...]
    pair_lanes = [slice(p * LANES, (p + 1) * LANES) for p in range(SB_PAIRS)]
    q_rows = [slice(t * KEY_TILE, (t + 1) * KEY_TILE) for t in range(SB_QTILES)]
    q2 = []
    for qr in q_rows:
        for sl in pair_lanes:
            q = q_ref[qr, sl]
            zero = jnp.zeros_like(q)
            q2.append(jnp.concatenate([jnp.where(first, q, zero), jnp.where(first, zero, q)], axis=0))

    def early(x):
        return jnp.concatenate([x[:SB_EARLY], x[KEY_TILE:KEY_TILE + SB_EARLY]], axis=0)

    def with_early(x, e, combine):
        return jnp.concatenate([combine(x[:SB_EARLY], e[:SB_EARLY]), x[SB_EARLY:KEY_TILE],
                                combine(x[KEY_TILE:KEY_TILE + SB_EARLY], e[SB_EARLY:]),
                                x[KEY_TILE + SB_EARLY:]], axis=0)

    def log_terms(z, valid):
        log_beta = jnp.minimum(z, 0.0) - jnp.log(1.0 + jnp.exp(-jnp.abs(z)))
        log_1mb = log_beta - z
        if valid is not None:
            log_1mb = jnp.where(valid, log_1mb, 0.0)
        hi, lo = _split_bf16(log_1mb)
        return log_beta, jnp.concatenate([hi, lo], axis=1)

    def sweep(kt_ref, vt_ref, jobs, first_pass=False):
        chains = [(t * SB_PAIRS + p, sl) + job
                  for t, job in enumerate(jobs) for p, sl in enumerate(pair_lanes)]

        def scores(chain):
            c, sl, rows, _, erows = chain
            z = lax.dot_general(q2[c], kt_ref[rows, sl], _NT, preferred_element_type=F32)
            ze = None if erows is None else lax.dot_general(
                early(q2[c]), kt_ref[erows, sl], _NT, preferred_element_type=F32)
            return z, ze

        def suffix_sums(chain, z, ze):
            valids = chain[3]
            terms = [] if ze is None else [log_terms(ze, None)]
            terms += [log_terms(z[:, n * KEY_TILE:(n + 1) * KEY_TILE], valid)
                      for n, valid in enumerate(valids)]
            sums = jnp.dot(jnp.concatenate([hl for _, hl in terms], axis=0), uw,
                           preferred_element_type=F32)
            return [lb for lb, _ in terms], sums

        def weights(chain, log_betas, sums):
            c, _, _, valids, erows = chain
            off = 0 if erows is None else 2 * SB_EARLY
            lbs = log_betas if erows is None else log_betas[1:]
            carry = None if first_pass else carry_ref[c]
            w_n = [None] * len(valids)
            for n in reversed(range(len(valids))):
                cs = sums[off + n * 2 * KEY_TILE:off + (n + 1) * 2 * KEY_TILE]
                log_w = lbs[n] + cs[:, :KEY_TILE]
                w = jnp.exp(log_w if carry is None else log_w + carry)
                if valids[n] is not None:
                    w = jnp.where(valids[n], w, 0.0)
                w_n[n] = w.astype(BF16)
                carry = cs[:, KEY_TILE:] if carry is None else carry + cs[:, KEY_TILE:]
            we = None
            if erows is not None:
                cs = sums[:off]
                we = jnp.exp(log_betas[0] + cs[:, :KEY_TILE] + early(carry)).astype(BF16)
                carry = with_early(carry, cs[:, KEY_TILE:], jnp.add)
            carry_ref[c] = carry
            return jnp.concatenate(w_n, axis=1), we

        def accumulate(chain, w, we):
            c, sl, rows, _, erows = chain
            pv = jnp.dot(w, vt_ref[rows, sl], preferred_element_type=F32)
            if erows is not None:
                pv = with_early(pv, jnp.dot(we, vt_ref[erows, sl], preferred_element_type=F32),
                                jnp.add)
            if first_pass:
                acc_ref[c] = pv
            else:
                acc_ref[c] += pv

        stages = (scores, suffix_sums, weights, accumulate)
        held = [None] * len(chains)
        for k in range(len(chains) + len(stages) - 1):
            for s, stage in reversed(list(enumerate(stages))):
                i = k - s
                if 0 <= i < len(chains):
                    held[i] = stage(chains[i]) if s == 0 else stage(chains[i], *held[i])

    row2 = lax.broadcasted_iota(jnp.int32, (2 * KEY_TILE, KEY_TILE), 0) % KEY_TILE
    col2 = lax.broadcasted_iota(jnp.int32, (2 * KEY_TILE, KEY_TILE), 1)
    causal = col2 < row2

    def key_rows(tile, n_tiles):
        return pl.ds(pl.multiple_of(tile * KEY_TILE, KEY_TILE), n_tiles * KEY_TILE)

    i_first = j * SB_QTILES
    fused = j >= 1
    assert SB_QTILES >= 3

    def standard(i):
        return key_rows(i - 1, 2), [None, causal], key_rows(i - 2, 1)

    @pl.when(fused)
    def _():
        sweep(k_ref, v_ref, [standard(i_first + t) for t in range(SB_QTILES)], first_pass=True)

    @pl.when(jnp.logical_not(fused))
    def _():
        sweep(k_ref, v_ref, [(key_rows(0, 1), [causal], None), (key_rows(0, 2), [None, causal], None)]
              + [standard(t) for t in range(2, SB_QTILES)], first_pass=True)

    left = [jnp.maximum(i_first + t - 2, 0) for t in range(SB_QTILES)]
    is_standard = [fused if t < 2 else True for t in range(SB_QTILES)]
    late = row2 >= SB_EARLY

    def has_keys(t, it):
        return jnp.where(it < 0, is_standard[t], it < left[t])

    def live_flags(it):
        floor = jnp.full((SUBLANES, LANES), -jnp.inf, F32)
        pending, everyone = floor, floor
        for t in range(SB_QTILES):
            top = jnp.max(carry_ref[t * SB_PAIRS:(t + 1) * SB_PAIRS].reshape(-1, SUBLANES, LANES),
                          axis=0)
            everyone = jnp.maximum(everyone, top)
            pending = jnp.maximum(pending, jnp.where(has_keys(t, it), top, floor))
        return ((jnp.max(pending) > SB_UNDERFLOW).astype(jnp.int32),
                (jnp.max(everyone) > SB_UNDERFLOW).astype(jnp.int32))

    def cond(c):
        return c[1] > 0

    def body(c):
        it = c[0]
        catch_up = it < 0
        jobs = []
        for t in range(SB_QTILES):
            tile = jnp.where(catch_up, left[t], jnp.maximum(left[t] - 1 - it, 0))
            valid = jnp.logical_and(has_keys(t, it),
                                    jnp.logical_or(late, jnp.logical_not(catch_up)))
            jobs.append((key_rows(tile, 1), [valid], None))
        sweep(k_ref, v_ref, jobs)
        return (it + 1,) + live_flags(it + 1)

    _, _, live = lax.while_loop(cond, body, (jnp.int32(-1),) + live_flags(jnp.int32(-1)))

    @pl.when(live > 0)
    def _():
        sweep(mk_ref, mv_ref, [(slice(None), [col2 >= META_ROWS - N_META], None)] * SB_QTILES)

    for t, qr in enumerate(q_rows):
        for p, sl in enumerate(pair_lanes):
            acc = acc_ref[t * SB_PAIRS + p]
            out = jnp.where(first, acc[:KEY_TILE], acc[KEY_TILE:])
            gate = g_ref[qr, sl].astype(F32)
            o_ref[qr, sl] = (out * (gate * _sigmoid(gate))).astype(BF16)


def _stick_breaking(proj, proj_meta, uw, batch, seq):
    assert SB_WIDTH == COL_TILE
    proj3 = proj.reshape(batch, seq, IN_COLS)
    q_blk = SB_QTILES * KEY_TILE
    assert seq % q_blk == 0
    return pl.pallas_call(
        _sb_kernel,
        grid=(batch, seq // q_blk),
        in_specs=[
            pl.BlockSpec((None, q_blk, SB_WIDTH), lambda b, i: (b, i, T_SB_Q)),
            pl.BlockSpec((None, seq, SB_WIDTH), lambda b, i: (b, 0, T_SB_K)),
            pl.BlockSpec((None, seq, SB_WIDTH), lambda b, i: (b, 0, T_SB_V)),
            pl.BlockSpec((None, q_blk, SB_WIDTH), lambda b, i: (b, i, T_SB_GATE)),
            pl.BlockSpec((META_ROWS, SB_WIDTH), lambda b, i: (0, T_SB_K)),
            pl.BlockSpec((META_ROWS, SB_WIDTH), lambda b, i: (0, T_SB_V)),
            pl.BlockSpec((2 * KEY_TILE, 2 * KEY_TILE), lambda b, i: (0, 0)),
        ],
        out_specs=pl.BlockSpec((None, q_blk, SB_WIDTH), lambda b, i: (b, i, 0)),
        out_shape=jax.ShapeDtypeStruct((batch, seq, SB_WIDTH), BF16),
        scratch_shapes=[pltpu.VMEM((SB_QTILES * SB_PAIRS, 2 * KEY_TILE, LANES), F32),
                        pltpu.VMEM((SB_QTILES * SB_PAIRS, 2 * KEY_TILE, LANES), F32)],
        compiler_params=pltpu.CompilerParams(
            dimension_semantics=("parallel", "arbitrary"),
            vmem_limit_bytes=VMEM_LIMIT_BYTES),
        name="stick_breaking",
    )(proj3, proj3, proj3, proj3, proj_meta, proj_meta, uw)


def _hg_decay(f_logit, lbv, sum_w, pad_rows):
    f = lbv + (1.0 - lbv) * _sigmoid(f_logit)
    if pad_rows:
        row = lax.broadcasted_iota(jnp.int32, f.shape, 0)
        f = jnp.where(row >= pad_rows, f, 1.0)
    g_hi, g_lo = _split_bf16(jnp.log(f))
    sums = jnp.dot(sum_w, jnp.concatenate([g_hi, g_lo], axis=0), preferred_element_type=F32)
    return sums, 1.0 - f


def _hg_next_state(state_t, b, kk, v):
    b_last = b[HG_CHUNK - 1:HG_CHUNK, :]
    k_dec = (kk * jnp.exp(b_last - b)).astype(BF16)
    return state_t * jnp.exp(b_last) + lax.dot_general(v, k_dec, _TN, preferred_element_type=F32)


def _hg_scores(qf, kk, sums, masks):
    a = jnp.where(masks[0], lax.dot_general(qf.astype(BF16), kk.astype(BF16), _NT,
                                            preferred_element_type=F32), 0.0)
    for lvl in range(len(HG_LEVELS)):
        e = jnp.exp(sums[(1 + lvl) * HG_CHUNK:(2 + lvl) * HG_CHUNK])
        part = lax.dot_general((qf * e).astype(BF16), (kk * e).astype(BF16), _NT,
                               preferred_element_type=F32)
        a = jnp.where(masks[1 + lvl], part, a)
    return a


def _hg_score_masks():
    row = lax.broadcasted_iota(jnp.int32, (HG_CHUNK, HG_CHUNK), 0)
    col = lax.broadcasted_iota(jnp.int32, (HG_CHUNK, HG_CHUNK), 1)
    return [row == col] + [
        ((row // (2 * m)) == (col // (2 * m))) & ((row // m) % 2 == 1) & ((col // m) % 2 == 0)
        for m in HG_LEVELS]


def _hg_kernel(q_ref, f_ref, v_ref, gate_ref, mf_ref, mv_ref, lbl_ref, gn_ref,
               sumw_ref, o_ref, state_ref, *, chunks_per_tile):
    t = pl.program_id(1)
    sum_w = sumw_ref[...]
    lbl = lbl_ref[...]
    ex = jnp.exp(lbl - jnp.max(lbl, axis=0, keepdims=True))
    lb = ex[0:1, :] / jnp.sum(ex, axis=0, keepdims=True)
    gn = gn_ref[...]
    head_lanes = [slice(h * HG_DIM, (h + 1) * HG_DIM) for h in range(HG_HEADS)]

    @pl.when(t == 0)
    def _():
        for h, sl in enumerate(head_lanes):
            b, kk = _hg_decay(mf_ref[:, sl], lb[:, sl], sum_w[:HG_CHUNK], HG_CHUNK - N_META)
            state_ref[h] = _hg_next_state(jnp.zeros((HG_DIM, HG_DIM), F32), b, kk, mv_ref[:, sl])

    masks = _hg_score_masks()

    def body(c, carry):
        base = c * (HG_UNROLL * HG_CHUNK)
        rows = [pl.ds(pl.multiple_of(base + u * HG_CHUNK, HG_CHUNK), HG_CHUNK)
                for u in range(HG_UNROLL)]
        states = [state_ref[h] for h in range(HG_HEADS)]

        def decay(r):
            return [_hg_decay(f_ref[r, sl], lb[:, sl], sum_w, 0) for sl in head_lanes]

        def mix(r, dec):
            out = []
            for h, sl in enumerate(head_lanes):
                sums, kk = dec[h]
                b = sums[:HG_CHUNK]
                q = q_ref[r, sl].astype(F32)
                qf = q * _sigmoid(q) * HG_SCALE
                scores = _hg_scores(qf, kk, sums, masks)
                inter = lax.dot_general((qf * jnp.exp(b)).astype(BF16), states[h].astype(BF16),
                                        _NT, preferred_element_type=F32)
                states[h] = _hg_next_state(states[h], b, kk, v_ref[r, sl])
                out.append((scores, inter))
            return out

        def emit(r, mixed):
            for h, sl in enumerate(head_lanes):
                scores, inter = mixed[h]
                o = inter + jnp.dot(scores.astype(BF16), v_ref[r, sl], preferred_element_type=F32)
                o = o * lax.rsqrt(jnp.mean(o * o, axis=-1, keepdims=True) + EPS) * gn[:, sl]
                gate = gate_ref[r, sl].astype(F32)
                o_ref[r, sl] = (o * (gate * _sigmoid(gate))).astype(BF16)

        stages = (decay, mix, emit)
        held = [None] * len(rows)
        for k in range(len(rows) + len(stages) - 1):
            for s in reversed(range(len(stages))):
                i = k - s
                if 0 <= i < len(rows):
                    held[i] = stages[s](rows[i]) if s == 0 else stages[s](rows[i], held[i])
        for h in range(HG_HEADS):
            state_ref[h] = states[h]
        return carry

    lax.fori_loop(0, chunks_per_tile // HG_UNROLL, body, 0)


def _hgrn2(proj, f_proj, proj_meta, f_meta, lb_logits, hg_norm_g, sum_w, batch, seq, row_tile):
    assert seq % row_tile == 0 and row_tile % HG_CHUNK == 0
    proj3 = proj.reshape(batch, seq, IN_COLS)
    f3 = f_proj.reshape(batch, seq, HG_WIDTH)
    meta_blk = META_ROWS // HG_CHUNK - 1
    const = lambda b, t: (0, 0)
    return pl.pallas_call(
        functools.partial(_hg_kernel, chunks_per_tile=row_tile // HG_CHUNK),
        grid=(batch, seq // row_tile),
        in_specs=[
            pl.BlockSpec((None, row_tile, HG_WIDTH), lambda b, t: (b, t, T_HG_Q)),
            pl.BlockSpec((None, row_tile, HG_WIDTH), lambda b, t: (b, t, 0)),
            pl.BlockSpec((None, row_tile, HG_WIDTH), lambda b, t: (b, t, T_HG_I)),
            pl.BlockSpec((None, row_tile, HG_WIDTH), lambda b, t: (b, t, T_HG_GATE)),
            pl.BlockSpec((HG_CHUNK, HG_WIDTH), lambda b, t: (meta_blk, 0)),
            pl.BlockSpec((HG_CHUNK, HG_WIDTH), lambda b, t: (meta_blk, T_HG_I)),
            pl.BlockSpec(lb_logits.shape, const),
            pl.BlockSpec((1, HG_WIDTH), const),
            pl.BlockSpec(sum_w.shape, const),
        ],
        out_specs=pl.BlockSpec((None, row_tile, HG_WIDTH), lambda b, t: (b, t, 0)),
        out_shape=jax.ShapeDtypeStruct((batch, seq, HG_WIDTH), BF16),
        scratch_shapes=[pltpu.VMEM((HG_HEADS, HG_DIM, HG_DIM), F32)],
        compiler_params=pltpu.CompilerParams(
            dimension_semantics=("parallel", "arbitrary"),
            vmem_limit_bytes=VMEM_LIMIT_BYTES),
        name="hgrn2",
    )(proj3, f3, proj3, proj3, f_meta, proj_meta, lb_logits, hg_norm_g, sum_w)


MERGE_SLAB = 256


def _final_kernel(ysb_ref, yhg_ref, gsb_ref, ghg_ref, x_ref, wsb_ref, whg_ref, wout_ref, fg_ref,
                  o_ref, wsb_s, whg_s, wout_s):
    @pl.when(pl.program_id(0) == 0)
    def _():
        wsb_s[...] = wsb_ref[...].astype(BF16)
        whg_s[...] = whg_ref[...].astype(BF16)
        wout_s[...] = wout_ref[...].astype(BF16)

    n_slab = o_ref.shape[0] // MERGE_SLAB
    slabs = [pl.ds(s * MERGE_SLAB, MERGE_SLAB) for s in range(n_slab)]
    a = [jnp.dot(ysb_ref[sl, :], wsb_s[...], preferred_element_type=F32) for sl in slabs]
    b = [jnp.dot(yhg_ref[sl, :], whg_s[...], preferred_element_type=F32) for sl in slabs]
    merged = [(_sigmoid(gsb_ref[sl, :].astype(F32)) * a[s]
               + _sigmoid(ghg_ref[sl, :].astype(F32)) * b[s]).astype(BF16)
              for s, sl in enumerate(slabs)]
    h = [x_ref[sl, :] + jnp.dot(merged[s], wout_s[...], preferred_element_type=F32)
         for s, sl in enumerate(slabs)]
    for s, sl in enumerate(slabs):
        ms = jnp.mean(h[s] * h[s], axis=-1, keepdims=True)
        o_ref[sl, :] = h[s] * lax.rsqrt(ms + EPS) * fg_ref[...]


def _final(y_sb, y_hg, proj, x2d, w_sb_out, w_hg_out, w_out, final_norm_g, row_tile):
    rows = x2d.shape[0]
    assert rows % row_tile == 0
    gate_blk = (4 * SB_WIDTH + 4 * HG_WIDTH) // D_MODEL
    const = lambda r: (0, 0)
    return pl.pallas_call(
        _final_kernel,
        grid=(rows // row_tile,),
        in_specs=[
            pl.BlockSpec((row_tile, SB_WIDTH), lambda r: (r, 0)),
            pl.BlockSpec((row_tile, HG_WIDTH), lambda r: (r, 0)),
            pl.BlockSpec((row_tile, D_MODEL), lambda r: (r, gate_blk)),
            pl.BlockSpec((row_tile, D_MODEL), lambda r: (r, gate_blk + 1)),
            pl.BlockSpec((row_tile, D_MODEL), lambda r: (r, 0)),
            pl.BlockSpec((SB_WIDTH, D_MODEL), const),
            pl.BlockSpec((HG_WIDTH, D_MODEL), const),
            pl.BlockSpec((D_MODEL, D_MODEL), const),
            pl.BlockSpec((1, D_MODEL), const),
        ],
        out_specs=pl.BlockSpec((row_tile, D_MODEL), lambda r: (r, 0)),
        out_shape=jax.ShapeDtypeStruct((rows, D_MODEL), F32),
        scratch_shapes=[pltpu.VMEM((SB_WIDTH, D_MODEL), BF16),
                        pltpu.VMEM((HG_WIDTH, D_MODEL), BF16),
                        pltpu.VMEM((D_MODEL, D_MODEL), BF16)],
        compiler_params=pltpu.CompilerParams(
            dimension_semantics=("arbitrary",),
            vmem_limit_bytes=VMEM_LIMIT_BYTES),
        name="merge_out",
    )(y_sb, y_hg, proj, proj, x2d, w_sb_out, w_hg_out, w_out, final_norm_g)


def _suffix_sum_weights():
    j = np.arange(2 * KEY_TILE)[:, None] % KEY_TILE
    s = np.arange(2 * KEY_TILE)[None, :]
    return jnp.asarray(np.where(s < KEY_TILE, j > s, True), BF16)


def _hg_sum_weights():
    r = np.arange(HG_CHUNK)[:, None]
    c = np.arange(HG_CHUNK)[None, :]
    groups = [c <= r]
    for m in HG_LEVELS:
        ref = (r // (2 * m)) * 2 * m + m - 1
        groups.append(np.where(r > ref, (c > ref) & (c <= r), (c > r) & (c <= ref)))
    w = np.concatenate(groups, axis=0)
    return jnp.asarray(np.concatenate([w, w], axis=1), BF16)


def kernel(x, meta, norm_g, w_in, w_sb_out, w_hg_out, w_out, hg_norm_g, hg_lb_logits, final_norm_g):
    batch, seq, d = x.shape
    assert d == D_MODEL and meta.shape == (N_META, D_MODEL)
    assert norm_g.shape[0] == 1 and w_in.shape == (1, D_MODEL, IN_COLS)
    assert seq % HG_ROWS == 0 and (batch * seq) % INPROJ_ROWS == 0

    x2d = x.reshape(batch * seq, D_MODEL)
    w_in2 = w_in.reshape(D_MODEL, IN_COLS)

    proj, f_proj, proj_meta, f_meta = _inproj(x2d, meta.astype(x.dtype), norm_g, w_in2, INPROJ_ROWS)

    y_sb = _stick_breaking(proj, proj_meta, _suffix_sum_weights(), batch, seq)
    y_hg = _hgrn2(proj, f_proj, proj_meta, f_meta, hg_lb_logits, hg_norm_g,
                  _hg_sum_weights(), batch, seq, HG_ROWS)

    out = _final(y_sb.reshape(batch * seq, SB_WIDTH), y_hg.reshape(batch * seq, HG_WIDTH), proj,
                 x2d, w_sb_out.reshape(SB_WIDTH, D_MODEL), w_hg_out.reshape(HG_WIDTH, D_MODEL),
                 w_out.reshape(D_MODEL, D_MODEL), final_norm_g.reshape(1, D_MODEL), MERGE_ROWS)
    return out.reshape(batch, seq, D_MODEL)
```
